```python
import jax, jax.numpy as jnp
from jax import lax
import numpy as np

D_MODEL = 2048
BATCH = 4
SEQ = 4096
DEPTH = 1

CHUNK = 64
LRU_WIDTH = 1024
LRU_BLOCKS = 16
LRU_BLOCK_W = LRU_WIDTH // LRU_BLOCKS
CONV_WIDTH = 4
RG_C = 8.0
RWKV_WIDTH = D_MODEL - LRU_WIDTH
HEAD_SIZE = 64
RWKV_HEADS = RWKV_WIDTH // HEAD_SIZE
DECAY_LORA = 96
AAA_LORA = 96
GATE_LORA = 256
RWKV_SHIFT_W = 3 * RWKV_WIDTH + DECAY_LORA + AAA_LORA + GATE_LORA
IN_PROJ_W = 2 * LRU_WIDTH + RWKV_SHIFT_W
N_EXPERTS = 32
TOP_K = 4
D_EXPERT = D_MODEL
SWIGLU_LIMIT = 7.0
SWIGLU_ALPHA = 1.702
MOE_BLOCK = 128
LN_EPS = 1e-5
GN_EPS = HEAD_SIZE * 1e-5
DEEPNORM_ALPHA = (2.0 * DEPTH) ** 0.25
DEEPNORM_BETA = (8.0 * DEPTH) ** -0.25

kernel_name = 'hybrid_rglru_rwkv7_moe_deepnorm'


def _layer_norm(x, g, b):
    xf = x.astype(jnp.float32)
    mu = jnp.mean(xf, axis=-1, keepdims=True)
    var = jnp.mean(jnp.square(xf - mu), axis=-1, keepdims=True)
    return ((xf - mu) * lax.rsqrt(var + LN_EPS) * g + b).astype(x.dtype)


def _rglru_branch(u, gate_in, conv_w, conv_b, w_rgate, b_rgate, w_igate, b_igate, lru_lambda):
    bsz, t, c = u.shape
    f32 = jnp.float32
    uc = lax.conv_general_dilated(u, conv_w[:, None, :], window_strides=(1,), padding=[(CONV_WIDTH - 1, 0)], dimension_numbers=('NWC', 'WIO', 'NWC'), feature_group_count=c) + conv_b
    ub = uc.reshape(bsz, t, LRU_BLOCKS, LRU_BLOCK_W)
    r_gate = jax.nn.sigmoid(jnp.einsum('btnj,njk->btnk', ub, w_rgate).reshape(bsz, t, c) + b_rgate)
    i_gate = jax.nn.sigmoid(jnp.einsum('btnj,njk->btnk', ub, w_igate).reshape(bsz, t, c) + b_igate)
    log_a = (-RG_C * r_gate.astype(f32)) * jax.nn.softplus(-lru_lambda.astype(f32))
    a = jnp.exp(log_a)
    b = jnp.sqrt(-jnp.expm1(2.0 * log_a)) * (i_gate * uc).astype(f32)

    def combine(lhs, rhs):
        a1, b1 = lhs
        a2, b2 = rhs
        return a1 * a2, a2 * b1 + b2

    _, h = lax.associative_scan(combine, (a, b), axis=1)
    return h.astype(u.dtype) * jax.nn.gelu(gate_in)


def _rwkv7_scan(r, w, k, v, kk, a):
    bsz, t, h, n = r.shape
    n_chunks = t // CHUNK

    def to_chunks(z):
        return z.transpose(1, 0, 2, 3).reshape(n_chunks, CHUNK, bsz, h, n)

    def step(s, inp):
        r_t, w_t, k_t, v_t, kk_t, a_t = inp
        s_kk = jnp.einsum('bhvk,bhk->bhv', s, kk_t)
        s = s * w_t[:, :, None, :] - s_kk[..., None] * (kk_t * a_t)[:, :, None, :] + v_t[..., None] * k_t[:, :, None, :]
        return s, jnp.einsum('bhvk,bhk->bhv', s, r_t)

    def chunk_step(s, inp_c):
        return lax.scan(step, s, inp_c)

    s0 = jnp.zeros((bsz, h, n, n), jnp.float32)
    _, y = lax.scan(chunk_step, s0, tuple(to_chunks(z) for z in (r, w, k, v, kk, a)))
    return y.reshape(t, bsz, h, n).transpose(1, 0, 2, 3)


def _rwkv7_branch(p, shift_mu, w0, rw_decay_up, a0, rw_aaa_up, rw_gate_up, k_k, k_a, r_k, gn_g, gn_b):
    bsz, t, _ = p.shape
    f32 = jnp.float32
    p_prev = jnp.pad(p, ((0, 0), (1, 0), (0, 0)))[:, :-1]
    p = p + (p_prev - p) * shift_mu
    splits = [RWKV_WIDTH, 2 * RWKV_WIDTH, 3 * RWKV_WIDTH, 3 * RWKV_WIDTH + DECAY_LORA, 3 * RWKV_WIDTH + DECAY_LORA + AAA_LORA]
    r, k, v, wd, ad, gd = jnp.split(p, splits, axis=-1)
    w_log = -jax.nn.softplus(-(w0 + jnp.tanh(wd) @ rw_decay_up).astype(f32)) - 0.5
    decay = jnp.exp(-jnp.exp(w_log))
    a = jax.nn.sigmoid((a0 + ad @ rw_aaa_up).astype(f32))
    g = (jax.nn.sigmoid(gd) @ rw_gate_up).astype(f32)

    def heads(z):
        return z.reshape(bsz, t, RWKV_HEADS, HEAD_SIZE)

    kk = heads((k * k_k).astype(f32))
    kk = kk / jnp.maximum(jnp.linalg.norm(kk, axis=-1, keepdims=True), 1e-12)
    k = k.astype(f32) * (1.0 + (a - 1.0) * k_a.astype(f32))
    r_h, k_h, v_h = heads(r.astype(f32)), heads(k), heads(v.astype(f32))
    y = _rwkv7_scan(r_h, heads(decay), k_h, v_h, kk, heads(a))
    mu = jnp.mean(y, axis=-1, keepdims=True)
    var = jnp.mean(jnp.square(y - mu), axis=-1, keepdims=True)
    y = (y - mu) * lax.rsqrt(var + GN_EPS) * gn_g.reshape(RWKV_HEADS, HEAD_SIZE) + gn_b.reshape(RWKV_HEADS, HEAD_SIZE)
    y = y + jnp.sum(r_h * k_h * r_k, axis=-1, keepdims=True) * v_h
    return (y.reshape(bsz, t, RWKV_WIDTH) * g).astype(p.dtype)


def _clamped_swiglu(h):
    gate, up = jnp.split(h, 2, axis=-1)
    gate = jnp.minimum(gate, SWIGLU_LIMIT)
    up = jnp.clip(up, -SWIGLU_LIMIT, SWIGLU_LIMIT)
    return gate * jax.nn.sigmoid(SWIGLU_ALPHA * gate) * (up + 1.0)


def _moe(x2, w_router, b_router, w_exp1, b_exp1, w_exp2, b_exp2):
    n = x2.shape[0]
    logits = (x2 @ w_router + b_router).astype(jnp.float32)
    top_v, top_i = lax.top_k(logits, TOP_K)
    gates = jax.nn.softmax(top_v, axis=-1).astype(x2.dtype)
    flat_e = top_i.reshape(-1)
    flat_g = gates.reshape(-1)
    order = jnp.argsort(flat_e)
    sorted_e = flat_e[order]
    sorted_tok = (order // TOP_K).astype(jnp.int32)
    counts = jnp.bincount(flat_e, length=N_EXPERTS)
    padded = ((counts + MOE_BLOCK - 1) // MOE_BLOCK) * MOE_BLOCK
    pad_end = jnp.cumsum(padded)
    pad_start = pad_end - padded
    grp_start = jnp.cumsum(counts) - counts
    dest = pad_start[sorted_e] + (jnp.arange(n * TOP_K) - grp_start[sorted_e])
    n_slots = ((n * TOP_K + MOE_BLOCK - 1) // MOE_BLOCK + N_EXPERTS) * MOE_BLOCK
    n_blocks = n_slots // MOE_BLOCK
    slot_tok = jnp.zeros((n_slots,), jnp.int32).at[dest].set(sorted_tok)
    slot_gate = jnp.zeros((n_slots,), x2.dtype).at[dest].set(flat_g[order])
    blk_e = jnp.minimum(jnp.searchsorted(pad_end, jnp.arange(n_blocks) * MOE_BLOCK, side='right'), N_EXPERTS - 1)

    def expert_block(y, blk):
        tok, gate, e = blk
        h = x2[tok] @ w_exp1[e] + b_exp1[e]
        out = _clamped_swiglu(h) @ w_exp2[e] + b_exp2[e]
        return y.at[tok].add(out * gate[:, None]), None

    y, _ = lax.scan(expert_block, jnp.zeros_like(x2), (slot_tok.reshape(n_blocks, MOE_BLOCK), slot_gate.reshape(n_blocks, MOE_BLOCK), blk_e))
    return y


def setup_inputs(seed: int = 0) -> dict:
    key = jax.random.key(seed)
    ks = iter(jax.random.split(key, 40))
    f32 = jnp.float32
    L = DEPTH

    def nrm(shape, scale):
        return jax.random.normal(next(ks), shape, f32) * scale

    def unif(shape, lo, hi):
        return jax.random.uniform(next(ks), shape, f32, lo, hi)

    a_init = unif((L, LRU_WIDTH), 0.9, 0.999)
    return {
        'x': nrm((BATCH, SEQ, D_MODEL), 1.0),
        'ln_in_g': 1.0 + nrm((D_MODEL,), 0.02),
        'ln_in_b': nrm((D_MODEL,), 0.02),
        'w_in': nrm((L, D_MODEL, IN_PROJ_W), D_MODEL ** -0.5),
        'conv_w': nrm((L, CONV_WIDTH, LRU_WIDTH), CONV_WIDTH ** -0.5),
        'conv_b': nrm((L, LRU_WIDTH), 0.01),
        'w_rgate': nrm((L, LRU_BLOCKS, LRU_BLOCK_W, LRU_BLOCK_W), LRU_BLOCK_W ** -0.5),
        'b_rgate': nrm((L, LRU_WIDTH), 0.01),
        'w_igate': nrm((L, LRU_BLOCKS, LRU_BLOCK_W, LRU_BLOCK_W), LRU_BLOCK_W ** -0.5),
        'b_igate': nrm((L, LRU_WIDTH), 0.01),
        'lru_lambda': jnp.log(a_init) - jnp.log1p(-a_init),
        'shift_mu': unif((L, RWKV_SHIFT_W), 0.0, 1.0),
        'w0': unif((L, RWKV_WIDTH), -6.0, 1.0),
        'rw_decay_up': nrm((L, DECAY_LORA, RWKV_WIDTH), 0.1 * DECAY_LORA ** -0.5),
        'a0': nrm((L, RWKV_WIDTH), 0.1),
        'rw_aaa_up': nrm((L, AAA_LORA, RWKV_WIDTH), 0.1 * AAA_LORA ** -0.5),
        'rw_gate_up': nrm((L, GATE_LORA, RWKV_WIDTH), GATE_LORA ** -0.5),
        'k_k': 0.85 + nrm((L, RWKV_WIDTH), 0.02),
        'k_a': 1.0 + nrm((L, RWKV_WIDTH), 0.02),
        'r_k': nrm((L, RWKV_HEADS, HEAD_SIZE), 0.1),
        'gn_g': 1.0 + nrm((L, RWKV_WIDTH), 0.02),
        'gn_b': nrm((L, RWKV_WIDTH), 0.02),
        'w_out': nrm((L, D_MODEL, D_MODEL), DEEPNORM_BETA * D_MODEL ** -0.5),
        'ln1_g': 1.0 + nrm((L, D_MODEL), 0.02),
        'ln1_b': nrm((L, D_MODEL), 0.02),
        'w_router': nrm((L, D_MODEL, N_EXPERTS), D_MODEL ** -0.5),
        'b_router': nrm((L, N_EXPERTS), 0.01),
        'w_exp1': nrm((L, N_EXPERTS, D_MODEL, 2 * D_EXPERT), D_MODEL ** -0.5),
        'b_exp1': nrm((L, N_EXPERTS, 2 * D_EXPERT), 0.01),
        'w_exp2': nrm((L, N_EXPERTS, D_EXPERT, D_MODEL), DEEPNORM_BETA * D_EXPERT ** -0.5),
        'b_exp2': nrm((L, N_EXPERTS, D_MODEL), 0.01),
        'ln2_g': 1.0 + nrm((L, D_MODEL), 0.02),
        'ln2_b': nrm((L, D_MODEL), 0.02),
    }


def reference(x, ln_in_g, ln_in_b, w_in, conv_w, conv_b, w_rgate, b_rgate, w_igate, b_igate, lru_lambda, shift_mu, w0, rw_decay_up, a0, rw_aaa_up, rw_gate_up, k_k, k_a, r_k, gn_g, gn_b, w_out, ln1_g, ln1_b, w_router, b_router, w_exp1, b_exp1, w_exp2, b_exp2, ln2_g, ln2_b):
    bsz, t, d = x.shape
    h = _layer_norm(x, ln_in_g, ln_in_b)
    for l in range(DEPTH):
        p = h @ w_in[l]
        lru_u, lru_gate, rw = jnp.split(p, [LRU_WIDTH, 2 * LRU_WIDTH], axis=-1)
        y_lru = _rglru_branch(lru_u, lru_gate, conv_w[l], conv_b[l], w_rgate[l], b_rgate[l], w_igate[l], b_igate[l], lru_lambda[l])
        y_rw = _rwkv7_branch(rw, shift_mu[l], w0[l], rw_decay_up[l], a0[l], rw_aaa_up[l], rw_gate_up[l], k_k[l], k_a[l], r_k[l], gn_g[l], gn_b[l])
        mix = jnp.concatenate([y_lru, y_rw], axis=-1) @ w_out[l]
        h = _layer_norm(DEEPNORM_ALPHA * h + mix, ln1_g[l], ln1_b[l])
        moe = _moe(h.reshape(bsz * t, d), w_router[l], b_router[l], w_exp1[l], b_exp1[l], w_exp2[l], b_exp2[l]).reshape(bsz, t, d)
        h = _layer_norm(DEEPNORM_ALPHA * h + moe, ln2_g[l], ln2_b[l])
    return h
```

```python
import functools

import jax
import jax.numpy as jnp
from jax import lax
from jax.experimental import pallas as pl
from jax.experimental.pallas import tpu as pltpu

D_MODEL = 2048
DEPTH = 1
CHUNK = 64
LRU_WIDTH = 1024
LRU_BLOCKS = 16
LRU_BLOCK_W = LRU_WIDTH // LRU_BLOCKS
CONV_WIDTH = 4
RG_C = 8.0
RWKV_WIDTH = D_MODEL - LRU_WIDTH
HEAD_SIZE = 64
RWKV_HEADS = RWKV_WIDTH // HEAD_SIZE
DECAY_LORA = 96
AAA_LORA = 96
GATE_LORA = 256
N_EXPERTS = 32
TOP_K = 4
D_EXPERT = D_MODEL
SWIGLU_LIMIT = 7.0
SWIGLU_ALPHA = 1.702
LN_EPS = 1e-5
GN_EPS = HEAD_SIZE * 1e-5
DEEPNORM_ALPHA = (2.0 * DEPTH) ** 0.25

LANE = 128
SUBLANE = 8
MXU_DIM = 256
LORA_PAD = 128
LORA_W = 2 * LORA_PAD + GATE_LORA
P_WIDTH = 2 * LRU_WIDTH + 3 * RWKV_WIDTH + LORA_W
VMEM_LIMIT = 56 * 1024 * 1024

F32 = jnp.float32
BF16 = jnp.bfloat16


def _cparams(sem):
    return pltpu.CompilerParams(dimension_semantics=sem, vmem_limit_bytes=VMEM_LIMIT)


def _layer_norm_rows(x, g, b):
    mu = jnp.mean(x, axis=-1, keepdims=True)
    xc = x - mu
    var = jnp.mean(xc * xc, axis=-1, keepdims=True)
    return xc * lax.rsqrt(var + LN_EPS) * g + b


def _softplus(z):
    return jnp.maximum(z, 0.0) + jnp.log1p(jnp.exp(-jnp.abs(z)))


def _sigmoid(z):
    return 1.0 / (1.0 + jnp.exp(-z))


def _split2(x):
    hi = x.astype(BF16)
    lo = (x - hi.astype(F32)).astype(BF16)
    return hi, lo


def _head_sum(x, ones_ref):
    hi, lo = _split2(x)
    ones = ones_ref[...]
    return (jnp.dot(hi, ones, preferred_element_type=F32)
            + jnp.dot(lo, ones, preferred_element_type=F32))


def _in_proj_kernel(x_ref, g_ref, b_ref, w_ref, h_ref, p_ref, hb_ref):
    @pl.when(pl.program_id(1) == 0)
    def _():
        h = _layer_norm_rows(x_ref[...], g_ref[...], b_ref[...])
        h_ref[...] = h
        hb_ref[...] = h.astype(BF16)

    p_ref[...] = jnp.dot(hb_ref[...], w_ref[...], preferred_element_type=F32)


def _in_proj(x2, g, b, w_bf16, tm, tn):
    n = x2.shape[0]
    return pl.pallas_call(
        _in_proj_kernel,
        grid=(n // tm, P_WIDTH // tn),
        in_specs=[
            pl.BlockSpec((tm, D_MODEL), lambda i, j: (i, 0)),
            pl.BlockSpec((1, D_MODEL), lambda i, j: (0, 0)),
            pl.BlockSpec((1, D_MODEL), lambda i, j: (0, 0)),
            pl.BlockSpec((D_MODEL, tn), lambda i, j: (0, j)),
        ],
        out_specs=[
            pl.BlockSpec((tm, D_MODEL), lambda i, j: (i, 0)),
            pl.BlockSpec((tm, tn), lambda i, j: (i, j)),
        ],
        out_shape=[
            jax.ShapeDtypeStruct((n, D_MODEL), F32),
            jax.ShapeDtypeStruct((n, P_WIDTH), F32),
        ],
        scratch_shapes=[pltpu.VMEM((tm, D_MODEL), BF16)],
        compiler_params=_cparams(("parallel", "arbitrary")),
        name="in_proj",
    )(x2, g, b, w_bf16)


def _lru_kernel(u_ref, gi_ref, cw_ref, cb_ref, wr_ref, br_ref, wi_ref, bi_ref,
                lam_ref, o_ref, ubuf, a_s, b_s, carry, *, tt):
    first = pl.program_id(1) == 0

    @pl.when(first)
    def _():
        ubuf[0:SUBLANE, :] = jnp.zeros((SUBLANE, LRU_WIDTH), F32)
        carry[...] = jnp.zeros_like(carry)

    @pl.when(jnp.logical_not(first))
    def _():
        ubuf[0:SUBLANE, :] = ubuf[tt:tt + SUBLANE, :]

    ubuf[SUBLANE:, :] = u_ref[...]
    uc = cb_ref[...]
    for i in range(CONV_WIDTH):
        off = SUBLANE - (CONV_WIDTH - 1) + i
        uc = uc + cw_ref[i:i + 1, :] * ubuf[off:off + tt, :]

    ucb = uc.astype(BF16)
    n_grp = LRU_WIDTH // MXU_DIM

    def gate(w_ref, bias_ref):
        parts = [jnp.dot(ucb[:, g * MXU_DIM:(g + 1) * MXU_DIM], w_ref[g],
                         preferred_element_type=F32) for g in range(n_grp)]
        return _sigmoid(jnp.concatenate(parts, axis=1) + bias_ref[...])

    r_gate = gate(wr_ref, br_ref)
    i_gate = gate(wi_ref, bi_ref)
    log_a = (-RG_C * r_gate) * _softplus(-lam_ref[...])
    a_s[...] = jnp.exp(log_a)
    th = jnp.tanh(log_a)
    b_s[...] = jnp.sqrt(-2.0 * th / (1.0 - th)) * (i_gate * uc)

    row = lax.broadcasted_iota(jnp.int32, (SUBLANE, LRU_WIDTH), 0)

    def group(gidx, c):
        off = pl.multiple_of(gidx * SUBLANE, SUBLANE)
        a = a_s[pl.ds(off, SUBLANE), :]
        b = b_s[pl.ds(off, SUBLANE), :]
        for d in (1, 2, 4):
            keep = row >= d
            a_sh = pltpu.roll(a, d, axis=0)
            b_sh = pltpu.roll(b, d, axis=0)
            b = jnp.where(keep, a * b_sh + b, b)
            a = jnp.where(keep, a * a_sh, a)
        h = a * c + b
        b_s[pl.ds(off, SUBLANE), :] = h
        return h[SUBLANE - 1:SUBLANE, :]

    carry[...] = lax.fori_loop(0, tt // SUBLANE, group, carry[...])
    o_ref[...] = (b_s[...] * jax.nn.gelu(gi_ref[...])).astype(o_ref.dtype)


def _lru(p, conv_w, conv_b, wr_bd, br, wi_bd, bi, lam, bsz, t, tt):
    n = bsz * t
    nt = t // tt
    vec = lambda: pl.BlockSpec((1, LRU_WIDTH), lambda b, i: (0, 0))
    wspec = lambda: pl.BlockSpec((LRU_WIDTH // MXU_DIM, MXU_DIM, MXU_DIM), lambda b, i: (0, 0, 0))
    return pl.pallas_call(
        functools.partial(_lru_kernel, tt=tt),
        grid=(bsz, nt),
        in_specs=[
            pl.BlockSpec((tt, LRU_WIDTH), lambda b, i: (b * nt + i, 0)),
            pl.BlockSpec((tt, LRU_WIDTH), lambda b, i: (b * nt + i, 1)),
            pl.BlockSpec((CONV_WIDTH, LRU_WIDTH), lambda b, i: (0, 0)),
            vec(), wspec(), vec(), wspec(), vec(), vec(),
        ],
        out_specs=pl.BlockSpec((tt, LRU_WIDTH), lambda b, i: (b * nt + i, 0)),
        out_shape=jax.ShapeDtypeStruct((n, LRU_WIDTH), BF16),
        scratch_shapes=[
            pltpu.VMEM((tt + SUBLANE, LRU_WIDTH), F32),
            pltpu.VMEM((tt, LRU_WIDTH), F32),
            pltpu.VMEM((tt, LRU_WIDTH), F32),
            pltpu.VMEM((1, LRU_WIDTH), F32),
        ],
        compiler_params=_cparams(("parallel", "arbitrary")),
        name="lru",
    )(p, p, conv_w, conv_b, wr_bd, br, wi_bd, bi, lam)


def _rw_prep_kernel(pr_ref, pk_ref, pv_ref, pa_ref, mur_ref, muk_ref, muv_ref, mua_ref,
                    w0_ref, wdec_ref, a0_ref, waaa_ref, wgate_ref, kk_ref, ka_ref, ones_ref,
                    r_out, k_out, v_out, kkn_out, b_out, lw_out, g_out,
                    prev_r, prev_k, prev_v, prev_a, *, tt):
    first = pl.program_id(1) == 0

    @pl.when(first)
    def _():
        for ref in (prev_r, prev_k, prev_v, prev_a):
            ref[...] = jnp.zeros_like(ref)

    def shift(x_ref, prev_ref, mu_ref):
        x = x_ref[...]
        row = lax.broadcasted_iota(jnp.int32, x.shape, 0)
        prev = jnp.where(row == 0, prev_ref[...], pltpu.roll(x, 1, axis=0))
        prev_ref[...] = x[tt - 1:tt, :]
        return x + (prev - x) * mu_ref[...]

    r = shift(pr_ref, prev_r, mur_ref)
    k = shift(pk_ref, prev_k, muk_ref)
    v = shift(pv_ref, prev_v, muv_ref)
    ad = shift(pa_ref, prev_a, mua_ref)

    wd = jnp.tanh(ad[:, 0:LORA_PAD]).astype(BF16)
    aa = ad[:, LORA_PAD:2 * LORA_PAD].astype(BF16)
    gd = _sigmoid(ad[:, 2 * LORA_PAD:]).astype(BF16)
    w_pre = w0_ref[...] + jnp.dot(wd, wdec_ref[...], preferred_element_type=F32)
    w_log = -_softplus(-w_pre) - 0.5
    a = _sigmoid(a0_ref[...] + jnp.dot(aa, waaa_ref[...], preferred_element_type=F32))
    g = jnp.dot(gd, wgate_ref[...], preferred_element_type=F32)

    kk = k * kk_ref[...]
    norm = jnp.sqrt(_head_sum(kk * kk, ones_ref))
    kk = kk / jnp.maximum(norm, 1e-12)

    r_out[...] = r
    k_out[...] = k * (1.0 + (a - 1.0) * ka_ref[...])
    v_out[...] = v
    kkn_out[...] = kk
    b_out[...] = kk * a
    lw_out[...] = -jnp.exp(w_log)
    g_out[...] = g


def _rw_prep(p, mu_r, mu_k, mu_v, mu_a, w0, wdec, a0, waaa, wgate, k_k, k_a, ones_bd, bsz, t, tt):
    n = bsz * t
    nt = t // tt
    cb = 2 * LRU_WIDTH // RWKV_WIDTH
    row = lambda c: pl.BlockSpec((tt, RWKV_WIDTH), lambda b, i: (b * nt + i, c))
    vec = lambda w: pl.BlockSpec((1, w), lambda b, i: (0, 0))
    full = lambda s: pl.BlockSpec(s, lambda b, i: (0, 0))
    lora_cb = (2 * LRU_WIDTH + 3 * RWKV_WIDTH) // LORA_W
    out = jax.ShapeDtypeStruct((n, RWKV_WIDTH), F32)
    return pl.pallas_call(
        functools.partial(_rw_prep_kernel, tt=tt),
        grid=(bsz, nt),
        in_specs=[
            row(cb), row(cb + 1), row(cb + 2),
            pl.BlockSpec((tt, LORA_W), lambda b, i: (b * nt + i, lora_cb)),
            vec(RWKV_WIDTH), vec(RWKV_WIDTH), vec(RWKV_WIDTH), vec(LORA_W),
            vec(RWKV_WIDTH), full((LORA_PAD, RWKV_WIDTH)),
            vec(RWKV_WIDTH), full((LORA_PAD, RWKV_WIDTH)),
            full((GATE_LORA, RWKV_WIDTH)),
            vec(RWKV_WIDTH), vec(RWKV_WIDTH),
            full((RWKV_WIDTH, RWKV_WIDTH)),
        ],
        out_specs=[pl.BlockSpec((tt, RWKV_WIDTH), lambda b, i: (b * nt + i, 0))] * 7,
        out_shape=[out] * 7,
        scratch_shapes=[pltpu.VMEM((1, RWKV_WIDTH), F32)] * 3 + [pltpu.VMEM((1, LORA_W), F32)],
        compiler_params=_cparams(("parallel", "arbitrary")),
        name="rw_prep",
    )(p, p, p, p, mu_r, mu_k, mu_v, mu_a, w0, wdec, a0, waaa, wgate, k_k, k_a, ones_bd)


def _rw_chunk_kernel(r_ref, k_ref, v_ref, kk_ref, b_ref, lw_ref, y_ref, s_ref):
    c = CHUNK
    hs = HEAD_SIZE

    @pl.when(pl.program_id(1) == 0)
    def _():
        s_ref[...] = jnp.zeros_like(s_ref)

    lw = lw_ref[...]
    ri = lax.broadcasted_iota(jnp.int32, (c, c), 0)
    ci = lax.broadcasted_iota(jnp.int32, (c, c), 1)
    tri = (ri >= ci).astype(BF16)
    lw_hi = lw.astype(BF16)
    rem = lw - lw_hi.astype(F32)
    lw_mid = rem.astype(BF16)
    lw_lo = (rem - lw_mid.astype(F32)).astype(BF16)
    cl = (jnp.dot(tri, lw_hi, preferred_element_type=F32)
          + jnp.dot(tri, lw_mid, preferred_element_type=F32)
          + jnp.dot(tri, lw_lo, preferred_element_type=F32))
    cl_last = cl[c - 1:c, :]
    g_inv = jnp.exp(-cl)
    g_dec = jnp.exp(cl_last - cl)
    g_last = jnp.exp(cl_last)

    kkn = kk_ref[...]
    bv = b_ref[...]
    kx = k_ref[...]
    v_all = v_ref[...].astype(BF16)
    rg_f32 = r_ref[...] * jnp.exp(cl)
    lhs_a = (-kkn * jnp.exp(cl - lw)).astype(BF16)
    lhs_r = rg_f32.astype(BF16)
    rhs_b = (bv * g_inv).astype(BF16)
    rhs_k = (kx * g_inv).astype(BF16)
    dec_b = (bv * g_dec).astype(BF16)
    dec_k = (kx * g_dec).astype(BF16)

    r2 = lax.broadcasted_iota(jnp.int32, (2 * c, 2 * c), 0)
    c2 = lax.broadcasted_iota(jnp.int32, (2 * c, 2 * c), 1)
    tq = jnp.where(r2 >= c, r2 - c, r2)
    tk = jnp.where(c2 >= c, c2 - c, c2)
    mask = tk < tq + jnp.where(r2 >= c, 1, 0)
    zeros_cc = jnp.zeros((c, hs), BF16)

    dn_t = (((1,), (1,)), ((), ()))
    dn_l = (((0,), (0,)), ((), ()))

    for h in range(RWKV_HEADS):
        sl = slice(h * hs, (h + 1) * hs)
        lhs = jnp.concatenate([lhs_a[:, sl], lhs_r[:, sl]], axis=0)
        rhs = jnp.concatenate([rhs_b[:, sl], rhs_k[:, sl]], axis=0)
        aa = lax.dot_general(lhs, rhs, dn_t, preferred_element_type=F32)
        aa = jnp.where(mask, aa, 0.0)
        a_ab = aa[:c, :c]
        a_ak = aa[:c, c:].astype(BF16)
        a_r = aa[c:, :].astype(BF16)
        v_h = v_all[:, sl]
        akv = jnp.dot(a_ak, v_h, preferred_element_type=F32)
        x = jnp.concatenate([lhs_a[:, sl].astype(F32), akv], axis=1)
        p = a_ab
        n_sq = 6
        for it in range(n_sq):
            pb = p.astype(BF16)
            x = x + jnp.dot(pb, x.astype(BF16), preferred_element_type=F32)
            if it + 1 < n_sq:
                p = jnp.dot(pb, pb, preferred_element_type=F32)
        xb = x.astype(BF16)
        rhs2 = jnp.concatenate(
            [xb, jnp.concatenate([zeros_cc, v_h], axis=1)], axis=0)
        qy = jnp.dot(a_r, rhs2, preferred_element_type=F32)
        dec = jnp.concatenate([dec_b[:, sl], dec_k[:, sl]], axis=0)
        mnt = lax.dot_general(rhs2, dec, dn_l, preferred_element_type=F32)
        s0 = s_ref[h]
        s0b = s0.astype(BF16)
        q = (rg_f32[:, sl] + qy[:, :hs]).astype(BF16)
        y_h = lax.dot_general(q, s0b, dn_t, preferred_element_type=F32) + qy[:, hs:]
        s_ref[h] = (s0 * g_last[:, sl]
                    + jnp.dot(s0b, mnt[:hs, :].astype(BF16), preferred_element_type=F32)
                    + mnt[hs:, :])
        y_ref[:, sl] = y_h


def _rw_chunk(r, k, v, kk, b, lw, bsz, t):
    n = bsz * t
    nc = t // CHUNK
    spec = lambda: pl.BlockSpec((CHUNK, RWKV_WIDTH), lambda bb, i: (bb * nc + i, 0))
    return pl.pallas_call(
        _rw_chunk_kernel,
        grid=(bsz, nc),
        in_specs=[spec() for _ in range(6)],
        out_specs=spec(),
        out_shape=jax.ShapeDtypeStruct((n, RWKV_WIDTH), F32),
        scratch_shapes=[pltpu.VMEM((RWKV_HEADS, HEAD_SIZE, HEAD_SIZE), F32)],
        compiler_params=_cparams(("parallel", "arbitrary")),
        name="rw_chunk",
    )(r, k, v, kk, b, lw)


def _rw_post_kernel(y_ref, r_ref, k_ref, v_ref, g_ref, rk_ref, gg_ref, gb_ref, ones_ref, o_ref):
    y = y_ref[...]
    inv = 1.0 / HEAD_SIZE
    mu = _head_sum(y, ones_ref) * inv
    yc = y - mu
    var = _head_sum(yc * yc, ones_ref) * inv
    yn = yc * lax.rsqrt(var + GN_EPS) * gg_ref[...] + gb_ref[...]
    bonus = _head_sum(r_ref[...] * k_ref[...] * rk_ref[...], ones_ref) * v_ref[...]
    o_ref[...] = ((yn + bonus) * g_ref[...]).astype(o_ref.dtype)


def _rw_post(y, r, k, v, g, r_k, gn_g, gn_b, ones_bd, tt):
    n = y.shape[0]
    row = lambda: pl.BlockSpec((tt, RWKV_WIDTH), lambda i: (i, 0))
    vec = lambda: pl.BlockSpec((1, RWKV_WIDTH), lambda i: (0, 0))
    return pl.pallas_call(
        _rw_post_kernel,
        grid=(n // tt,),
        in_specs=[row(), row(), row(), row(), row(), vec(), vec(), vec(),
                  pl.BlockSpec((RWKV_WIDTH, RWKV_WIDTH), lambda i: (0, 0))],
        out_specs=row(),
        out_shape=jax.ShapeDtypeStruct((n, RWKV_WIDTH), BF16),
        compiler_params=_cparams(("parallel",)),
        name="rw_post",
    )(y, r, k, v, g, r_k, gn_g, gn_b, ones_bd)


def _out_proj_kernel(yl_ref, yr_ref, h0_ref, wo_ref, g_ref, b_ref, wr_ref, br_ref,
                     h1_ref, ti_ref, tg_ref, rank_ref, cnt_ref, carry, *, tm):
    @pl.when(pl.program_id(0) == 0)
    def _():
        carry[...] = jnp.zeros_like(carry)

    mix = (jnp.dot(yl_ref[...], wo_ref[0:LRU_WIDTH, :], preferred_element_type=F32)
           + jnp.dot(yr_ref[...], wo_ref[LRU_WIDTH:, :], preferred_element_type=F32))
    h1 = _layer_norm_rows(DEEPNORM_ALPHA * h0_ref[...] + mix, g_ref[...], b_ref[...])
    h1_ref[...] = h1

    h_hi, h_lo = _split2(h1)
    w_hi = wr_ref[0]
    w_lo = wr_ref[1]
    logits = (jnp.dot(h_hi, w_hi, preferred_element_type=F32)
              + jnp.dot(h_hi, w_lo, preferred_element_type=F32)
              + jnp.dot(h_lo, w_hi, preferred_element_type=F32)) + br_ref[...]

    lane = lax.broadcasted_iota(jnp.int32, (tm, N_EXPERTS), 1).astype(F32)
    lane4 = lax.broadcasted_iota(jnp.int32, (tm, TOP_K), 1)
    work = logits
    vals, idxs, sels = [], [], []
    for _ in range(TOP_K):
        m = jnp.max(work, axis=-1, keepdims=True)
        idx = jnp.min(jnp.where(work == m, lane, float(N_EXPERTS)), axis=-1, keepdims=True)
        sel = lane == idx
        vals.append(m)
        idxs.append(idx)
        sels.append(sel)
        work = jnp.where(sel, -jnp.inf, work)
    exps = [jnp.exp(vv - vals[0]) for vv in vals]
    denom = exps[0] + exps[1] + exps[2] + exps[3]

    member = jnp.zeros((tm, N_EXPERTS), F32)
    for sel in sels:
        member = member + sel.astype(F32)
    ri = lax.broadcasted_iota(jnp.int32, (tm, tm), 0)
    ci = lax.broadcasted_iota(jnp.int32, (tm, tm), 1)
    before = (ci < ri).astype(BF16)
    rank_full = jnp.dot(before, member.astype(BF16), preferred_element_type=F32) + carry[...]
    carry[...] = carry[...] + jnp.sum(member, axis=0, keepdims=True)
    cnt_ref[...] = carry[...].astype(jnp.int32)

    ti = jnp.zeros((tm, TOP_K), jnp.int32)
    tg = jnp.zeros((tm, TOP_K), F32)
    rk = jnp.zeros((tm, TOP_K), jnp.int32)
    for kq in range(TOP_K):
        rank_k = jnp.sum(jnp.where(sels[kq], rank_full, 0.0), axis=-1, keepdims=True)
        ti = jnp.where(lane4 == kq, idxs[kq].astype(jnp.int32), ti)
        tg = jnp.where(lane4 == kq, exps[kq] / denom, tg)
        rk = jnp.where(lane4 == kq, rank_k.astype(jnp.int32), rk)
    ti_ref[...] = ti
    tg_ref[...] = tg
    rank_ref[...] = rk


def _out_proj(y_lru, y_rw, h0, w_out_bf16, g, b, w_router_split, b_router, tm):
    n = h0.shape[0]
    row = lambda w: pl.BlockSpec((tm, w), lambda i: (i, 0))
    vec = lambda w: pl.BlockSpec((1, w), lambda i: (0, 0))
    return pl.pallas_call(
        functools.partial(_out_proj_kernel, tm=tm),
        grid=(n // tm,),
        in_specs=[
            row(LRU_WIDTH), row(RWKV_WIDTH), row(D_MODEL),
            pl.BlockSpec((D_MODEL, D_MODEL), lambda i: (0, 0)),
            vec(D_MODEL), vec(D_MODEL),
            pl.BlockSpec((2, D_MODEL, N_EXPERTS), lambda i: (0, 0, 0)),
            vec(N_EXPERTS),
        ],
        out_specs=[row(D_MODEL), row(TOP_K), row(TOP_K), row(TOP_K), vec(N_EXPERTS)],
        out_shape=[
            jax.ShapeDtypeStruct((n, D_MODEL), F32),
            jax.ShapeDtypeStruct((n, TOP_K), jnp.int32),
            jax.ShapeDtypeStruct((n, TOP_K), F32),
            jax.ShapeDtypeStruct((n, TOP_K), jnp.int32),
            jax.ShapeDtypeStruct((1, N_EXPERTS), jnp.int32),
        ],
        scratch_shapes=[pltpu.VMEM((1, N_EXPERTS), F32)],
        compiler_params=_cparams(("arbitrary",)),
        name="out_proj",
    )(y_lru, y_rw, h0, w_out_bf16, g, b, w_router_split, b_router)


def _scatter_kernel(dest_ref, fill_ref, h_ref, xs_ref, zbuf, sem, zsem, *, tb, tm, nb):
    @pl.when(pl.program_id(0) == 0)
    def _():
        zbuf[...] = jnp.zeros_like(zbuf)

        def fill_copy(start):
            start = pl.multiple_of(start, SUBLANE)
            return pltpu.make_async_copy(zbuf, xs_ref.at[pl.ds(start, tm), :], zsem)

        def fill(e, carry):
            @pl.when(fill_ref[e] >= 0)
            def _():
                fill_copy(fill_ref[e]).start()
            return carry

        def fill_wait(e, carry):
            @pl.when(fill_ref[e] >= 0)
            def _():
                fill_copy(fill_ref[e]).wait()
            return carry

        def tail(blk, carry):
            fill_copy(blk * tm).start()
            return carry

        def tail_wait(blk, carry):
            fill_copy(blk * tm).wait()
            return carry

        n_used = fill_ref[N_EXPERTS]
        lax.fori_loop(0, N_EXPERTS, fill, 0)
        lax.fori_loop(n_used, nb, tail, 0)
        lax.fori_loop(0, N_EXPERTS, fill_wait, 0)
        lax.fori_loop(n_used, nb, tail_wait, 0)

    base = pl.program_id(0) * (tb * TOP_K)

    def copy(t, kq):
        slot = dest_ref[base + t * TOP_K + kq]
        return pltpu.make_async_copy(h_ref.at[pl.ds(t, 1), :], xs_ref.at[pl.ds(slot, 1), :], sem)

    def issue(t, carry):
        for kq in range(TOP_K):
            copy(t, kq).start()
        return carry

    lax.fori_loop(0, tb, issue, 0)

    def drain(t, carry):
        for kq in range(TOP_K):
            copy(t, kq).wait()
        return carry

    lax.fori_loop(0, tb, drain, 0)


def _scatter(dest_flat, fill_start, h1, n_slots, tb, tm):
    n = h1.shape[0]
    nb = n_slots // tm
    grid_spec = pltpu.PrefetchScalarGridSpec(
        num_scalar_prefetch=2,
        grid=(n // tb,),
        in_specs=[pl.BlockSpec((tb, D_MODEL), lambda i, d, f: (i, 0))],
        out_specs=pl.BlockSpec(memory_space=pl.ANY),
        scratch_shapes=[pltpu.VMEM((tm, D_MODEL), F32), pltpu.SemaphoreType.DMA(()),
                        pltpu.SemaphoreType.DMA(())],
    )
    return pl.pallas_call(
        functools.partial(_scatter_kernel, tb=tb, tm=tm, nb=nb),
        grid_spec=grid_spec,
        out_shape=jax.ShapeDtypeStruct((n_slots, D_MODEL), F32),
        compiler_params=_cparams(("arbitrary",)),
        name="scatter",
    )(dest_flat, fill_start, h1)


def _gmm1_kernel(meta_ref, x_ref, wg_ref, wu_ref, bg_ref, bu_ref, o_ref, wgb, wub, *, nb):
    i = pl.program_id(1)
    e = meta_ref[i]
    prev_e = meta_ref[jnp.maximum(i - 1, 0)]

    @pl.when(jnp.logical_or(i == 0, e != prev_e))
    def _():
        wgb[...] = wg_ref[0].astype(BF16)
        wub[...] = wu_ref[0].astype(BF16)

    @pl.when(i < meta_ref[nb])
    def _():
        x = x_ref[...].astype(BF16)
        gate = jnp.dot(x, wgb[...], preferred_element_type=F32) + bg_ref[0]
        up = jnp.dot(x, wub[...], preferred_element_type=F32) + bu_ref[0]
        gate = jnp.minimum(gate, SWIGLU_LIMIT)
        up = jnp.clip(up, -SWIGLU_LIMIT, SWIGLU_LIMIT)
        act = gate * _sigmoid(SWIGLU_ALPHA * gate) * (up + 1.0)
        o_ref[...] = act.astype(o_ref.dtype)

    @pl.when(i >= meta_ref[nb])
    def _():
        o_ref[...] = jnp.zeros_like(o_ref)


def _gmm1(meta, xs, w1, b1, nb, tm, tn):
    n_slots = xs.shape[0]
    nj = D_EXPERT // tn

    def row_blk(j, i, m):
        return jnp.minimum(i, m[nb] - 1)

    grid_spec = pltpu.PrefetchScalarGridSpec(
        num_scalar_prefetch=1,
        grid=(nj, nb),
        in_specs=[
            pl.BlockSpec((tm, D_MODEL), lambda j, i, m: (row_blk(j, i, m), 0)),
            pl.BlockSpec((1, D_MODEL, tn), lambda j, i, m: (m[i], 0, j)),
            pl.BlockSpec((1, D_MODEL, tn), lambda j, i, m: (m[i], 0, j + nj)),
            pl.BlockSpec((1, 1, tn), lambda j, i, m: (m[i], 0, j)),
            pl.BlockSpec((1, 1, tn), lambda j, i, m: (m[i], 0, j + nj)),
        ],
        out_specs=pl.BlockSpec((tm, tn), lambda j, i, m: (i, j)),
        scratch_shapes=[pltpu.VMEM((D_MODEL, tn), BF16), pltpu.VMEM((D_MODEL, tn), BF16)],
    )
    return pl.pallas_call(
        functools.partial(_gmm1_kernel, nb=nb),
        grid_spec=grid_spec,
        out_shape=jax.ShapeDtypeStruct((n_slots, D_EXPERT), BF16),
        compiler_params=_cparams(("arbitrary", "arbitrary")),
        name="gmm1",
    )(meta, xs, w1, w1, b1, b1)


def _gmm2_kernel(meta_ref, a_ref, w_ref, b_ref, o_ref, wb, *, nb):
    i = pl.program_id(1)
    e = meta_ref[i]
    prev_e = meta_ref[jnp.maximum(i - 1, 0)]

    @pl.when(jnp.logical_or(i == 0, e != prev_e))
    def _():
        wb[...] = w_ref[0].astype(BF16)

    @pl.when(i < meta_ref[nb])
    def _():
        o_ref[...] = jnp.dot(a_ref[...], wb[...], preferred_element_type=F32) + b_ref[0]

    @pl.when(i >= meta_ref[nb])
    def _():
        o_ref[...] = jnp.zeros_like(o_ref)


def _gmm2(meta, act, w2, b2, nb, tm, tn):
    n_slots = act.shape[0]
    nj = D_MODEL // tn

    def row_blk(j, i, m):
        return jnp.minimum(i, m[nb] - 1)

    grid_spec = pltpu.PrefetchScalarGridSpec(
        num_scalar_prefetch=1,
        grid=(nj, nb),
        in_specs=[
            pl.BlockSpec((tm, D_EXPERT), lambda j, i, m: (row_blk(j, i, m), 0)),
            pl.BlockSpec((1, D_EXPERT, tn), lambda j, i, m: (m[i], 0, j)),
            pl.BlockSpec((1, 1, tn), lambda j, i, m: (m[i], 0, j)),
        ],
        out_specs=pl.BlockSpec((tm, tn), lambda j, i, m: (i, j)),
        scratch_shapes=[pltpu.VMEM((D_EXPERT, tn), BF16)],
    )
    return pl.pallas_call(
        functools.partial(_gmm2_kernel, nb=nb),
        grid_spec=grid_spec,
        out_shape=jax.ShapeDtypeStruct((n_slots, D_MODEL), F32),
        compiler_params=_cparams(("arbitrary", "arbitrary")),
        name="gmm2",
    )(meta, act, w2, b2)


def _combine_kernel(dest_ref, h1_ref, tg_ref, g_ref, b_ref, ys_ref, o_ref, buf, sem, *, tb):
    base = pl.program_id(0) * (tb * TOP_K)

    def copy(t, kq):
        slot = dest_ref[base + t * TOP_K + kq]
        return pltpu.make_async_copy(ys_ref.at[pl.ds(slot, 1), :], buf.at[kq, pl.ds(t, 1), :], sem)

    def issue(t, carry):
        for kq in range(TOP_K):
            copy(t, kq).start()
        return carry

    lax.fori_loop(0, tb, issue, 0)

    def drain(t, carry):
        for kq in range(TOP_K):
            copy(t, kq).wait()
        return carry

    lax.fori_loop(0, tb, drain, 0)

    tg = tg_ref[...]
    y = DEEPNORM_ALPHA * h1_ref[...]
    for kq in range(TOP_K):
        y = y + tg[:, kq:kq + 1] * buf[kq]
    o_ref[...] = _layer_norm_rows(y, g_ref[...], b_ref[...])


def _combine(dest_flat, h1, tg, g, b, ys, tb):
    n = h1.shape[0]
    grid_spec = pltpu.PrefetchScalarGridSpec(
        num_scalar_prefetch=1,
        grid=(n // tb,),
        in_specs=[
            pl.BlockSpec((tb, D_MODEL), lambda i, d: (i, 0)),
            pl.BlockSpec((tb, TOP_K), lambda i, d: (i, 0)),
            pl.BlockSpec((1, D_MODEL), lambda i, d: (0, 0)),
            pl.BlockSpec((1, D_MODEL), lambda i, d: (0, 0)),
            pl.BlockSpec(memory_space=pl.ANY),
        ],
        out_specs=pl.BlockSpec((tb, D_MODEL), lambda i, d: (i, 0)),
        scratch_shapes=[pltpu.VMEM((TOP_K, tb, D_MODEL), F32), pltpu.SemaphoreType.DMA(())],
    )
    return pl.pallas_call(
        functools.partial(_combine_kernel, tb=tb),
        grid_spec=grid_spec,
        out_shape=jax.ShapeDtypeStruct((n, D_MODEL), F32),
        compiler_params=_cparams(("arbitrary",)),
        name="combine",
    )(dest_flat, h1, tg, g, b, ys)


def _block_diag(w, group):
    nb = w.shape[0] // group
    w = w.reshape(nb, group, LRU_BLOCK_W, LRU_BLOCK_W)
    eye = jnp.eye(group, dtype=w.dtype)
    out = jnp.einsum('gajk,ab->gajbk', w, eye)
    return out.reshape(nb, group * LRU_BLOCK_W, group * LRU_BLOCK_W)


def _pad_rows(w, rows):
    return jnp.pad(w, ((0, rows - w.shape[0]), (0, 0)))


def _layer(h_in_x, l, prm, bsz, t):
    n = bsz * t
    row = lambda a: a.reshape(1, -1)

    w_in = prm['w_in'][l]
    o_rw = 2 * LRU_WIDTH
    o_l = o_rw + 3 * RWKV_WIDTH
    pad_c = lambda w: jnp.pad(w, ((0, 0), (0, LORA_PAD - w.shape[1])))
    w_in_p = jnp.concatenate([
        w_in[:, :o_l],
        pad_c(w_in[:, o_l:o_l + DECAY_LORA]),
        pad_c(w_in[:, o_l + DECAY_LORA:o_l + DECAY_LORA + AAA_LORA]),
        w_in[:, o_l + DECAY_LORA + AAA_LORA:],
    ], axis=1).astype(BF16)
    mu = prm['shift_mu'][l]
    pad_v = lambda v: jnp.pad(v, (0, LORA_PAD - v.shape[0]))
    mu_l = mu[3 * RWKV_WIDTH:]
    mu_a = jnp.concatenate([pad_v(mu_l[:DECAY_LORA]), pad_v(mu_l[DECAY_LORA:DECAY_LORA + AAA_LORA]),
                            mu_l[DECAY_LORA + AAA_LORA:]])

    tm_in = min(512, n)
    h0, p = _in_proj(h_in_x, row(prm['ln_in_g']), row(prm['ln_in_b']), w_in_p, tm_in, 512)

    tt = min(512, t)
    group = MXU_DIM // LRU_BLOCK_W
    y_lru = _lru(p, prm['conv_w'][l], row(prm['conv_b'][l]),
                 _block_diag(prm['w_rgate'][l], group).astype(BF16), row(prm['b_rgate'][l]),
                 _block_diag(prm['w_igate'][l], group).astype(BF16), row(prm['b_igate'][l]),
                 row(prm['lru_lambda'][l]), bsz, t, tt)

    head_id = jnp.arange(RWKV_WIDTH) // HEAD_SIZE
    ones_bd = (head_id[:, None] == head_id[None, :]).astype(BF16)
    tt_rw = min(256, t)
    r, k, v, kk, bvec, lw, g = _rw_prep(
        p, row(mu[:RWKV_WIDTH]), row(mu[RWKV_WIDTH:2 * RWKV_WIDTH]),
        row(mu[2 * RWKV_WIDTH:3 * RWKV_WIDTH]), row(mu_a),
        row(prm['w0'][l]), _pad_rows(prm['rw_decay_up'][l], LORA_PAD).astype(BF16),
        row(prm['a0'][l]), _pad_rows(prm['rw_aaa_up'][l], LORA_PAD).astype(BF16),
        prm['rw_gate_up'][l].astype(BF16), row(prm['k_k'][l]), row(prm['k_a'][l]),
        ones_bd, bsz, t, tt_rw)
    y = _rw_chunk(r, k, v, kk, bvec, lw, bsz, t)
    y_rw = _rw_post(y, r, k, v, g, row(prm['r_k'][l]), row(prm['gn_g'][l]), row(prm['gn_b'][l]),
                    ones_bd, min(512, n))

    w_r = prm['w_router'][l]
    w_r_hi = w_r.astype(BF16)
    w_r_lo = (w_r - w_r_hi.astype(F32)).astype(BF16)
    tm_out = min(512, n)
    h1, top_i, top_g, rank, counts = _out_proj(
        y_lru, y_rw, h0, prm['w_out'][l].astype(BF16), row(prm['ln1_g'][l]), row(prm['ln1_b'][l]),
        jnp.stack([w_r_hi, w_r_lo]), row(prm['b_router'][l]), tm_out)

    tm = 256
    counts = counts.reshape(N_EXPERTS)
    padded = ((counts + tm - 1) // tm) * tm
    pad_end = jnp.cumsum(padded)
    pad_start = pad_end - padded
    dest = (pad_start[top_i] + rank).reshape(-1).astype(jnp.int32)
    nb = (n * TOP_K) // tm + N_EXPERTS
    n_slots = nb * tm
    n_used = (pad_end[-1] // tm).astype(jnp.int32)
    blk = jnp.minimum(jnp.arange(nb, dtype=jnp.int32), n_used - 1)
    blk_e = jnp.minimum(jnp.searchsorted(pad_end, blk * tm, side='right'), N_EXPERTS - 1)
    meta = jnp.concatenate([blk_e.astype(jnp.int32), n_used.reshape(1)])

    fill_start = jnp.concatenate([jnp.where(padded > 0, pad_end - tm, -1).astype(jnp.int32),
                                  n_used.reshape(1)])
    tb = min(128, n)
    xs = _scatter(dest, fill_start, h1, n_slots, tb, tm)
    act = _gmm1(meta, xs, prm['w_exp1'][l], prm['b_exp1'][l].reshape(N_EXPERTS, 1, 2 * D_EXPERT),
                nb, tm, 512)
    ys = _gmm2(meta, act, prm['w_exp2'][l], prm['b_exp2'][l].reshape(N_EXPERTS, 1, D_MODEL),
               nb, tm, 1024)
    return _combine(dest, h1, top_g, row(prm['ln2_g'][l]), row(prm['ln2_b'][l]),
                    ys, tb)


def kernel(x, ln_in_g, ln_in_b, w_in, conv_w, conv_b, w_rgate, b_rgate, w_igate, b_igate, lru_lambda, shift_mu, w0, rw_decay_up, a0, rw_aaa_up, rw_gate_up, k_k, k_a, r_k, gn_g, gn_b, w_out, ln1_g, ln1_b, w_router, b_router, w_exp1, b_exp1, w_exp2, b_exp2, ln2_g, ln2_b):
    bsz, t, d = x.shape
    prm = dict(ln_in_g=ln_in_g, ln_in_b=ln_in_b, w_in=w_in, conv_w=conv_w, conv_b=conv_b,
               w_rgate=w_rgate, b_rgate=b_rgate, w_igate=w_igate, b_igate=b_igate,
               lru_lambda=lru_lambda, shift_mu=shift_mu, w0=w0, rw_decay_up=rw_decay_up, a0=a0,
               rw_aaa_up=rw_aaa_up, rw_gate_up=rw_gate_up, k_k=k_k, k_a=k_a, r_k=r_k, gn_g=gn_g,
               gn_b=gn_b, w_out=w_out, ln1_g=ln1_g, ln1_b=ln1_b, w_router=w_router,
               b_router=b_router, w_exp1=w_exp1, b_exp1=b_exp1, w_exp2=w_exp2, b_exp2=b_exp2,
               ln2_g=ln2_g, ln2_b=ln2_b)
    out = _layer(x.reshape(bsz * t, d), 0, prm, bsz, t)
    return out.reshape(bsz, t, d)
```

```python
import functools

import jax
import jax.numpy as jnp
from jax import lax
from jax.experimental import pallas as pl
from jax.experimental.pallas import tpu as pltpu

D_MODEL = 2048
DEPTH = 1
CHUNK = 64
LRU_WIDTH = 1024
LRU_BLOCKS = 16
LRU_BLOCK_W = LRU_WIDTH // LRU_BLOCKS
CONV_WIDTH = 4
RG_C = 8.0
RWKV_WIDTH = D_MODEL - LRU_WIDTH
HEAD_SIZE = 64
RWKV_HEADS = RWKV_WIDTH // HEAD_SIZE
DECAY_LORA = 96
AAA_LORA = 96
GATE_LORA = 256
N_EXPERTS = 32
TOP_K = 4
D_EXPERT = D_MODEL
SWIGLU_LIMIT = 7.0
SWIGLU_ALPHA = 1.702
LN_EPS = 1e-5
GN_EPS = HEAD_SIZE * 1e-5
DEEPNORM_ALPHA = (2.0 * DEPTH) ** 0.25

LANE = 128
SUBLANE = 8
MXU_DIM = 256
LORA_PAD = 128
LORA_W = 2 * LORA_PAD + GATE_LORA
P_WIDTH = 2 * LRU_WIDTH + 3 * RWKV_WIDTH + LORA_W
VMEM_LIMIT = 56 * 1024 * 1024

F32 = jnp.float32
BF16 = jnp.bfloat16


def _cparams(sem):
    return pltpu.CompilerParams(dimension_semantics=sem, vmem_limit_bytes=VMEM_LIMIT)


def _layer_norm_rows(x, g, b):
    mu = jnp.mean(x, axis=-1, keepdims=True)
    xc = x - mu
    var = jnp.mean(xc * xc, axis=-1, keepdims=True)
    return xc * lax.rsqrt(var + LN_EPS) * g + b


def _softplus(z):
    return jnp.maximum(z, 0.0) + jnp.log1p(jnp.exp(-jnp.abs(z)))


def _sigmoid(z):
    return 1.0 / (1.0 + jnp.exp(-z))


def _split2(x):
    hi = x.astype(BF16)
    lo = (x - hi.astype(F32)).astype(BF16)
    return hi, lo


HALF = D_MODEL // 2
U32 = jnp.uint32


def _pack_halves(x):
    lo = lax.bitcast_convert_type(x[:, :HALF].astype(BF16).astype(F32), U32) >> 16
    hi = lax.bitcast_convert_type(x[:, HALF:].astype(BF16).astype(F32), U32) & jnp.uint32(0xFFFF0000)
    return hi | lo


def _unpack_halves(w):
    lo = lax.bitcast_convert_type(w << 16, F32)
    hi = lax.bitcast_convert_type(w & jnp.uint32(0xFFFF0000), F32)
    return lo, hi


def _head_sum(x, ones_ref):
    hi, lo = _split2(x)
    ones = ones_ref[...]
    return (jnp.dot(hi, ones, preferred_element_type=F32)
            + jnp.dot(lo, ones, preferred_element_type=F32))


def _in_proj_kernel(x_ref, g_ref, b_ref, w_ref, h_ref, p_ref, hb_ref):
    @pl.when(pl.program_id(1) == 0)
    def _():
        h = _layer_norm_rows(x_ref[...], g_ref[...], b_ref[...])
        h_ref[...] = h
        hb_ref[...] = h.astype(BF16)

    p_ref[...] = jnp.dot(hb_ref[...], w_ref[...], preferred_element_type=F32)


def _in_proj(x2, g, b, w_bf16, tm, tn):
    n = x2.shape[0]
    return pl.pallas_call(
        _in_proj_kernel,
        grid=(n // tm, P_WIDTH // tn),
        in_specs=[
            pl.BlockSpec((tm, D_MODEL), lambda i, j: (i, 0)),
            pl.BlockSpec((1, D_MODEL), lambda i, j: (0, 0)),
            pl.BlockSpec((1, D_MODEL), lambda i, j: (0, 0)),
            pl.BlockSpec((D_MODEL, tn), lambda i, j: (0, j)),
        ],
        out_specs=[
            pl.BlockSpec((tm, D_MODEL), lambda i, j: (i, 0)),
            pl.BlockSpec((tm, tn), lambda i, j: (i, j)),
        ],
        out_shape=[
            jax.ShapeDtypeStruct((n, D_MODEL), F32),
            jax.ShapeDtypeStruct((n, P_WIDTH), F32),
        ],
        scratch_shapes=[pltpu.VMEM((tm, D_MODEL), BF16)],
        compiler_params=_cparams(("parallel", "arbitrary")),
        name="in_proj",
    )(x2, g, b, w_bf16)


def _lru_kernel(u_ref, gi_ref, cw_ref, cb_ref, wr_ref, br_ref, wi_ref, bi_ref,
                lam_ref, o_ref, ubuf, a_s, b_s, carry, *, tt):
    first = pl.program_id(1) == 0

    @pl.when(first)
    def _():
        ubuf[0:SUBLANE, :] = jnp.zeros((SUBLANE, LRU_WIDTH), F32)
        carry[...] = jnp.zeros_like(carry)

    @pl.when(jnp.logical_not(first))
    def _():
        ubuf[0:SUBLANE, :] = ubuf[tt:tt + SUBLANE, :]

    ubuf[SUBLANE:, :] = u_ref[...]
    uc = cb_ref[...]
    for i in range(CONV_WIDTH):
        off = SUBLANE - (CONV_WIDTH - 1) + i
        uc = uc + cw_ref[i:i + 1, :] * ubuf[off:off + tt, :]

    ucb = uc.astype(BF16)
    n_grp = LRU_WIDTH // MXU_DIM

    def gate(w_ref, bias_ref):
        parts = [jnp.dot(ucb[:, g * MXU_DIM:(g + 1) * MXU_DIM], w_ref[g],
                         preferred_element_type=F32) for g in range(n_grp)]
        return _sigmoid(jnp.concatenate(parts, axis=1) + bias_ref[...])

    r_gate = gate(wr_ref, br_ref)
    i_gate = gate(wi_ref, bi_ref)
    log_a = (-RG_C * r_gate) * _softplus(-lam_ref[...])
    a_s[...] = jnp.exp(log_a)
    th = jnp.tanh(log_a)
    b_s[...] = jnp.sqrt(-2.0 * th / (1.0 - th)) * (i_gate * uc)

    row = lax.broadcasted_iota(jnp.int32, (SUBLANE, LRU_WIDTH), 0)

    def group(gidx, c):
        off = pl.multiple_of(gidx * SUBLANE, SUBLANE)
        a = a_s[pl.ds(off, SUBLANE), :]
        b = b_s[pl.ds(off, SUBLANE), :]
        for d in (1, 2, 4):
            keep = row >= d
            a_sh = pltpu.roll(a, d, axis=0)
            b_sh = pltpu.roll(b, d, axis=0)
            b = jnp.where(keep, a * b_sh + b, b)
            a = jnp.where(keep, a * a_sh, a)
        h = a * c + b
        b_s[pl.ds(off, SUBLANE), :] = h
        return h[SUBLANE - 1:SUBLANE, :]

    carry[...] = lax.fori_loop(0, tt // SUBLANE, group, carry[...])
    o_ref[...] = (b_s[...] * jax.nn.gelu(gi_ref[...])).astype(o_ref.dtype)


def _lru(p, conv_w, conv_b, wr_bd, br, wi_bd, bi, lam, bsz, t, tt):
    n = bsz * t
    nt = t // tt
    vec = lambda: pl.BlockSpec((1, LRU_WIDTH), lambda b, i: (0, 0))
    wspec = lambda: pl.BlockSpec((LRU_WIDTH // MXU_DIM, MXU_DIM, MXU_DIM), lambda b, i: (0, 0, 0))
    return pl.pallas_call(
        functools.partial(_lru_kernel, tt=tt),
        grid=(bsz, nt),
        in_specs=[
            pl.BlockSpec((tt, LRU_WIDTH), lambda b, i: (b * nt + i, 0)),
            pl.BlockSpec((tt, LRU_WIDTH), lambda b, i: (b * nt + i, 1)),
            pl.BlockSpec((CONV_WIDTH, LRU_WIDTH), lambda b, i: (0, 0)),
            vec(), wspec(), vec(), wspec(), vec(), vec(),
        ],
        out_specs=pl.BlockSpec((tt, LRU_WIDTH), lambda b, i: (b * nt + i, 0)),
        out_shape=jax.ShapeDtypeStruct((n, LRU_WIDTH), BF16),
        scratch_shapes=[
            pltpu.VMEM((tt + SUBLANE, LRU_WIDTH), F32),
            pltpu.VMEM((tt, LRU_WIDTH), F32),
            pltpu.VMEM((tt, LRU_WIDTH), F32),
            pltpu.VMEM((1, LRU_WIDTH), F32),
        ],
        compiler_params=_cparams(("parallel", "arbitrary")),
        name="lru",
    )(p, p, conv_w, conv_b, wr_bd, br, wi_bd, bi, lam)


def _rw_prep_kernel(pr_ref, pk_ref, pv_ref, pa_ref, mur_ref, muk_ref, muv_ref, mua_ref,
                    w0_ref, wdec_ref, a0_ref, waaa_ref, wgate_ref, kk_ref, ka_ref, ones_ref,
                    r_out, k_out, v_out, kkn_out, b_out, lw_out, g_out,
                    prev_r, prev_k, prev_v, prev_a, *, tt):
    first = pl.program_id(1) == 0

    @pl.when(first)
    def _():
        for ref in (prev_r, prev_k, prev_v, prev_a):
            ref[...] = jnp.zeros_like(ref)

    def shift(x_ref, prev_ref, mu_ref):
        x = x_ref[...]
        row = lax.broadcasted_iota(jnp.int32, x.shape, 0)
        prev = jnp.where(row == 0, prev_ref[...], pltpu.roll(x, 1, axis=0))
        prev_ref[...] = x[tt - 1:tt, :]
        return x + (prev - x) * mu_ref[...]

    r = shift(pr_ref, prev_r, mur_ref)
    k = shift(pk_ref, prev_k, muk_ref)
    v = shift(pv_ref, prev_v, muv_ref)
    ad = shift(pa_ref, prev_a, mua_ref)

    wd = jnp.tanh(ad[:, 0:LORA_PAD]).astype(BF16)
    aa = ad[:, LORA_PAD:2 * LORA_PAD].astype(BF16)
    gd = _sigmoid(ad[:, 2 * LORA_PAD:]).astype(BF16)
    w_pre = w0_ref[...] + jnp.dot(wd, wdec_ref[...], preferred_element_type=F32)
    w_log = -_softplus(-w_pre) - 0.5
    a = _sigmoid(a0_ref[...] + jnp.dot(aa, waaa_ref[...], preferred_element_type=F32))
    g = jnp.dot(gd, wgate_ref[...], preferred_element_type=F32)

    kk = k * kk_ref[...]
    norm = jnp.sqrt(_head_sum(kk * kk, ones_ref))
    kk = kk / jnp.maximum(norm, 1e-12)

    r_out[...] = r
    k_out[...] = k * (1.0 + (a - 1.0) * ka_ref[...])
    v_out[...] = v
    kkn_out[...] = kk
    b_out[...] = kk * a
    lw_out[...] = -jnp.exp(w_log)
    g_out[...] = g


def _rw_prep(p, mu_r, mu_k, mu_v, mu_a, w0, wdec, a0, waaa, wgate, k_k, k_a, ones_bd, bsz, t, tt):
    n = bsz * t
    nt = t // tt
    cb = 2 * LRU_WIDTH // RWKV_WIDTH
    row = lambda c: pl.BlockSpec((tt, RWKV_WIDTH), lambda b, i: (b * nt + i, c))
    vec = lambda w: pl.BlockSpec((1, w), lambda b, i: (0, 0))
    full = lambda s: pl.BlockSpec(s, lambda b, i: (0, 0))
    lora_cb = (2 * LRU_WIDTH + 3 * RWKV_WIDTH) // LORA_W
    out = jax.ShapeDtypeStruct((n, RWKV_WIDTH), F32)
    return pl.pallas_call(
        functools.partial(_rw_prep_kernel, tt=tt),
        grid=(bsz, nt),
        in_specs=[
            row(cb), row(cb + 1), row(cb + 2),
            pl.BlockSpec((tt, LORA_W), lambda b, i: (b * nt + i, lora_cb)),
            vec(RWKV_WIDTH), vec(RWKV_WIDTH), vec(RWKV_WIDTH), vec(LORA_W),
            vec(RWKV_WIDTH), full((LORA_PAD, RWKV_WIDTH)),
            vec(RWKV_WIDTH), full((LORA_PAD, RWKV_WIDTH)),
            full((GATE_LORA, RWKV_WIDTH)),
            vec(RWKV_WIDTH), vec(RWKV_WIDTH),
            full((RWKV_WIDTH, RWKV_WIDTH)),
        ],
        out_specs=[pl.BlockSpec((tt, RWKV_WIDTH), lambda b, i: (b * nt + i, 0))] * 7,
        out_shape=[out] * 7,
        scratch_shapes=[pltpu.VMEM((1, RWKV_WIDTH), F32)] * 3 + [pltpu.VMEM((1, LORA_W), F32)],
        compiler_params=_cparams(("parallel", "arbitrary")),
        name="rw_prep",
    )(p, p, p, p, mu_r, mu_k, mu_v, mu_a, w0, wdec, a0, waaa, wgate, k_k, k_a, ones_bd)


def _rw_chunk_kernel(r_ref, k_ref, v_ref, kk_ref, b_ref, lw_ref, y_ref, *s_refs):
    c = CHUNK
    hs = HEAD_SIZE

    @pl.when(pl.program_id(1) == 0)
    def _():
        for s_ref in s_refs:
            s_ref[...] = jnp.zeros_like(s_ref)

    lw = lw_ref[...]
    ri = lax.broadcasted_iota(jnp.int32, (c, c), 0)
    ci = lax.broadcasted_iota(jnp.int32, (c, c), 1)
    tri = (ri >= ci).astype(BF16)
    lw_hi = lw.astype(BF16)
    rem = lw - lw_hi.astype(F32)
    lw_mid = rem.astype(BF16)
    lw_lo = (rem - lw_mid.astype(F32)).astype(BF16)
    cl = (jnp.dot(tri, lw_hi, preferred_element_type=F32)
          + jnp.dot(tri, lw_mid, preferred_element_type=F32)
          + jnp.dot(tri, lw_lo, preferred_element_type=F32))
    cl_last = cl[c - 1:c, :]
    g_inv = jnp.exp(-cl)
    g_dec = jnp.exp(cl_last - cl)
    g_last = jnp.exp(cl_last)

    kkn = kk_ref[...]
    bv = b_ref[...]
    kx = k_ref[...]
    v_all = v_ref[...].astype(BF16)
    rg_f32 = r_ref[...] * jnp.exp(cl)
    lhs_a = (-kkn * jnp.exp(cl - lw)).astype(BF16)
    lhs_r = rg_f32.astype(BF16)
    rhs_b = (bv * g_inv).astype(BF16)
    rhs_k = (kx * g_inv).astype(BF16)
    dec_b = (bv * g_dec).astype(BF16)
    dec_k = (kx * g_dec).astype(BF16)

    r2 = lax.broadcasted_iota(jnp.int32, (2 * c, 2 * c), 0)
    c2 = lax.broadcasted_iota(jnp.int32, (2 * c, 2 * c), 1)
    tq = jnp.where(r2 >= c, r2 - c, r2)
    tk = jnp.where(c2 >= c, c2 - c, c2)
    mask = tk < tq + jnp.where(r2 >= c, 1, 0)
    zeros_cc = jnp.zeros((c, hs), BF16)

    dn_t = (((1,), (1,)), ((), ()))
    dn_l = (((0,), (0,)), ((), ()))

    heads = range(RWKV_HEADS)
    sls = [slice(h * hs, (h + 1) * hs) for h in heads]
    s0s = [s_refs[h][...] for h in heads]
    aas = []
    for sl in sls:
        lhs = jnp.concatenate([lhs_a[:, sl], lhs_r[:, sl]], axis=0)
        rhs = jnp.concatenate([rhs_b[:, sl], rhs_k[:, sl]], axis=0)
        aa = lax.dot_general(lhs, rhs, dn_t, preferred_element_type=F32)
        aas.append(jnp.where(mask, aa, 0.0))
    v_hs = [v_all[:, sl] for sl in sls]
    a_rs = [aa[c:, :].astype(BF16) for aa in aas]
    ps = [aa[:c, :c] for aa in aas]
    xs = []
    for h in heads:
        akv = jnp.dot(aas[h][:c, c:].astype(BF16), v_hs[h], preferred_element_type=F32)
        xs.append(jnp.concatenate([lhs_a[:, sls[h]].astype(F32), akv], axis=1))
    n_sq = 6
    for it in range(n_sq):
        pbs = [p.astype(BF16) for p in ps]
        xs = [x + jnp.dot(pb, x.astype(BF16), preferred_element_type=F32)
              for x, pb in zip(xs, pbs)]
        if it + 1 < n_sq:
            ps = [jnp.dot(pb, pb, preferred_element_type=F32) for pb in pbs]
    for h in heads:
        sl = sls[h]
        rhs2 = jnp.concatenate(
            [xs[h].astype(BF16), jnp.concatenate([zeros_cc, v_hs[h]], axis=1)], axis=0)
        qy = jnp.dot(a_rs[h], rhs2, preferred_element_type=F32)
        dec = jnp.concatenate([dec_b[:, sl], dec_k[:, sl]], axis=0)
        mnt = lax.dot_general(rhs2, dec, dn_l, preferred_element_type=F32)
        s0b = s0s[h].astype(BF16)
        q = (rg_f32[:, sl] + qy[:, :hs]).astype(BF16)
        y_ref[:, sl] = lax.dot_general(q, s0b, dn_t, preferred_element_type=F32) + qy[:, hs:]
        s_refs[h][...] = (s0s[h] * g_last[:, sl]
                          + jnp.dot(s0b, mnt[:hs, :].astype(BF16), preferred_element_type=F32)
                          + mnt[hs:, :])


def _rw_chunk(r, k, v, kk, b, lw, bsz, t):
    n = bsz * t
    nc = t // CHUNK
    spec = lambda: pl.BlockSpec((CHUNK, RWKV_WIDTH), lambda bb, i: (bb * nc + i, 0))
    return pl.pallas_call(
        _rw_chunk_kernel,
        grid=(bsz, nc),
        in_specs=[spec() for _ in range(6)],
        out_specs=spec(),
        out_shape=jax.ShapeDtypeStruct((n, RWKV_WIDTH), F32),
        scratch_shapes=[pltpu.VMEM((HEAD_SIZE, HEAD_SIZE), F32) for _ in range(RWKV_HEADS)],
        compiler_params=_cparams(("parallel", "arbitrary")),
        name="rw_chunk",
    )(r, k, v, kk, b, lw)


def _rw_post_kernel(y_ref, r_ref, k_ref, v_ref, g_ref, rk_ref, gg_ref, gb_ref, ones_ref, o_ref):
    y = y_ref[...]
    inv = 1.0 / HEAD_SIZE
    mu = _head_sum(y, ones_ref) * inv
    yc = y - mu
    var = _head_sum(yc * yc, ones_ref) * inv
    yn = yc * lax.rsqrt(var + GN_EPS) * gg_ref[...] + gb_ref[...]
    bonus = _head_sum(r_ref[...] * k_ref[...] * rk_ref[...], ones_ref) * v_ref[...]
    o_ref[...] = ((yn + bonus) * g_ref[...]).astype(o_ref.dtype)


def _rw_post(y, r, k, v, g, r_k, gn_g, gn_b, ones_bd, tt):
    n = y.shape[0]
    row = lambda: pl.BlockSpec((tt, RWKV_WIDTH), lambda i: (i, 0))
    vec = lambda: pl.BlockSpec((1, RWKV_WIDTH), lambda i: (0, 0))
    return pl.pallas_call(
        _rw_post_kernel,
        grid=(n // tt,),
        in_specs=[row(), row(), row(), row(), row(), vec(), vec(), vec(),
                  pl.BlockSpec((RWKV_WIDTH, RWKV_WIDTH), lambda i: (0, 0))],
        out_specs=row(),
        out_shape=jax.ShapeDtypeStruct((n, RWKV_WIDTH), BF16),
        compiler_params=_cparams(("parallel",)),
        name="rw_post",
    )(y, r, k, v, g, r_k, gn_g, gn_b, ones_bd)


def _out_proj_kernel(yl_ref, yr_ref, h0_ref, wo_ref, g_ref, b_ref, wr_ref, br_ref,
                     h1_ref, h1p_ref, ti_ref, tg_ref, rank_ref, cnt_ref, carry, *, tm):
    @pl.when(pl.program_id(0) == 0)
    def _():
        carry[...] = jnp.zeros_like(carry)

    mix = (jnp.dot(yl_ref[...], wo_ref[0:LRU_WIDTH, :], preferred_element_type=F32)
           + jnp.dot(yr_ref[...], wo_ref[LRU_WIDTH:, :], preferred_element_type=F32))
    h1 = _layer_norm_rows(DEEPNORM_ALPHA * h0_ref[...] + mix, g_ref[...], b_ref[...])
    h1_ref[...] = h1
    h1p_ref[...] = _pack_halves(h1)

    h_hi, h_lo = _split2(h1)
    w_hi = wr_ref[0]
    w_lo = wr_ref[1]
    logits = (jnp.dot(h_hi, w_hi, preferred_element_type=F32)
              + jnp.dot(h_hi, w_lo, preferred_element_type=F32)
              + jnp.dot(h_lo, w_hi, preferred_element_type=F32)) + br_ref[...]

    lane = lax.broadcasted_iota(jnp.int32, (tm, N_EXPERTS), 1).astype(F32)
    lane4 = lax.broadcasted_iota(jnp.int32, (tm, TOP_K), 1)
    work = logits
    vals, idxs, sels = [], [], []
    for _ in range(TOP_K):
        m = jnp.max(work, axis=-1, keepdims=True)
        idx = jnp.min(jnp.where(work == m, lane, float(N_EXPERTS)), axis=-1, keepdims=True)
        sel = lane == idx
        vals.append(m)
        idxs.append(idx)
        sels.append(sel)
        work = jnp.where(sel, -jnp.inf, work)
    exps = [jnp.exp(vv - vals[0]) for vv in vals]
    denom = exps[0] + exps[1] + exps[2] + exps[3]

    member = jnp.zeros((tm, N_EXPERTS), F32)
    for sel in sels:
        member = member + sel.astype(F32)
    ri = lax.broadcasted_iota(jnp.int32, (tm, tm), 0)
    ci = lax.broadcasted_iota(jnp.int32, (tm, tm), 1)
    before = (ci < ri).astype(BF16)
    rank_full = jnp.dot(before, member.astype(BF16), preferred_element_type=F32) + carry[...]
    carry[...] = carry[...] + jnp.sum(member, axis=0, keepdims=True)
    cnt_ref[...] = carry[...].astype(jnp.int32)

    ti = jnp.zeros((tm, TOP_K), jnp.int32)
    tg = jnp.zeros((tm, TOP_K), F32)
    rk = jnp.zeros((tm, TOP_K), jnp.int32)
    for kq in range(TOP_K):
        rank_k = jnp.sum(jnp.where(sels[kq], rank_full, 0.0), axis=-1, keepdims=True)
        ti = jnp.where(lane4 == kq, idxs[kq].astype(jnp.int32), ti)
        tg = jnp.where(lane4 == kq, exps[kq] / denom, tg)
        rk = jnp.where(lane4 == kq, rank_k.astype(jnp.int32), rk)
    ti_ref[...] = ti
    tg_ref[...] = tg
    rank_ref[...] = rk


def _out_proj(y_lru, y_rw, h0, w_out_bf16, g, b, w_router_split, b_router, tm):
    n = h0.shape[0]
    row = lambda w: pl.BlockSpec((tm, w), lambda i: (i, 0))
    vec = lambda w: pl.BlockSpec((1, w), lambda i: (0, 0))
    return pl.pallas_call(
        functools.partial(_out_proj_kernel, tm=tm),
        grid=(n // tm,),
        in_specs=[
            row(LRU_WIDTH), row(RWKV_WIDTH), row(D_MODEL),
            pl.BlockSpec((D_MODEL, D_MODEL), lambda i: (0, 0)),
            vec(D_MODEL), vec(D_MODEL),
            pl.BlockSpec((2, D_MODEL, N_EXPERTS), lambda i: (0, 0, 0)),
            vec(N_EXPERTS),
        ],
        out_specs=[row(D_MODEL), row(HALF), row(TOP_K), row(TOP_K), row(TOP_K), vec(N_EXPERTS)],
        out_shape=[
            jax.ShapeDtypeStruct((n, D_MODEL), F32),
            jax.ShapeDtypeStruct((n, HALF), U32),
            jax.ShapeDtypeStruct((n, TOP_K), jnp.int32),
            jax.ShapeDtypeStruct((n, TOP_K), F32),
            jax.ShapeDtypeStruct((n, TOP_K), jnp.int32),
            jax.ShapeDtypeStruct((1, N_EXPERTS), jnp.int32),
        ],
        scratch_shapes=[pltpu.VMEM((1, N_EXPERTS), F32)],
        compiler_params=_cparams(("arbitrary",)),
        name="out_proj",
    )(y_lru, y_rw, h0, w_out_bf16, g, b, w_router_split, b_router)


def _scatter_kernel(dest_ref, fill_ref, h_ref, xs_ref, zbuf, sem, zsem, *, tb, tm, nb):
    @pl.when(pl.program_id(0) == 0)
    def _():
        zbuf[...] = jnp.zeros_like(zbuf)

        def fill_copy(start):
            start = pl.multiple_of(start, SUBLANE)
            return pltpu.make_async_copy(zbuf, xs_ref.at[pl.ds(start, tm), :], zsem)

        def fill(e, carry):
            @pl.when(fill_ref[e] >= 0)
            def _():
                fill_copy(fill_ref[e]).start()
            return carry

        def fill_wait(e, carry):
            @pl.when(fill_ref[e] >= 0)
            def _():
                fill_copy(fill_ref[e]).wait()
            return carry

        def tail(blk, carry):
            fill_copy(blk * tm).start()
            return carry

        def tail_wait(blk, carry):
            fill_copy(blk * tm).wait()
            return carry

        n_used = fill_ref[N_EXPERTS]
        lax.fori_loop(0, N_EXPERTS, fill, 0)
        lax.fori_loop(n_used, nb, tail, 0)
        lax.fori_loop(0, N_EXPERTS, fill_wait, 0)
        lax.fori_loop(n_used, nb, tail_wait, 0)

    base = pl.program_id(0) * (tb * TOP_K)

    def copy(t, kq):
        slot = dest_ref[base + t * TOP_K + kq]
        return pltpu.make_async_copy(h_ref.at[pl.ds(t, 1), :], xs_ref.at[pl.ds(slot, 1), :], sem)

    def issue(t, carry):
        for kq in range(TOP_K):
            copy(t, kq).start()
        return carry

    lax.fori_loop(0, tb, issue, 0)
    for kq in range(TOP_K):
        pltpu.make_async_copy(h_ref, xs_ref.at[pl.ds(0, tb), :], sem).wait()


def _scatter(dest_flat, fill_start, h1p, n_slots, tb, tm):
    n = h1p.shape[0]
    nb = n_slots // tm
    grid_spec = pltpu.PrefetchScalarGridSpec(
        num_scalar_prefetch=2,
        grid=(n // tb,),
        in_specs=[pl.BlockSpec((tb, HALF), lambda i, d, f: (i, 0))],
        out_specs=pl.BlockSpec(memory_space=pl.ANY),
        scratch_shapes=[pltpu.VMEM((tm, HALF), U32), pltpu.SemaphoreType.DMA(()),
                        pltpu.SemaphoreType.DMA(())],
    )
    return pl.pallas_call(
        functools.partial(_scatter_kernel, tb=tb, tm=tm, nb=nb),
        grid_spec=grid_spec,
        out_shape=jax.ShapeDtypeStruct((n_slots, HALF), U32),
        compiler_params=_cparams(("arbitrary",)),
        name="scatter",
    )(dest_flat, fill_start, h1p)


def _gmm1_kernel(meta_ref, x_ref, wg_ref, wu_ref, bg_ref, bu_ref, o_ref, wgb, wub, *, nb):
    i = pl.program_id(1)
    e = meta_ref[i]
    prev_e = meta_ref[jnp.maximum(i - 1, 0)]

    @pl.when(jnp.logical_or(i == 0, e != prev_e))
    def _():
        wgb[...] = wg_ref[0].astype(BF16)
        wub[...] = wu_ref[0].astype(BF16)

    @pl.when(i < meta_ref[nb])
    def _():
        x_lo, x_hi = _unpack_halves(x_ref[...])
        x_lo = x_lo.astype(BF16)
        x_hi = x_hi.astype(BF16)
        gate = (jnp.dot(x_lo, wgb[0:HALF, :], preferred_element_type=F32)
                + jnp.dot(x_hi, wgb[HALF:, :], preferred_element_type=F32) + bg_ref[0])
        up = (jnp.dot(x_lo, wub[0:HALF, :], preferred_element_type=F32)
              + jnp.dot(x_hi, wub[HALF:, :], preferred_element_type=F32) + bu_ref[0])
        gate = jnp.minimum(gate, SWIGLU_LIMIT)
        up = jnp.clip(up, -SWIGLU_LIMIT, SWIGLU_LIMIT)
        act = gate * _sigmoid(SWIGLU_ALPHA * gate) * (up + 1.0)
        o_ref[...] = act.astype(o_ref.dtype)

    @pl.when(i >= meta_ref[nb])
    def _():
        o_ref[...] = jnp.zeros_like(o_ref)


def _gmm1(meta, xs, w1, b1, nb, tm, tn):
    n_slots = xs.shape[0]
    nj = D_EXPERT // tn

    def row_blk(j, i, m):
        return jnp.minimum(i, m[nb] - 1)

    grid_spec = pltpu.PrefetchScalarGridSpec(
        num_scalar_prefetch=1,
        grid=(nj, nb),
        in_specs=[
            pl.BlockSpec((tm, HALF), lambda j, i, m: (row_blk(j, i, m), 0)),
            pl.BlockSpec((1, D_MODEL, tn), lambda j, i, m: (m[i], 0, j)),
            pl.BlockSpec((1, D_MODEL, tn), lambda j, i, m: (m[i], 0, j + nj)),
            pl.BlockSpec((1, 1, tn), lambda j, i, m: (m[i], 0, j)),
            pl.BlockSpec((1, 1, tn), lambda j, i, m: (m[i], 0, j + nj)),
        ],
        out_specs=pl.BlockSpec((tm, tn), lambda j, i, m: (i, j)),
        scratch_shapes=[pltpu.VMEM((D_MODEL, tn), BF16), pltpu.VMEM((D_MODEL, tn), BF16)],
    )
    return pl.pallas_call(
        functools.partial(_gmm1_kernel, nb=nb),
        grid_spec=grid_spec,
        out_shape=jax.ShapeDtypeStruct((n_slots, D_EXPERT), BF16),
        compiler_params=_cparams(("arbitrary", "arbitrary")),
        name="gmm1",
    )(meta, xs, w1, w1, b1, b1)


def _gmm2_kernel(meta_ref, a_ref, w_ref, b_ref, o_ref, wb, *, nb):
    i = pl.program_id(1)
    e = meta_ref[i]
    prev_e = meta_ref[jnp.maximum(i - 1, 0)]

    @pl.when(jnp.logical_or(i == 0, e != prev_e))
    def _():
        wb[...] = w_ref[0].astype(BF16)

    @pl.when(i < meta_ref[nb])
    def _():
        out = jnp.dot(a_ref[...], wb[...], preferred_element_type=F32) + b_ref[0]
        o_ref[...] = _pack_halves(out)

    @pl.when(i >= meta_ref[nb])
    def _():
        o_ref[...] = jnp.zeros_like(o_ref)


def _gmm2(meta, act, w2, b2, nb, tm):
    n_slots = act.shape[0]

    def row_blk(j, i, m):
        return jnp.minimum(i, m[nb] - 1)

    grid_spec = pltpu.PrefetchScalarGridSpec(
        num_scalar_prefetch=1,
        grid=(1, nb),
        in_specs=[
            pl.BlockSpec((tm, D_EXPERT), lambda j, i, m: (row_blk(j, i, m), 0)),
            pl.BlockSpec((1, D_EXPERT, D_MODEL), lambda j, i, m: (m[i], 0, 0)),
            pl.BlockSpec((1, 1, D_MODEL), lambda j, i, m: (m[i], 0, 0)),
        ],
        out_specs=pl.BlockSpec((tm, HALF), lambda j, i, m: (i, 0)),
        scratch_shapes=[pltpu.VMEM((D_EXPERT, D_MODEL), BF16)],
    )
    return pl.pallas_call(
        functools.partial(_gmm2_kernel, nb=nb),
        grid_spec=grid_spec,
        out_shape=jax.ShapeDtypeStruct((n_slots, HALF), U32),
        compiler_params=_cparams(("arbitrary", "arbitrary")),
        name="gmm2",
    )(meta, act, w2, b2)


def _combine_kernel(dest_ref, h1_ref, tg_ref, g_ref, b_ref, ys_ref, o_ref, buf, sems, *, tb, n_steps):
    i = pl.program_id(0)

    def issue(step, half):
        base = step * (tb * TOP_K)

        def body(t, carry):
            for kq in range(TOP_K):
                slot = dest_ref[base + t * TOP_K + kq]
                pltpu.make_async_copy(ys_ref.at[pl.ds(slot, 1), :],
                                      buf.at[half, kq, pl.ds(t, 1), :], sems.at[half]).start()
            return carry

        lax.fori_loop(0, tb, body, 0)

    @pl.when(i == 0)
    def _():
        issue(0, 0)

    @pl.when(i + 1 < n_steps)
    def _():
        issue(i + 1, (i + 1) % 2)

    half = i % 2
    for kq in range(TOP_K):
        pltpu.make_async_copy(ys_ref.at[pl.ds(0, tb), :], buf.at[half, kq], sems.at[half]).wait()

    tg = tg_ref[...]
    y = DEEPNORM_ALPHA * h1_ref[...]
    for kq in range(TOP_K):
        lo, hi = _unpack_halves(buf[half, kq])
        y = y + tg[:, kq:kq + 1] * jnp.concatenate([lo, hi], axis=1)
    o_ref[...] = _layer_norm_rows(y, g_ref[...], b_ref[...])


def _combine(dest_flat, h1, tg, g, b, ys, tb):
    n = h1.shape[0]
    grid_spec = pltpu.PrefetchScalarGridSpec(
        num_scalar_prefetch=1,
        grid=(n // tb,),
        in_specs=[
            pl.BlockSpec((tb, D_MODEL), lambda i, d: (i, 0)),
            pl.BlockSpec((tb, TOP_K), lambda i, d: (i, 0)),
            pl.BlockSpec((1, D_MODEL), lambda i, d: (0, 0)),
            pl.BlockSpec((1, D_MODEL), lambda i, d: (0, 0)),
            pl.BlockSpec(memory_space=pl.ANY),
        ],
        out_specs=pl.BlockSpec((tb, D_MODEL), lambda i, d: (i, 0)),
        scratch_shapes=[pltpu.VMEM((2, TOP_K, tb, HALF), U32), pltpu.SemaphoreType.DMA((2,))],
    )
    return pl.pallas_call(
        functools.partial(_combine_kernel, tb=tb, n_steps=n // tb),
        grid_spec=grid_spec,
        out_shape=jax.ShapeDtypeStruct((n, D_MODEL), F32),
        compiler_params=_cparams(("arbitrary",)),
        name="combine",
    )(dest_flat, h1, tg, g, b, ys)


def _block_diag(w, group):
    nb = w.shape[0] // group
    w = w.reshape(nb, group, LRU_BLOCK_W, LRU_BLOCK_W)
    eye = jnp.eye(group, dtype=w.dtype)
    out = jnp.einsum('gajk,ab->gajbk', w, eye)
    return out.reshape(nb, group * LRU_BLOCK_W, group * LRU_BLOCK_W)


def _pad_rows(w, rows):
    return jnp.pad(w, ((0, rows - w.shape[0]), (0, 0)))


def _layer(h_in_x, l, prm, bsz, t):
    n = bsz * t
    row = lambda a: a.reshape(1, -1)

    w_in = prm['w_in'][l]
    o_rw = 2 * LRU_WIDTH
    o_l = o_rw + 3 * RWKV_WIDTH
    pad_c = lambda w: jnp.pad(w, ((0, 0), (0, LORA_PAD - w.shape[1])))
    w_in_p = jnp.concatenate([
        w_in[:, :o_l],
        pad_c(w_in[:, o_l:o_l + DECAY_LORA]),
        pad_c(w_in[:, o_l + DECAY_LORA:o_l + DECAY_LORA + AAA_LORA]),
        w_in[:, o_l + DECAY_LORA + AAA_LORA:],
    ], axis=1).astype(BF16)
    mu = prm['shift_mu'][l]
    pad_v = lambda v: jnp.pad(v, (0, LORA_PAD - v.shape[0]))
    mu_l = mu[3 * RWKV_WIDTH:]
    mu_a = jnp.concatenate([pad_v(mu_l[:DECAY_LORA]), pad_v(mu_l[DECAY_LORA:DECAY_LORA + AAA_LORA]),
                            mu_l[DECAY_LORA + AAA_LORA:]])

    tm_in = min(512, n)
    h0, p = _in_proj(h_in_x, row(prm['ln_in_g']), row(prm['ln_in_b']), w_in_p, tm_in, P_WIDTH // 4)

    tt = min(512, t)
    group = MXU_DIM // LRU_BLOCK_W
    y_lru = _lru(p, prm['conv_w'][l], row(prm['conv_b'][l]),
                 _block_diag(prm['w_rgate'][l], group).astype(BF16), row(prm['b_rgate'][l]),
                 _block_diag(prm['w_igate'][l], group).astype(BF16), row(prm['b_igate'][l]),
                 row(prm['lru_lambda'][l]), bsz, t, tt)

    head_id = jnp.arange(RWKV_WIDTH) // HEAD_SIZE
    ones_bd = (head_id[:, None] == head_id[None, :]).astype(BF16)
    tt_rw = min(256, t)
    r, k, v, kk, bvec, lw, g = _rw_prep(
        p, row(mu[:RWKV_WIDTH]), row(mu[RWKV_WIDTH:2 * RWKV_WIDTH]),
        row(mu[2 * RWKV_WIDTH:3 * RWKV_WIDTH]), row(mu_a),
        row(prm['w0'][l]), _pad_rows(prm['rw_decay_up'][l], LORA_PAD).astype(BF16),
        row(prm['a0'][l]), _pad_rows(prm['rw_aaa_up'][l], LORA_PAD).astype(BF16),
        prm['rw_gate_up'][l].astype(BF16), row(prm['k_k'][l]), row(prm['k_a'][l]),
        ones_bd, bsz, t, tt_rw)
    y = _rw_chunk(r, k, v, kk, bvec, lw, bsz, t)
    y_rw = _rw_post(y, r, k, v, g, row(prm['r_k'][l]), row(prm['gn_g'][l]), row(prm['gn_b'][l]),
                    ones_bd, min(512, n))

    w_r = prm['w_router'][l]
    w_r_hi = w_r.astype(BF16)
    w_r_lo = (w_r - w_r_hi.astype(F32)).astype(BF16)
    tm_out = min(512, n)
    h1, h1p, top_i, top_g, rank, counts = _out_proj(
        y_lru, y_rw, h0, prm['w_out'][l].astype(BF16), row(prm['ln1_g'][l]), row(prm['ln1_b'][l]),
        jnp.stack([w_r_hi, w_r_lo]), row(prm['b_router'][l]), tm_out)

    tm = 256
    counts = counts.reshape(N_EXPERTS)
    padded = ((counts + tm - 1) // tm) * tm
    pad_end = jnp.cumsum(padded)
    pad_start = pad_end - padded
    dest = (pad_start[top_i] + rank).reshape(-1).astype(jnp.int32)
    nb = (n * TOP_K) // tm + N_EXPERTS
    n_slots = nb * tm
    n_used = (pad_end[-1] // tm).astype(jnp.int32)
    blk = jnp.minimum(jnp.arange(nb, dtype=jnp.int32), n_used - 1)
    blk_e = jnp.sum((pad_end[None, :] <= (blk * tm)[:, None]).astype(jnp.int32), axis=1)
    blk_e = jnp.minimum(blk_e, N_EXPERTS - 1)
    meta = jnp.concatenate([blk_e.astype(jnp.int32), n_used.reshape(1)])

    fill_start = jnp.concatenate([jnp.where(padded > 0, pad_end - tm, -1).astype(jnp.int32),
                                  n_used.reshape(1)])
    tb = min(128, n)
    xs = _scatter(dest, fill_start, h1p, n_slots, tb, tm)
    act = _gmm1(meta, xs, prm['w_exp1'][l], prm['b_exp1'][l].reshape(N_EXPERTS, 1, 2 * D_EXPERT),
                nb, tm, 1024)
    ys = _gmm2(meta, act, prm['w_exp2'][l], prm['b_exp2'][l].reshape(N_EXPERTS, 1, D_MODEL),
               nb, tm)
    return _combine(dest, h1, top_g, row(prm['ln2_g'][l]), row(prm['ln2_b'][l]),
                    ys, tb)


def kernel(x, ln_in_g, ln_in_b, w_in, conv_w, conv_b, w_rgate, b_rgate, w_igate, b_igate, lru_lambda, shift_mu, w0, rw_decay_up, a0, rw_aaa_up, rw_gate_up, k_k, k_a, r_k, gn_g, gn_b, w_out, ln1_g, ln1_b, w_router, b_router, w_exp1, b_exp1, w_exp2, b_exp2, ln2_g, ln2_b):
    bsz, t, d = x.shape
    prm = dict(ln_in_g=ln_in_g, ln_in_b=ln_in_b, w_in=w_in, conv_w=conv_w, conv_b=conv_b,
               w_rgate=w_rgate, b_rgate=b_rgate, w_igate=w_igate, b_igate=b_igate,
               lru_lambda=lru_lambda, shift_mu=shift_mu, w0=w0, rw_decay_up=rw_decay_up, a0=a0,
               rw_aaa_up=rw_aaa_up, rw_gate_up=rw_gate_up, k_k=k_k, k_a=k_a, r_k=r_k, gn_g=gn_g,
               gn_b=gn_b, w_out=w_out, ln1_g=ln1_g, ln1_b=ln1_b, w_router=w_router,
               b_router=b_router, w_exp1=w_exp1, b_exp1=b_exp1, w_exp2=w_exp2, b_exp2=b_exp2,
               ln2_g=ln2_g, ln2_b=ln2_b)
    out = _layer(x.reshape(bsz * t, d), 0, prm, bsz, t)
    return out.reshape(bsz, t, d)
```

```python
import functools

import jax
import jax.numpy as jnp
from jax import lax
from jax.experimental import pallas as pl
from jax.experimental.pallas import tpu as pltpu

D_MODEL = 2048
DEPTH = 1
CHUNK = 64
LRU_WIDTH = 1024
LRU_BLOCKS = 16
LRU_BLOCK_W = LRU_WIDTH // LRU_BLOCKS
CONV_WIDTH = 4
RG_C = 8.0
RWKV_WIDTH = D_MODEL - LRU_WIDTH
HEAD_SIZE = 64
RWKV_HEADS = RWKV_WIDTH // HEAD_SIZE
DECAY_LORA = 96
AAA_LORA = 96
GATE_LORA = 256
N_EXPERTS = 32
TOP_K = 4
D_EXPERT = D_MODEL
SWIGLU_LIMIT = 7.0
SWIGLU_ALPHA = 1.702
LN_EPS = 1e-5
GN_EPS = HEAD_SIZE * 1e-5
DEEPNORM_ALPHA = (2.0 * DEPTH) ** 0.25

LANE = 128
SUBLANE = 8
MXU_DIM = 256
LORA_PAD = 128
LORA_W = 2 * LORA_PAD + GATE_LORA
P_WIDTH = 2 * LRU_WIDTH + 3 * RWKV_WIDTH + LORA_W
VMEM_LIMIT = 56 * 1024 * 1024

F32 = jnp.float32
BF16 = jnp.bfloat16


def _cparams(sem):
    return pltpu.CompilerParams(dimension_semantics=sem, vmem_limit_bytes=VMEM_LIMIT)


def _layer_norm_rows(x, g, b):
    mu = jnp.mean(x, axis=-1, keepdims=True)
    xc = x - mu
    var = jnp.mean(xc * xc, axis=-1, keepdims=True)
    return xc * lax.rsqrt(var + LN_EPS) * g + b


def _softplus(z):
    return jnp.maximum(z, 0.0) + jnp.log1p(jnp.exp(-jnp.abs(z)))


def _sigmoid(z):
    return 1.0 / (1.0 + jnp.exp(-z))


def _split2(x):
    hi = x.astype(BF16)
    lo = (x - hi.astype(F32)).astype(BF16)
    return hi, lo


HALF = D_MODEL // 2
U32 = jnp.uint32


def _pack_halves(x):
    lo = lax.bitcast_convert_type(x[:, :HALF].astype(BF16).astype(F32), U32) >> 16
    hi = lax.bitcast_convert_type(x[:, HALF:].astype(BF16).astype(F32), U32) & jnp.uint32(0xFFFF0000)
    return hi | lo


def _unpack_halves(w):
    lo = lax.bitcast_convert_type(w << 16, F32)
    hi = lax.bitcast_convert_type(w & jnp.uint32(0xFFFF0000), F32)
    return lo, hi


def _head_sum(x, ones_ref):
    hi, lo = _split2(x)
    ones = ones_ref[...]
    parts = []
    for g in range(x.shape[1] // MXU_DIM):
        sl = slice(g * MXU_DIM, (g + 1) * MXU_DIM)
        parts.append(jnp.dot(hi[:, sl], ones, preferred_element_type=F32)
                     + jnp.dot(lo[:, sl], ones, preferred_element_type=F32))
    return jnp.concatenate(parts, axis=1)


def _in_proj_kernel(x_ref, g_ref, b_ref, w_ref, h_ref, p_ref, hb_ref):
    @pl.when(pl.program_id(1) == 0)
    def _():
        h = _layer_norm_rows(x_ref[...], g_ref[...], b_ref[...])
        h_ref[...] = h
        hb_ref[...] = h.astype(BF16)

    p_ref[...] = jnp.dot(hb_ref[...], w_ref[...], preferred_element_type=F32)


def _in_proj(x2, g, b, w_bf16, tm, tn):
    n = x2.shape[0]
    return pl.pallas_call(
        _in_proj_kernel,
        grid=(n // tm, P_WIDTH // tn),
        in_specs=[
            pl.BlockSpec((tm, D_MODEL), lambda i, j: (i, 0)),
            pl.BlockSpec((1, D_MODEL), lambda i, j: (0, 0)),
            pl.BlockSpec((1, D_MODEL), lambda i, j: (0, 0)),
            pl.BlockSpec((D_MODEL, tn), lambda i, j: (0, j)),
        ],
        out_specs=[
            pl.BlockSpec((tm, D_MODEL), lambda i, j: (i, 0)),
            pl.BlockSpec((tm, tn), lambda i, j: (i, j)),
        ],
        out_shape=[
            jax.ShapeDtypeStruct((n, D_MODEL), F32),
            jax.ShapeDtypeStruct((n, P_WIDTH), F32),
        ],
        scratch_shapes=[pltpu.VMEM((tm, D_MODEL), BF16)],
        compiler_params=_cparams(("parallel", "arbitrary")),
        name="in_proj",
    )(x2, g, b, w_bf16)


def _lru_kernel(u_ref, gi_ref, cw_ref, cb_ref, wr_ref, br_ref, wi_ref, bi_ref,
                lam_ref, o_ref, ubuf, a_s, b_s, carry, *, tt):
    first = pl.program_id(1) == 0

    @pl.when(first)
    def _():
        ubuf[0:SUBLANE, :] = jnp.zeros((SUBLANE, LRU_WIDTH), F32)
        carry[...] = jnp.zeros_like(carry)

    @pl.when(jnp.logical_not(first))
    def _():
        ubuf[0:SUBLANE, :] = ubuf[tt:tt + SUBLANE, :]

    ubuf[SUBLANE:, :] = u_ref[...]
    uc = cb_ref[...]
    for i in range(CONV_WIDTH):
        off = SUBLANE - (CONV_WIDTH - 1) + i
        uc = uc + cw_ref[i:i + 1, :] * ubuf[off:off + tt, :]

    ucb = uc.astype(BF16)
    n_grp = LRU_WIDTH // MXU_DIM

    def gate(w_ref, bias_ref):
        parts = [jnp.dot(ucb[:, g * MXU_DIM:(g + 1) * MXU_DIM], w_ref[g],
                         preferred_element_type=F32) for g in range(n_grp)]
        return _sigmoid(jnp.concatenate(parts, axis=1) + bias_ref[...])

    r_gate = gate(wr_ref, br_ref)
    i_gate = gate(wi_ref, bi_ref)
    log_a = (-RG_C * r_gate) * _softplus(-lam_ref[...])
    a_s[...] = jnp.exp(log_a)
    th = jnp.tanh(log_a)
    b_s[...] = jnp.sqrt(-2.0 * th / (1.0 - th)) * (i_gate * uc)

    row = lax.broadcasted_iota(jnp.int32, (SUBLANE, LRU_WIDTH), 0)

    def group(gidx, c):
        off = pl.multiple_of(gidx * SUBLANE, SUBLANE)
        a = a_s[pl.ds(off, SUBLANE), :]
        b = b_s[pl.ds(off, SUBLANE), :]
        for d in (1, 2, 4):
            keep = row >= d
            a_sh = pltpu.roll(a, d, axis=0)
            b_sh = pltpu.roll(b, d, axis=0)
            b = jnp.where(keep, a * b_sh + b, b)
            a = jnp.where(keep, a * a_sh, a)
        h = a * c + b
        b_s[pl.ds(off, SUBLANE), :] = h
        return h[SUBLANE - 1:SUBLANE, :]

    carry[...] = lax.fori_loop(0, tt // SUBLANE, group, carry[...])
    o_ref[...] = (b_s[...] * jax.nn.gelu(gi_ref[...])).astype(o_ref.dtype)


def _lru(p, conv_w, conv_b, wr_bd, br, wi_bd, bi, lam, bsz, t, tt):
    n = bsz * t
    nt = t // tt
    vec = lambda: pl.BlockSpec((1, LRU_WIDTH), lambda b, i: (0, 0))
    wspec = lambda: pl.BlockSpec((LRU_WIDTH // MXU_DIM, MXU_DIM, MXU_DIM), lambda b, i: (0, 0, 0))
    return pl.pallas_call(
        functools.partial(_lru_kernel, tt=tt),
        grid=(bsz, nt),
        in_specs=[
            pl.BlockSpec((tt, LRU_WIDTH), lambda b, i: (b * nt + i, 0)),
            pl.BlockSpec((tt, LRU_WIDTH), lambda b, i: (b * nt + i, 1)),
            pl.BlockSpec((CONV_WIDTH, LRU_WIDTH), lambda b, i: (0, 0)),
            vec(), wspec(), vec(), wspec(), vec(), vec(),
        ],
        out_specs=pl.BlockSpec((tt, LRU_WIDTH), lambda b, i: (b * nt + i, 0)),
        out_shape=jax.ShapeDtypeStruct((n, LRU_WIDTH), BF16),
        scratch_shapes=[
            pltpu.VMEM((tt + SUBLANE, LRU_WIDTH), F32),
            pltpu.VMEM((tt, LRU_WIDTH), F32),
            pltpu.VMEM((tt, LRU_WIDTH), F32),
            pltpu.VMEM((1, LRU_WIDTH), F32),
        ],
        compiler_params=_cparams(("parallel", "arbitrary")),
        name="lru",
    )(p, p, conv_w, conv_b, wr_bd, br, wi_bd, bi, lam)


def _rw_prep_kernel(pr_ref, pk_ref, pv_ref, pa_ref, mur_ref, muk_ref, muv_ref, mua_ref,
                    w0_ref, wdec_ref, a0_ref, waaa_ref, wgate_ref, kk_ref, ka_ref, ones_ref,
                    r_out, k_out, v_out, kkn_out, b_out, lw_out, g_out,
                    prev_r, prev_k, prev_v, prev_a, *, tt):
    first = pl.program_id(1) == 0

    @pl.when(first)
    def _():
        for ref in (prev_r, prev_k, prev_v, prev_a):
            ref[...] = jnp.zeros_like(ref)

    def shift(x_ref, prev_ref, mu_ref):
        x = x_ref[...]
        row = lax.broadcasted_iota(jnp.int32, x.shape, 0)
        prev = jnp.where(row == 0, prev_ref[...], pltpu.roll(x, 1, axis=0))
        prev_ref[...] = x[tt - 1:tt, :]
        return x + (prev - x) * mu_ref[...]

    r = shift(pr_ref, prev_r, mur_ref)
    k = shift(pk_ref, prev_k, muk_ref)
    v = shift(pv_ref, prev_v, muv_ref)
    ad = shift(pa_ref, prev_a, mua_ref)

    wd = jnp.tanh(ad[:, 0:LORA_PAD]).astype(BF16)
    aa = ad[:, LORA_PAD:2 * LORA_PAD].astype(BF16)
    gd = _sigmoid(ad[:, 2 * LORA_PAD:]).astype(BF16)
    w_pre = w0_ref[...] + jnp.dot(wd, wdec_ref[...], preferred_element_type=F32)
    w_log = -_softplus(-w_pre) - 0.5
    a = _sigmoid(a0_ref[...] + jnp.dot(aa, waaa_ref[...], preferred_element_type=F32))
    g = jnp.dot(gd, wgate_ref[...], preferred_element_type=F32)

    kk = k * kk_ref[...]
    norm = jnp.sqrt(_head_sum(kk * kk, ones_ref))
    kk = kk / jnp.maximum(norm, 1e-12)

    r_out[...] = r
    k_out[...] = k * (1.0 + (a - 1.0) * ka_ref[...])
    v_out[...] = v
    kkn_out[...] = kk
    b_out[...] = kk * a
    lw_out[...] = -jnp.exp(w_log)
    g_out[...] = g


def _rw_prep(p, mu_r, mu_k, mu_v, mu_a, w0, wdec, a0, waaa, wgate, k_k, k_a, ones_bd, bsz, t, tt):
    n = bsz * t
    nt = t // tt
    cb = 2 * LRU_WIDTH // RWKV_WIDTH
    row = lambda c: pl.BlockSpec((tt, RWKV_WIDTH), lambda b, i: (b * nt + i, c))
    vec = lambda w: pl.BlockSpec((1, w), lambda b, i: (0, 0))
    full = lambda s: pl.BlockSpec(s, lambda b, i: (0, 0))
    lora_cb = (2 * LRU_WIDTH + 3 * RWKV_WIDTH) // LORA_W
    out = jax.ShapeDtypeStruct((n, RWKV_WIDTH), F32)
    return pl.pallas_call(
        functools.partial(_rw_prep_kernel, tt=tt),
        grid=(bsz, nt),
        in_specs=[
            row(cb), row(cb + 1), row(cb + 2),
            pl.BlockSpec((tt, LORA_W), lambda b, i: (b * nt + i, lora_cb)),
            vec(RWKV_WIDTH), vec(RWKV_WIDTH), vec(RWKV_WIDTH), vec(LORA_W),
            vec(RWKV_WIDTH), full((LORA_PAD, RWKV_WIDTH)),
            vec(RWKV_WIDTH), full((LORA_PAD, RWKV_WIDTH)),
            full((GATE_LORA, RWKV_WIDTH)),
            vec(RWKV_WIDTH), vec(RWKV_WIDTH),
            full((MXU_DIM, MXU_DIM)),
        ],
        out_specs=[pl.BlockSpec((tt, RWKV_WIDTH), lambda b, i: (b * nt + i, 0))] * 7,
        out_shape=[out] * 7,
        scratch_shapes=[pltpu.VMEM((1, RWKV_WIDTH), F32)] * 3 + [pltpu.VMEM((1, LORA_W), F32)],
        compiler_params=_cparams(("parallel", "arbitrary")),
        name="rw_prep",
    )(p, p, p, p, mu_r, mu_k, mu_v, mu_a, w0, wdec, a0, waaa, wgate, k_k, k_a, ones_bd)


def _rw_chunk_kernel(r_ref, k_ref, v_ref, kk_ref, b_ref, lw_ref, y_ref, *s_refs):
    c = CHUNK
    hs = HEAD_SIZE
    n_seq = r_ref.shape[0]

    @pl.when(pl.program_id(1) == 0)
    def _():
        for s_ref in s_refs:
            s_ref[...] = jnp.zeros_like(s_ref)

    ri = lax.broadcasted_iota(jnp.int32, (c, c), 0)
    ci = lax.broadcasted_iota(jnp.int32, (c, c), 1)
    tri = (ri >= ci).astype(BF16)

    r2 = lax.broadcasted_iota(jnp.int32, (2 * c, 2 * c), 0)
    c2 = lax.broadcasted_iota(jnp.int32, (2 * c, 2 * c), 1)
    tq = jnp.where(r2 >= c, r2 - c, r2)
    tk = jnp.where(c2 >= c, c2 - c, c2)
    mask = tk < tq + jnp.where(r2 >= c, 1, 0)
    zeros_cc = jnp.zeros((c, hs), BF16)

    dn_t = (((1,), (1,)), ((), ()))
    dn_l = (((0,), (0,)), ((), ()))

    def scaled_operands(q):
        lw = lw_ref[q]
        lw_hi = lw.astype(BF16)
        rem = lw - lw_hi.astype(F32)
        lw_mid = rem.astype(BF16)
        lw_lo = (rem - lw_mid.astype(F32)).astype(BF16)
        cl = (jnp.dot(tri, lw_hi, preferred_element_type=F32)
              + jnp.dot(tri, lw_mid, preferred_element_type=F32)
              + jnp.dot(tri, lw_lo, preferred_element_type=F32))
        cl_last = cl[c - 1:c, :]
        g_inv = jnp.exp(-cl)
        g_dec = jnp.exp(cl_last - cl)
        bv = b_ref[q]
        kx = k_ref[q]
        rg = r_ref[q] * jnp.exp(cl)
        return dict(
            g_last=jnp.exp(cl_last),
            v=v_ref[q].astype(BF16),
            rg=rg,
            lhs_a=(-kk_ref[q] * jnp.exp(cl - lw)).astype(BF16),
            lhs_r=rg.astype(BF16),
            rhs_b=(bv * g_inv).astype(BF16),
            rhs_k=(kx * g_inv).astype(BF16),
            dec_b=(bv * g_dec).astype(BF16),
            dec_k=(kx * g_dec).astype(BF16),
        )

    ops = [scaled_operands(q) for q in range(n_seq)]

    chains = [(q, h) for q in range(n_seq) for h in range(RWKV_HEADS)]
    sl_of = lambda h: slice(h * hs, (h + 1) * hs)
    s0s = [s_refs[q * RWKV_HEADS + h][...] for q, h in chains]
    aas = []
    for q, h in chains:
        o, sl = ops[q], sl_of(h)
        lhs = jnp.concatenate([o['lhs_a'][:, sl], o['lhs_r'][:, sl]], axis=0)
        rhs = jnp.concatenate([o['rhs_b'][:, sl], o['rhs_k'][:, sl]], axis=0)
        aa = lax.dot_general(lhs, rhs, dn_t, preferred_element_type=F32)
        aas.append(jnp.where(mask, aa, 0.0))
    v_hs = [ops[q]['v'][:, sl_of(h)] for q, h in chains]
    a_rs = [aa[c:, :].astype(BF16) for aa in aas]
    ps = [aa[:c, :c] for aa in aas]
    xs = []
    for i, (q, h) in enumerate(chains):
        akv = jnp.dot(aas[i][:c, c:].astype(BF16), v_hs[i], preferred_element_type=F32)
        xs.append(jnp.concatenate([ops[q]['lhs_a'][:, sl_of(h)].astype(F32), akv], axis=1))
    n_sq = 6
    for it in range(n_sq):
        pbs = [p.astype(BF16) for p in ps]
        xs = [x + jnp.dot(pb, x.astype(BF16), preferred_element_type=F32)
              for x, pb in zip(xs, pbs)]
        if it + 1 < n_sq:
            ps = [jnp.dot(pb, pb, preferred_element_type=F32) for pb in pbs]
    for i, (q, h) in enumerate(chains):
        o, sl = ops[q], sl_of(h)
        rhs2 = jnp.concatenate(
            [xs[i].astype(BF16), jnp.concatenate([zeros_cc, v_hs[i]], axis=1)], axis=0)
        qy = jnp.dot(a_rs[i], rhs2, preferred_element_type=F32)
        dec = jnp.concatenate([o['dec_b'][:, sl], o['dec_k'][:, sl]], axis=0)
        mnt = lax.dot_general(rhs2, dec, dn_l, preferred_element_type=F32)
        s0b = s0s[i].astype(BF16)
        qq = (o['rg'][:, sl] + qy[:, :hs]).astype(BF16)
        y_ref[q, :, sl] = lax.dot_general(qq, s0b, dn_t, preferred_element_type=F32) + qy[:, hs:]
        s_refs[i][...] = (s0s[i] * o['g_last'][:, sl]
                          + jnp.dot(s0b, mnt[:hs, :].astype(BF16), preferred_element_type=F32)
                          + mnt[hs:, :])


def _rw_chunk(r, k, v, kk, b, lw, bsz, t, n_seq):
    nc = t // CHUNK
    shape3 = (bsz, t, RWKV_WIDTH)
    spec = lambda: pl.BlockSpec((n_seq, CHUNK, RWKV_WIDTH), lambda bb, i: (bb, i, 0))
    y = pl.pallas_call(
        _rw_chunk_kernel,
        grid=(bsz // n_seq, nc),
        in_specs=[spec() for _ in range(6)],
        out_specs=spec(),
        out_shape=jax.ShapeDtypeStruct(shape3, F32),
        scratch_shapes=[pltpu.VMEM((HEAD_SIZE, HEAD_SIZE), F32)
                        for _ in range(n_seq * RWKV_HEADS)],
        compiler_params=_cparams(("parallel", "arbitrary")),
        name="rw_chunk",
    )(*[a.reshape(shape3) for a in (r, k, v, kk, b, lw)])
    return y.reshape(bsz * t, RWKV_WIDTH)


def _rw_post_kernel(y_ref, r_ref, k_ref, v_ref, g_ref, rk_ref, gg_ref, gb_ref, ones_ref, o_ref):
    y = y_ref[...]
    inv = 1.0 / HEAD_SIZE
    mu = _head_sum(y, ones_ref) * inv
    yc = y - mu
    var = _head_sum(yc * yc, ones_ref) * inv
    yn = yc * lax.rsqrt(var + GN_EPS) * gg_ref[...] + gb_ref[...]
    bonus = _head_sum(r_ref[...] * k_ref[...] * rk_ref[...], ones_ref) * v_ref[...]
    o_ref[...] = ((yn + bonus) * g_ref[...]).astype(o_ref.dtype)


def _rw_post(y, r, k, v, g, r_k, gn_g, gn_b, ones_bd, tt):
    n = y.shape[0]
    row = lambda: pl.BlockSpec((tt, RWKV_WIDTH), lambda i: (i, 0))
    vec = lambda: pl.BlockSpec((1, RWKV_WIDTH), lambda i: (0, 0))
    return pl.pallas_call(
        _rw_post_kernel,
        grid=(n // tt,),
        in_specs=[row(), row(), row(), row(), row(), vec(), vec(), vec(),
                  pl.BlockSpec((MXU_DIM, MXU_DIM), lambda i: (0, 0))],
        out_specs=row(),
        out_shape=jax.ShapeDtypeStruct((n, RWKV_WIDTH), BF16),
        compiler_params=_cparams(("parallel",)),
        name="rw_post",
    )(y, r, k, v, g, r_k, gn_g, gn_b, ones_bd)


def _out_proj_kernel(yl_ref, yr_ref, h0_ref, wo_ref, g_ref, b_ref, wr_ref, br_ref,
                     h1_ref, h1p_ref, ti_ref, tg_ref, rank_ref, cnt_ref, carry, *, tm):
    @pl.when(pl.program_id(0) == 0)
    def _():
        carry[...] = jnp.zeros_like(carry)

    mix = (jnp.dot(yl_ref[...], wo_ref[0:LRU_WIDTH, :], preferred_element_type=F32)
           + jnp.dot(yr_ref[...], wo_ref[LRU_WIDTH:, :], preferred_element_type=F32))
    h1 = _layer_norm_rows(DEEPNORM_ALPHA * h0_ref[...] + mix, g_ref[...], b_ref[...])
    h1_ref[...] = h1
    h1p_ref[...] = _pack_halves(h1)

    h_hi, h_lo = _split2(h1)
    w_hi = wr_ref[0]
    w_lo = wr_ref[1]
    logits = (jnp.dot(h_hi, w_hi, preferred_element_type=F32)
              + jnp.dot(h_hi, w_lo, preferred_element_type=F32)
              + jnp.dot(h_lo, w_hi, preferred_element_type=F32)) + br_ref[...]

    lane = lax.broadcasted_iota(jnp.int32, (tm, N_EXPERTS), 1).astype(F32)
    lane4 = lax.broadcasted_iota(jnp.int32, (tm, TOP_K), 1)
    work = logits
    vals, idxs, sels = [], [], []
    for _ in range(TOP_K):
        m = jnp.max(work, axis=-1, keepdims=True)
        idx = jnp.min(jnp.where(work == m, lane, float(N_EXPERTS)), axis=-1, keepdims=True)
        sel = lane == idx
        vals.append(m)
        idxs.append(idx)
        sels.append(sel)
        work = jnp.where(sel, -jnp.inf, work)
    exps = [jnp.exp(vv - vals[0]) for vv in vals]
    denom = exps[0] + exps[1] + exps[2] + exps[3]

    member = jnp.zeros((tm, N_EXPERTS), F32)
    for sel in sels:
        member = member + sel.astype(F32)
    ri = lax.broadcasted_iota(jnp.int32, (tm, tm), 0)
    ci = lax.broadcasted_iota(jnp.int32, (tm, tm), 1)
    before = (ci < ri).astype(BF16)
    rank_full = jnp.dot(before, member.astype(BF16), preferred_element_type=F32) + carry[...]
    carry[...] = carry[...] + jnp.sum(member, axis=0, keepdims=True)
    cnt_ref[...] = carry[...].astype(jnp.int32)

    ti = jnp.zeros((tm, TOP_K), jnp.int32)
    tg = jnp.zeros((tm, TOP_K), F32)
    rk = jnp.zeros((tm, TOP_K), jnp.int32)
    for kq in range(TOP_K):
        rank_k = jnp.sum(jnp.where(sels[kq], rank_full, 0.0), axis=-1, keepdims=True)
        ti = jnp.where(lane4 == kq, idxs[kq].astype(jnp.int32), ti)
        tg = jnp.where(lane4 == kq, exps[kq] / denom, tg)
        rk = jnp.where(lane4 == kq, rank_k.astype(jnp.int32), rk)
    ti_ref[...] = ti
    tg_ref[...] = tg
    rank_ref[...] = rk


def _out_proj(y_lru, y_rw, h0, w_out_bf16, g, b, w_router_split, b_router, tm):
    n = h0.shape[0]
    row = lambda w: pl.BlockSpec((tm, w), lambda i: (i, 0))
    vec = lambda w: pl.BlockSpec((1, w), lambda i: (0, 0))
    return pl.pallas_call(
        functools.partial(_out_proj_kernel, tm=tm),
        grid=(n // tm,),
        in_specs=[
            row(LRU_WIDTH), row(RWKV_WIDTH), row(D_MODEL),
            pl.BlockSpec((D_MODEL, D_MODEL), lambda i: (0, 0)),
            vec(D_MODEL), vec(D_MODEL),
            pl.BlockSpec((2, D_MODEL, N_EXPERTS), lambda i: (0, 0, 0)),
            vec(N_EXPERTS),
        ],
        out_specs=[row(D_MODEL), row(HALF), row(TOP_K), row(TOP_K), row(TOP_K), vec(N_EXPERTS)],
        out_shape=[
            jax.ShapeDtypeStruct((n, D_MODEL), F32),
            jax.ShapeDtypeStruct((n, HALF), U32),
            jax.ShapeDtypeStruct((n, TOP_K), jnp.int32),
            jax.ShapeDtypeStruct((n, TOP_K), F32),
            jax.ShapeDtypeStruct((n, TOP_K), jnp.int32),
            jax.ShapeDtypeStruct((1, N_EXPERTS), jnp.int32),
        ],
        scratch_shapes=[pltpu.VMEM((1, N_EXPERTS), F32)],
        compiler_params=_cparams(("arbitrary",)),
        name="out_proj",
    )(y_lru, y_rw, h0, w_out_bf16, g, b, w_router_split, b_router)


def _scatter_kernel(dest_ref, fill_ref, h_ref, xs_ref, zbuf, sem, zsem, *, tb, tm, nb):
    @pl.when(pl.program_id(0) == 0)
    def _():
        zbuf[...] = jnp.zeros_like(zbuf)

        def fill_copy(start):
            start = pl.multiple_of(start, SUBLANE)
            return pltpu.make_async_copy(zbuf, xs_ref.at[pl.ds(start, tm), :], zsem)

        def fill(e, carry):
            @pl.when(fill_ref[e] >= 0)
            def _():
                fill_copy(fill_ref[e]).start()
            return carry

        def fill_wait(e, carry):
            @pl.when(fill_ref[e] >= 0)
            def _():
                fill_copy(fill_ref[e]).wait()
            return carry

        def tail(blk, carry):
            fill_copy(blk * tm).start()
            return carry

        def tail_wait(blk, carry):
            fill_copy(blk * tm).wait()
            return carry

        n_used = fill_ref[N_EXPERTS]
        lax.fori_loop(0, N_EXPERTS, fill, 0)
        lax.fori_loop(n_used, nb, tail, 0)
        lax.fori_loop(0, N_EXPERTS, fill_wait, 0)
        lax.fori_loop(n_used, nb, tail_wait, 0)

    base = pl.program_id(0) * (tb * TOP_K)

    def copy(t, kq):
        slot = dest_ref[base + t * TOP_K + kq]
        return pltpu.make_async_copy(h_ref.at[pl.ds(t, 1), :], xs_ref.at[pl.ds(slot, 1), :], sem)

    def issue(t, carry):
        for kq in range(TOP_K):
            copy(t, kq).start()
        return carry

    lax.fori_loop(0, tb, issue, 0, unroll=4)
    for kq in range(TOP_K):
        pltpu.make_async_copy(h_ref, xs_ref.at[pl.ds(0, tb), :], sem).wait()


def _scatter(dest_flat, fill_start, h1p, n_slots, tb, tm):
    n = h1p.shape[0]
    nb = n_slots // tm
    grid_spec = pltpu.PrefetchScalarGridSpec(
        num_scalar_prefetch=2,
        grid=(n // tb,),
        in_specs=[pl.BlockSpec((tb, HALF), lambda i, d, f: (i, 0))],
        out_specs=pl.BlockSpec(memory_space=pl.ANY),
        scratch_shapes=[pltpu.VMEM((tm, HALF), U32), pltpu.SemaphoreType.DMA(()),
                        pltpu.SemaphoreType.DMA(())],
    )
    return pl.pallas_call(
        functools.partial(_scatter_kernel, tb=tb, tm=tm, nb=nb),
        grid_spec=grid_spec,
        out_shape=jax.ShapeDtypeStruct((n_slots, HALF), U32),
        compiler_params=_cparams(("arbitrary",)),
        name="scatter",
    )(dest_flat, fill_start, h1p)


def _gmm1_kernel(meta_ref, x_ref, wg_ref, wu_ref, bg_ref, bu_ref, o_ref, wgb, wub, *, nb):
    i = pl.program_id(1)
    e = meta_ref[i]
    prev_e = meta_ref[jnp.maximum(i - 1, 0)]

    @pl.when(jnp.logical_or(i == 0, e != prev_e))
    def _():
        wgb[...] = wg_ref[0].astype(BF16)
        wub[...] = wu_ref[0].astype(BF16)

    @pl.when(i < meta_ref[nb])
    def _():
        x_lo, x_hi = _unpack_halves(x_ref[...])
        x_lo = x_lo.astype(BF16)
        x_hi = x_hi.astype(BF16)
        gate = (jnp.dot(x_lo, wgb[0:HALF, :], preferred_element_type=F32)
                + jnp.dot(x_hi, wgb[HALF:, :], preferred_element_type=F32) + bg_ref[0])
        up = (jnp.dot(x_lo, wub[0:HALF, :], preferred_element_type=F32)
              + jnp.dot(x_hi, wub[HALF:, :], preferred_element_type=F32) + bu_ref[0])
        gate = jnp.minimum(gate, SWIGLU_LIMIT)
        up = jnp.clip(up, -SWIGLU_LIMIT, SWIGLU_LIMIT)
        act = gate * _sigmoid(SWIGLU_ALPHA * gate) * (up + 1.0)
        o_ref[...] = act.astype(o_ref.dtype)

    @pl.when(i >= meta_ref[nb])
    def _():
        o_ref[...] = jnp.zeros_like(o_ref)


def _gmm1(meta, xs, w1, b1, nb, tm, tn):
    n_slots = xs.shape[0]
    nj = D_EXPERT // tn

    def row_blk(j, i, m):
        return jnp.minimum(i, m[nb] - 1)

    grid_spec = pltpu.PrefetchScalarGridSpec(
        num_scalar_prefetch=1,
        grid=(nj, nb),
        in_specs=[
            pl.BlockSpec((tm, HALF), lambda j, i, m: (row_blk(j, i, m), 0)),
            pl.BlockSpec((1, D_MODEL, tn), lambda j, i, m: (m[i], 0, j)),
            pl.BlockSpec((1, D_MODEL, tn), lambda j, i, m: (m[i], 0, j + nj)),
            pl.BlockSpec((1, 1, tn), lambda j, i, m: (m[i], 0, j)),
            pl.BlockSpec((1, 1, tn), lambda j, i, m: (m[i], 0, j + nj)),
        ],
        out_specs=pl.BlockSpec((tm, tn), lambda j, i, m: (i, j)),
        scratch_shapes=[pltpu.VMEM((D_MODEL, tn), BF16), pltpu.VMEM((D_MODEL, tn), BF16)],
    )
    return pl.pallas_call(
        functools.partial(_gmm1_kernel, nb=nb),
        grid_spec=grid_spec,
        out_shape=jax.ShapeDtypeStruct((n_slots, D_EXPERT), BF16),
        compiler_params=_cparams(("arbitrary", "arbitrary")),
        name="gmm1",
    )(meta, xs, w1, w1, b1, b1)


def _gmm2_kernel(meta_ref, a_ref, w_ref, b_ref, o_ref, wb, *, nb):
    i = pl.program_id(1)
    e = meta_ref[i]
    prev_e = meta_ref[jnp.maximum(i - 1, 0)]

    @pl.when(jnp.logical_or(i == 0, e != prev_e))
    def _():
        wb[...] = w_ref[0].astype(BF16)

    @pl.when(i < meta_ref[nb])
    def _():
        out = jnp.dot(a_ref[...], wb[...], preferred_element_type=F32) + b_ref[0]
        o_ref[...] = _pack_halves(out)

    @pl.when(i >= meta_ref[nb])
    def _():
        o_ref[...] = jnp.zeros_like(o_ref)


def _gmm2(meta, act, w2, b2, nb, tm):
    n_slots = act.shape[0]

    def row_blk(j, i, m):
        return jnp.minimum(i, m[nb] - 1)

    grid_spec = pltpu.PrefetchScalarGridSpec(
        num_scalar_prefetch=1,
        grid=(1, nb),
        in_specs=[
            pl.BlockSpec((tm, D_EXPERT), lambda j, i, m: (row_blk(j, i, m), 0)),
            pl.BlockSpec((1, D_EXPERT, D_MODEL), lambda j, i, m: (m[i], 0, 0)),
            pl.BlockSpec((1, 1, D_MODEL), lambda j, i, m: (m[i], 0, 0)),
        ],
        out_specs=pl.BlockSpec((tm, HALF), lambda j, i, m: (i, 0)),
        scratch_shapes=[pltpu.VMEM((D_EXPERT, D_MODEL), BF16)],
    )
    return pl.pallas_call(
        functools.partial(_gmm2_kernel, nb=nb),
        grid_spec=grid_spec,
        out_shape=jax.ShapeDtypeStruct((n_slots, HALF), U32),
        compiler_params=_cparams(("arbitrary", "arbitrary")),
        name="gmm2",
    )(meta, act, w2, b2)


def _combine_kernel(dest_ref, h1_ref, tg_ref, g_ref, b_ref, ys_ref, o_ref, buf, sems, *, tb, n_steps):
    i = pl.program_id(0)

    def issue(step, half):
        base = step * (tb * TOP_K)

        def body(t, carry):
            for kq in range(TOP_K):
                slot = dest_ref[base + t * TOP_K + kq]
                pltpu.make_async_copy(ys_ref.at[pl.ds(slot, 1), :],
                                      buf.at[half, kq, pl.ds(t, 1), :], sems.at[half]).start()
            return carry

        lax.fori_loop(0, tb, body, 0, unroll=4)

    @pl.when(i == 0)
    def _():
        issue(0, 0)

    @pl.when(i + 1 < n_steps)
    def _():
        issue(i + 1, (i + 1) % 2)

    half = i % 2
    for kq in range(TOP_K):
        pltpu.make_async_copy(ys_ref.at[pl.ds(0, tb), :], buf.at[half, kq], sems.at[half]).wait()

    tg = tg_ref[...]
    y = DEEPNORM_ALPHA * h1_ref[...]
    for kq in range(TOP_K):
        lo, hi = _unpack_halves(buf[half, kq])
        y = y + tg[:, kq:kq + 1] * jnp.concatenate([lo, hi], axis=1)
    o_ref[...] = _layer_norm_rows(y, g_ref[...], b_ref[...])


def _combine(dest_flat, h1, tg, g, b, ys, tb):
    n = h1.shape[0]
    grid_spec = pltpu.PrefetchScalarGridSpec(
        num_scalar_prefetch=1,
        grid=(n // tb,),
        in_specs=[
            pl.BlockSpec((tb, D_MODEL), lambda i, d: (i, 0)),
            pl.BlockSpec((tb, TOP_K), lambda i, d: (i, 0)),
            pl.BlockSpec((1, D_MODEL), lambda i, d: (0, 0)),
            pl.BlockSpec((1, D_MODEL), lambda i, d: (0, 0)),
            pl.BlockSpec(memory_space=pl.ANY),
        ],
        out_specs=pl.BlockSpec((tb, D_MODEL), lambda i, d: (i, 0)),
        scratch_shapes=[pltpu.VMEM((2, TOP_K, tb, HALF), U32), pltpu.SemaphoreType.DMA((2,))],
    )
    return pl.pallas_call(
        functools.partial(_combine_kernel, tb=tb, n_steps=n // tb),
        grid_spec=grid_spec,
        out_shape=jax.ShapeDtypeStruct((n, D_MODEL), F32),
        compiler_params=_cparams(("arbitrary",)),
        name="combine",
    )(dest_flat, h1, tg, g, b, ys)


def _block_diag(w, group):
    nb = w.shape[0] // group
    w = w.reshape(nb, group, LRU_BLOCK_W, LRU_BLOCK_W)
    eye = jnp.eye(group, dtype=w.dtype)
    out = jnp.einsum('gajk,ab->gajbk', w, eye)
    return out.reshape(nb, group * LRU_BLOCK_W, group * LRU_BLOCK_W)


def _pad_rows(w, rows):
    return jnp.pad(w, ((0, rows - w.shape[0]), (0, 0)))


def _layer(h_in_x, l, prm, bsz, t):
    n = bsz * t
    row = lambda a: a.reshape(1, -1)

    w_in = prm['w_in'][l]
    o_rw = 2 * LRU_WIDTH
    o_l = o_rw + 3 * RWKV_WIDTH
    pad_c = lambda w: jnp.pad(w, ((0, 0), (0, LORA_PAD - w.shape[1])))
    w_in_p = jnp.concatenate([
        w_in[:, :o_l],
        pad_c(w_in[:, o_l:o_l + DECAY_LORA]),
        pad_c(w_in[:, o_l + DECAY_LORA:o_l + DECAY_LORA + AAA_LORA]),
        w_in[:, o_l + DECAY_LORA + AAA_LORA:],
    ], axis=1).astype(BF16)
    mu = prm['shift_mu'][l]
    pad_v = lambda v: jnp.pad(v, (0, LORA_PAD - v.shape[0]))
    mu_l = mu[3 * RWKV_WIDTH:]
    mu_a = jnp.concatenate([pad_v(mu_l[:DECAY_LORA]), pad_v(mu_l[DECAY_LORA:DECAY_LORA + AAA_LORA]),
                            mu_l[DECAY_LORA + AAA_LORA:]])

    tm_in = min(1024, n)
    h0, p = _in_proj(h_in_x, row(prm['ln_in_g']), row(prm['ln_in_b']), w_in_p, tm_in, 512)

    tt = min(512, t)
    group = MXU_DIM // LRU_BLOCK_W
    y_lru = _lru(p, prm['conv_w'][l], row(prm['conv_b'][l]),
                 _block_diag(prm['w_rgate'][l], group).astype(BF16), row(prm['b_rgate'][l]),
                 _block_diag(prm['w_igate'][l], group).astype(BF16), row(prm['b_igate'][l]),
                 row(prm['lru_lambda'][l]), bsz, t, tt)

    head_id = jnp.arange(MXU_DIM) // HEAD_SIZE
    ones_bd = (head_id[:, None] == head_id[None, :]).astype(BF16)
    tt_rw = min(256, t)
    r, k, v, kk, bvec, lw, g = _rw_prep(
        p, row(mu[:RWKV_WIDTH]), row(mu[RWKV_WIDTH:2 * RWKV_WIDTH]),
        row(mu[2 * RWKV_WIDTH:3 * RWKV_WIDTH]), row(mu_a),
        row(prm['w0'][l]), _pad_rows(prm['rw_decay_up'][l], LORA_PAD).astype(BF16),
        row(prm['a0'][l]), _pad_rows(prm['rw_aaa_up'][l], LORA_PAD).astype(BF16),
        prm['rw_gate_up'][l].astype(BF16), row(prm['k_k'][l]), row(prm['k_a'][l]),
        ones_bd, bsz, t, tt_rw)
    y = _rw_chunk(r, k, v, kk, bvec, lw, bsz, t, 4 if bsz % 4 == 0 else 1)
    y_rw = _rw_post(y, r, k, v, g, row(prm['r_k'][l]), row(prm['gn_g'][l]), row(prm['gn_b'][l]),
                    ones_bd, min(512, n))

    w_r = prm['w_router'][l]
    w_r_hi = w_r.astype(BF16)
    w_r_lo = (w_r - w_r_hi.astype(F32)).astype(BF16)
    tm_out = min(512, n)
    h1, h1p, top_i, top_g, rank, counts = _out_proj(
        y_lru, y_rw, h0, prm['w_out'][l].astype(BF16), row(prm['ln1_g'][l]), row(prm['ln1_b'][l]),
        jnp.stack([w_r_hi, w_r_lo]), row(prm['b_router'][l]), tm_out)

    tm = 512
    counts = counts.reshape(N_EXPERTS)
    padded = ((counts + tm - 1) // tm) * tm
    pad_end = jnp.cumsum(padded)
    pad_start = pad_end - padded
    dest = (pad_start[top_i] + rank).reshape(-1).astype(jnp.int32)
    nb = (n * TOP_K) // tm + N_EXPERTS
    n_slots = nb * tm
    n_used = (pad_end[-1] // tm).astype(jnp.int32)
    blk = jnp.minimum(jnp.arange(nb, dtype=jnp.int32), n_used - 1)
    blk_e = jnp.sum((pad_end[None, :] <= (blk * tm)[:, None]).astype(jnp.int32), axis=1)
    blk_e = jnp.minimum(blk_e, N_EXPERTS - 1)
    meta = jnp.concatenate([blk_e.astype(jnp.int32), n_used.reshape(1)])

    fill_start = jnp.concatenate([jnp.where(padded > 0, pad_end - tm, -1).astype(jnp.int32),
                                  n_used.reshape(1)])
    tb = min(128, n)
    xs = _scatter(dest, fill_start, h1p, n_slots, tb, tm)
    act = _gmm1(meta, xs, prm['w_exp1'][l], prm['b_exp1'][l].reshape(N_EXPERTS, 1, 2 * D_EXPERT),
                nb, tm, 1024)
    ys = _gmm2(meta, act, prm['w_exp2'][l], prm['b_exp2'][l].reshape(N_EXPERTS, 1, D_MODEL),
               nb, tm)
    return _combine(dest, h1, top_g, row(prm['ln2_g'][l]), row(prm['ln2_b'][l]),
                    ys, tb)


def kernel(x, ln_in_g, ln_in_b, w_in, conv_w, conv_b, w_rgate, b_rgate, w_igate, b_igate, lru_lambda, shift_mu, w0, rw_decay_up, a0, rw_aaa_up, rw_gate_up, k_k, k_a, r_k, gn_g, gn_b, w_out, ln1_g, ln1_b, w_router, b_router, w_exp1, b_exp1, w_exp2, b_exp2, ln2_g, ln2_b):
    bsz, t, d = x.shape
    prm = dict(ln_in_g=ln_in_g, ln_in_b=ln_in_b, w_in=w_in, conv_w=conv_w, conv_b=conv_b,
               w_rgate=w_rgate, b_rgate=b_rgate, w_igate=w_igate, b_igate=b_igate,
               lru_lambda=lru_lambda, shift_mu=shift_mu, w0=w0, rw_decay_up=rw_decay_up, a0=a0,
               rw_aaa_up=rw_aaa_up, rw_gate_up=rw_gate_up, k_k=k_k, k_a=k_a, r_k=r_k, gn_g=gn_g,
               gn_b=gn_b, w_out=w_out, ln1_g=ln1_g, ln1_b=ln1_b, w_router=w_router,
               b_router=b_router, w_exp1=w_exp1, b_exp1=b_exp1, w_exp2=w_exp2, b_exp2=b_exp2,
               ln2_g=ln2_g, ln2_b=ln2_b)
    out = _layer(x.reshape(bsz * t, d), 0, prm, bsz, t)
    return out.reshape(bsz, t, d)
```

```python
import functools

import jax
import jax.numpy as jnp
from jax import lax
from jax.experimental import pallas as pl
from jax.experimental.pallas import tpu as pltpu

D_MODEL = 2048
DEPTH = 1
CHUNK = 64
LRU_WIDTH = 1024
LRU_BLOCKS = 16
LRU_BLOCK_W = LRU_WIDTH // LRU_BLOCKS
CONV_WIDTH = 4
RG_C = 8.0
RWKV_WIDTH = D_MODEL - LRU_WIDTH
HEAD_SIZE = 64
RWKV_HEADS = RWKV_WIDTH // HEAD_SIZE
DECAY_LORA = 96
AAA_LORA = 96
GATE_LORA = 256
N_EXPERTS = 32
TOP_K = 4
D_EXPERT = D_MODEL
SWIGLU_LIMIT = 7.0
SWIGLU_ALPHA = 1.702
LN_EPS = 1e-5
GN_EPS = HEAD_SIZE * 1e-5
DEEPNORM_ALPHA = (2.0 * DEPTH) ** 0.25

LANE = 128
SUBLANE = 8
MXU_DIM = 256
LORA_PAD = 128
LORA_W = 2 * LORA_PAD + GATE_LORA
P_WIDTH = 2 * LRU_WIDTH + 3 * RWKV_WIDTH + LORA_W
VMEM_LIMIT = 56 * 1024 * 1024

F32 = jnp.float32
BF16 = jnp.bfloat16


def _cparams(sem):
    return pltpu.CompilerParams(dimension_semantics=sem, vmem_limit_bytes=VMEM_LIMIT)


def _layer_norm_rows(x, g, b):
    mu = jnp.mean(x, axis=-1, keepdims=True)
    xc = x - mu
    var = jnp.mean(xc * xc, axis=-1, keepdims=True)
    return xc * lax.rsqrt(var + LN_EPS) * g + b


def _softplus(z):
    return jnp.maximum(z, 0.0) + jnp.log1p(jnp.exp(-jnp.abs(z)))


def _sigmoid(z):
    return 1.0 / (1.0 + jnp.exp(-z))


def _split2(x):
    hi = x.astype(BF16)
    lo = (x - hi.astype(F32)).astype(BF16)
    return hi, lo


HALF = D_MODEL // 2
U32 = jnp.uint32


def _pack_halves(x):
    lo = lax.bitcast_convert_type(x[:, :HALF].astype(BF16).astype(F32), U32) >> 16
    hi = lax.bitcast_convert_type(x[:, HALF:].astype(BF16).astype(F32), U32) & jnp.uint32(0xFFFF0000)
    return hi | lo


def _unpack_halves(w):
    lo = lax.bitcast_convert_type(w << 16, F32)
    hi = lax.bitcast_convert_type(w & jnp.uint32(0xFFFF0000), F32)
    return lo, hi


ROW_TILES = HALF // LANE
assert ROW_TILES == SUBLANE


def _store_rows(ref, row0, packed):
    m = packed.shape[0]
    for c in range(ROW_TILES):
        ref[pl.ds(row0 * ROW_TILES + c, m, stride=ROW_TILES), :] = packed[:, c * LANE:(c + 1) * LANE]


def _load_rows(ref, row0, m):
    return jnp.concatenate(
        [ref[pl.ds(row0 * ROW_TILES + c, m, stride=ROW_TILES), :] for c in range(ROW_TILES)], axis=1)


def _row_tile(ref, row):
    return ref.at[pl.ds(pl.multiple_of(row * ROW_TILES, ROW_TILES), ROW_TILES), :]


def _head_sum(x, ones_ref):
    hi, lo = _split2(x)
    ones = ones_ref[...]
    parts = []
    for g in range(x.shape[1] // MXU_DIM):
        sl = slice(g * MXU_DIM, (g + 1) * MXU_DIM)
        parts.append(jnp.dot(hi[:, sl], ones, preferred_element_type=F32)
                     + jnp.dot(lo[:, sl], ones, preferred_element_type=F32))
    return jnp.concatenate(parts, axis=1)


def _in_proj_kernel(x_ref, g_ref, b_ref, w_ref, h_ref, p_ref, hb_ref):
    @pl.when(pl.program_id(1) == 0)
    def _():
        h = _layer_norm_rows(x_ref[...], g_ref[...], b_ref[...])
        h_ref[...] = h
        hb_ref[...] = h.astype(BF16)

    p_ref[...] = jnp.dot(hb_ref[...], w_ref[...], preferred_element_type=F32)


def _in_proj(x2, g, b, w_bf16, tm, tn):
    n = x2.shape[0]
    return pl.pallas_call(
        _in_proj_kernel,
        grid=(n // tm, P_WIDTH // tn),
        in_specs=[
            pl.BlockSpec((tm, D_MODEL), lambda i, j: (i, 0)),
            pl.BlockSpec((1, D_MODEL), lambda i, j: (0, 0)),
            pl.BlockSpec((1, D_MODEL), lambda i, j: (0, 0)),
            pl.BlockSpec((D_MODEL, tn), lambda i, j: (0, j)),
        ],
        out_specs=[
            pl.BlockSpec((tm, D_MODEL), lambda i, j: (i, 0)),
            pl.BlockSpec((tm, tn), lambda i, j: (i, j)),
        ],
        out_shape=[
            jax.ShapeDtypeStruct((n, D_MODEL), F32),
            jax.ShapeDtypeStruct((n, P_WIDTH), F32),
        ],
        scratch_shapes=[pltpu.VMEM((tm, D_MODEL), BF16)],
        compiler_params=_cparams(("parallel", "arbitrary")),
        name="in_proj",
    )(x2, g, b, w_bf16)


def _lru_kernel(u_ref, gi_ref, cw_ref, cb_ref, wr_ref, br_ref, wi_ref, bi_ref,
                lam_ref, o_ref, ubuf, a_s, b_s, carry, *, tt):
    first = pl.program_id(1) == 0

    @pl.when(first)
    def _():
        ubuf[0:SUBLANE, :] = jnp.zeros((SUBLANE, LRU_WIDTH), F32)
        carry[...] = jnp.zeros_like(carry)

    @pl.when(jnp.logical_not(first))
    def _():
        ubuf[0:SUBLANE, :] = ubuf[tt:tt + SUBLANE, :]

    ubuf[SUBLANE:, :] = u_ref[...]
    uc = cb_ref[...]
    for i in range(CONV_WIDTH):
        off = SUBLANE - (CONV_WIDTH - 1) + i
        uc = uc + cw_ref[i:i + 1, :] * ubuf[off:off + tt, :]

    ucb = uc.astype(BF16)
    n_grp = LRU_WIDTH // MXU_DIM

    def gate(w_ref, bias_ref):
        parts = [jnp.dot(ucb[:, g * MXU_DIM:(g + 1) * MXU_DIM], w_ref[g],
                         preferred_element_type=F32) for g in range(n_grp)]
        return _sigmoid(jnp.concatenate(parts, axis=1) + bias_ref[...])

    r_gate = gate(wr_ref, br_ref)
    i_gate = gate(wi_ref, bi_ref)
    log_a = (-RG_C * r_gate) * _softplus(-lam_ref[...])
    a_s[...] = jnp.exp(log_a)
    th = jnp.tanh(log_a)
    b_s[...] = jnp.sqrt(-2.0 * th / (1.0 - th)) * (i_gate * uc)

    row = lax.broadcasted_iota(jnp.int32, (SUBLANE, LRU_WIDTH), 0)

    def group(gidx, c):
        off = pl.multiple_of(gidx * SUBLANE, SUBLANE)
        a = a_s[pl.ds(off, SUBLANE), :]
        b = b_s[pl.ds(off, SUBLANE), :]
        for d in (1, 2, 4):
            keep = row >= d
            a_sh = pltpu.roll(a, d, axis=0)
            b_sh = pltpu.roll(b, d, axis=0)
            b = jnp.where(keep, a * b_sh + b, b)
            a = jnp.where(keep, a * a_sh, a)
        h = a * c + b
        b_s[pl.ds(off, SUBLANE), :] = h
        return h[SUBLANE - 1:SUBLANE, :]

    carry[...] = lax.fori_loop(0, tt // SUBLANE, group, carry[...])
    o_ref[...] = (b_s[...] * jax.nn.gelu(gi_ref[...])).astype(o_ref.dtype)


def _lru(p, conv_w, conv_b, wr_bd, br, wi_bd, bi, lam, bsz, t, tt):
    n = bsz * t
    nt = t // tt
    vec = lambda: pl.BlockSpec((1, LRU_WIDTH), lambda b, i: (0, 0))
    wspec = lambda: pl.BlockSpec((LRU_WIDTH // MXU_DIM, MXU_DIM, MXU_DIM), lambda b, i: (0, 0, 0))
    return pl.pallas_call(
        functools.partial(_lru_kernel, tt=tt),
        grid=(bsz, nt),
        in_specs=[
            pl.BlockSpec((tt, LRU_WIDTH), lambda b, i: (b * nt + i, 0)),
            pl.BlockSpec((tt, LRU_WIDTH), lambda b, i: (b * nt + i, 1)),
            pl.BlockSpec((CONV_WIDTH, LRU_WIDTH), lambda b, i: (0, 0)),
            vec(), wspec(), vec(), wspec(), vec(), vec(),
        ],
        out_specs=pl.BlockSpec((tt, LRU_WIDTH), lambda b, i: (b * nt + i, 0)),
        out_shape=jax.ShapeDtypeStruct((n, LRU_WIDTH), BF16),
        scratch_shapes=[
            pltpu.VMEM((tt + SUBLANE, LRU_WIDTH), F32),
            pltpu.VMEM((tt, LRU_WIDTH), F32),
            pltpu.VMEM((tt, LRU_WIDTH), F32),
            pltpu.VMEM((1, LRU_WIDTH), F32),
        ],
        compiler_params=_cparams(("parallel", "arbitrary")),
        name="lru",
    )(p, p, conv_w, conv_b, wr_bd, br, wi_bd, bi, lam)


def _rw_prep_kernel(pr_ref, pk_ref, pv_ref, pa_ref, mur_ref, muk_ref, muv_ref, mua_ref,
                    w0_ref, wdec_ref, a0_ref, waaa_ref, wgate_ref, kk_ref, ka_ref, ones_ref,
                    r_out, k_out, v_out, kkn_out, b_out, lw_out, g_out,
                    prev_r, prev_k, prev_v, prev_a, *, tt):
    first = pl.program_id(1) == 0

    @pl.when(first)
    def _():
        for ref in (prev_r, prev_k, prev_v, prev_a):
            ref[...] = jnp.zeros_like(ref)

    def shift(x_ref, prev_ref, mu_ref):
        x = x_ref[...]
        row = lax.broadcasted_iota(jnp.int32, x.shape, 0)
        prev = jnp.where(row == 0, prev_ref[...], pltpu.roll(x, 1, axis=0))
        prev_ref[...] = x[tt - 1:tt, :]
        return x + (prev - x) * mu_ref[...]

    r = shift(pr_ref, prev_r, mur_ref)
    k = shift(pk_ref, prev_k, muk_ref)
    v = shift(pv_ref, prev_v, muv_ref)
    ad = shift(pa_ref, prev_a, mua_ref)

    wd = jnp.tanh(ad[:, 0:LORA_PAD]).astype(BF16)
    aa = ad[:, LORA_PAD:2 * LORA_PAD].astype(BF16)
    gd = _sigmoid(ad[:, 2 * LORA_PAD:]).astype(BF16)
    w_pre = w0_ref[...] + jnp.dot(wd, wdec_ref[...], preferred_element_type=F32)
    w_log = -_softplus(-w_pre) - 0.5
    a = _sigmoid(a0_ref[...] + jnp.dot(aa, waaa_ref[...], preferred_element_type=F32))
    g = jnp.dot(gd, wgate_ref[...], preferred_element_type=F32)

    kk = k * kk_ref[...]
    norm = jnp.sqrt(_head_sum(kk * kk, ones_ref))
    kk = kk / jnp.maximum(norm, 1e-12)

    r_out[...] = r
    k_out[...] = k * (1.0 + (a - 1.0) * ka_ref[...])
    v_out[...] = v
    kkn_out[...] = kk
    b_out[...] = kk * a
    lw_out[...] = -jnp.exp(w_log)
    g_out[...] = g


def _rw_prep(p, mu_r, mu_k, mu_v, mu_a, w0, wdec, a0, waaa, wgate, k_k, k_a, ones_bd, bsz, t, tt):
    n = bsz * t
    nt = t // tt
    cb = 2 * LRU_WIDTH // RWKV_WIDTH
    row = lambda c: pl.BlockSpec((tt, RWKV_WIDTH), lambda b, i: (b * nt + i, c))
    vec = lambda w: pl.BlockSpec((1, w), lambda b, i: (0, 0))
    full = lambda s: pl.BlockSpec(s, lambda b, i: (0, 0))
    lora_cb = (2 * LRU_WIDTH + 3 * RWKV_WIDTH) // LORA_W
    out = jax.ShapeDtypeStruct((n, RWKV_WIDTH), F32)
    return pl.pallas_call(
        functools.partial(_rw_prep_kernel, tt=tt),
        grid=(bsz, nt),
        in_specs=[
            row(cb), row(cb + 1), row(cb + 2),
            pl.BlockSpec((tt, LORA_W), lambda b, i: (b * nt + i, lora_cb)),
            vec(RWKV_WIDTH), vec(RWKV_WIDTH), vec(RWKV_WIDTH), vec(LORA_W),
            vec(RWKV_WIDTH), full((LORA_PAD, RWKV_WIDTH)),
            vec(RWKV_WIDTH), full((LORA_PAD, RWKV_WIDTH)),
            full((GATE_LORA, RWKV_WIDTH)),
            vec(RWKV_WIDTH), vec(RWKV_WIDTH),
            full((MXU_DIM, MXU_DIM)),
        ],
        out_specs=[pl.BlockSpec((tt, RWKV_WIDTH), lambda b, i: (b * nt + i, 0))] * 7,
        out_shape=[out] * 7,
        scratch_shapes=[pltpu.VMEM((1, RWKV_WIDTH), F32)] * 3 + [pltpu.VMEM((1, LORA_W), F32)],
        compiler_params=_cparams(("parallel", "arbitrary")),
        name="rw_prep",
    )(p, p, p, p, mu_r, mu_k, mu_v, mu_a, w0, wdec, a0, waaa, wgate, k_k, k_a, ones_bd)


def _rw_chunk_kernel(r_ref, k_ref, v_ref, kk_ref, b_ref, lw_ref, y_ref, *s_refs):
    c = CHUNK
    hs = HEAD_SIZE
    n_seq = r_ref.shape[0]

    @pl.when(pl.program_id(1) == 0)
    def _():
        for s_ref in s_refs:
            s_ref[...] = jnp.zeros_like(s_ref)

    ri = lax.broadcasted_iota(jnp.int32, (c, c), 0)
    ci = lax.broadcasted_iota(jnp.int32, (c, c), 1)
    tri = (ri >= ci).astype(BF16)

    r2 = lax.broadcasted_iota(jnp.int32, (2 * c, 2 * c), 0)
    c2 = lax.broadcasted_iota(jnp.int32, (2 * c, 2 * c), 1)
    tq = jnp.where(r2 >= c, r2 - c, r2)
    tk = jnp.where(c2 >= c, c2 - c, c2)
    mask = tk < tq + jnp.where(r2 >= c, 1, 0)
    zeros_cc = jnp.zeros((c, hs), BF16)

    dn_t = (((1,), (1,)), ((), ()))
    dn_l = (((0,), (0,)), ((), ()))

    def scaled_operands(q):
        lw = lw_ref[q]
        lw_hi = lw.astype(BF16)
        rem = lw - lw_hi.astype(F32)
        lw_mid = rem.astype(BF16)
        lw_lo = (rem - lw_mid.astype(F32)).astype(BF16)
        cl = (jnp.dot(tri, lw_hi, preferred_element_type=F32)
              + jnp.dot(tri, lw_mid, preferred_element_type=F32)
              + jnp.dot(tri, lw_lo, preferred_element_type=F32))
        cl_last = cl[c - 1:c, :]
        g_inv = jnp.exp(-cl)
        g_dec = jnp.exp(cl_last - cl)
        bv = b_ref[q]
        kx = k_ref[q]
        rg = r_ref[q] * jnp.exp(cl)
        return dict(
            g_last=jnp.exp(cl_last),
            v=v_ref[q].astype(BF16),
            rg=rg,
            lhs_a=(-kk_ref[q] * jnp.exp(cl - lw)).astype(BF16),
            lhs_r=rg.astype(BF16),
            rhs_b=(bv * g_inv).astype(BF16),
            rhs_k=(kx * g_inv).astype(BF16),
            dec_b=(bv * g_dec).astype(BF16),
            dec_k=(kx * g_dec).astype(BF16),
        )

    ops = [scaled_operands(q) for q in range(n_seq)]

    chains = [(q, h) for q in range(n_seq) for h in range(RWKV_HEADS)]
    sl_of = lambda h: slice(h * hs, (h + 1) * hs)
    s0s = [s_refs[q * RWKV_HEADS + h][...] for q, h in chains]
    aas = []
    for q, h in chains:
        o, sl = ops[q], sl_of(h)
        lhs = jnp.concatenate([o['lhs_a'][:, sl], o['lhs_r'][:, sl]], axis=0)
        rhs = jnp.concatenate([o['rhs_b'][:, sl], o['rhs_k'][:, sl]], axis=0)
        aa = lax.dot_general(lhs, rhs, dn_t, preferred_element_type=F32)
        aas.append(jnp.where(mask, aa, 0.0))
    v_hs = [ops[q]['v'][:, sl_of(h)] for q, h in chains]
    a_rs = [aa[c:, :].astype(BF16) for aa in aas]
    ps = [aa[:c, :c] for aa in aas]
    xs = []
    for i, (q, h) in enumerate(chains):
        akv = jnp.dot(aas[i][:c, c:].astype(BF16), v_hs[i], preferred_element_type=F32)
        xs.append(jnp.concatenate([ops[q]['lhs_a'][:, sl_of(h)].astype(F32), akv], axis=1))
    n_sq = 6
    for it in range(n_sq):
        pbs = [p.astype(BF16) for p in ps]
        xs = [x + jnp.dot(pb, x.astype(BF16), preferred_element_type=F32)
              for x, pb in zip(xs, pbs)]
        if it + 1 < n_sq:
            ps = [jnp.dot(pb, pb, preferred_element_type=F32) for pb in pbs]
    for i, (q, h) in enumerate(chains):
        o, sl = ops[q], sl_of(h)
        rhs2 = jnp.concatenate(
            [xs[i].astype(BF16), jnp.concatenate([zeros_cc, v_hs[i]], axis=1)], axis=0)
        qy = jnp.dot(a_rs[i], rhs2, preferred_element_type=F32)
        dec = jnp.concatenate([o['dec_b'][:, sl], o['dec_k'][:, sl]], axis=0)
        mnt = lax.dot_general(rhs2, dec, dn_l, preferred_element_type=F32)
        s0b = s0s[i].astype(BF16)
        qq = (o['rg'][:, sl] + qy[:, :hs]).astype(BF16)
        y_ref[q, :, sl] = lax.dot_general(qq, s0b, dn_t, preferred_element_type=F32) + qy[:, hs:]
        s_refs[i][...] = (s0s[i] * o['g_last'][:, sl]
                          + jnp.dot(s0b, mnt[:hs, :].astype(BF16), preferred_element_type=F32)
                          + mnt[hs:, :])


def _rw_chunk(r, k, v, kk, b, lw, bsz, t, n_seq):
    nc = t // CHUNK
    shape3 = (bsz, t, RWKV_WIDTH)
    spec = lambda: pl.BlockSpec((n_seq, CHUNK, RWKV_WIDTH), lambda bb, i: (bb, i, 0))
    y = pl.pallas_call(
        _rw_chunk_kernel,
        grid=(bsz // n_seq, nc),
        in_specs=[spec() for _ in range(6)],
        out_specs=spec(),
        out_shape=jax.ShapeDtypeStruct(shape3, F32),
        scratch_shapes=[pltpu.VMEM((HEAD_SIZE, HEAD_SIZE), F32)
                        for _ in range(n_seq * RWKV_HEADS)],
        compiler_params=_cparams(("parallel", "arbitrary")),
        name="rw_chunk",
    )(*[a.reshape(shape3) for a in (r, k, v, kk, b, lw)])
    return y.reshape(bsz * t, RWKV_WIDTH)


def _rw_post_kernel(y_ref, r_ref, k_ref, v_ref, g_ref, rk_ref, gg_ref, gb_ref, ones_ref, o_ref):
    y = y_ref[...]
    inv = 1.0 / HEAD_SIZE
    mu = _head_sum(y, ones_ref) * inv
    yc = y - mu
    var = _head_sum(yc * yc, ones_ref) * inv
    yn = yc * lax.rsqrt(var + GN_EPS) * gg_ref[...] + gb_ref[...]
    bonus = _head_sum(r_ref[...] * k_ref[...] * rk_ref[...], ones_ref) * v_ref[...]
    o_ref[...] = ((yn + bonus) * g_ref[...]).astype(o_ref.dtype)


def _rw_post(y, r, k, v, g, r_k, gn_g, gn_b, ones_bd, tt):
    n = y.shape[0]
    row = lambda: pl.BlockSpec((tt, RWKV_WIDTH), lambda i: (i, 0))
    vec = lambda: pl.BlockSpec((1, RWKV_WIDTH), lambda i: (0, 0))
    return pl.pallas_call(
        _rw_post_kernel,
        grid=(n // tt,),
        in_specs=[row(), row(), row(), row(), row(), vec(), vec(), vec(),
                  pl.BlockSpec((MXU_DIM, MXU_DIM), lambda i: (0, 0))],
        out_specs=row(),
        out_shape=jax.ShapeDtypeStruct((n, RWKV_WIDTH), BF16),
        compiler_params=_cparams(("parallel",)),
        name="rw_post",
    )(y, r, k, v, g, r_k, gn_g, gn_b, ones_bd)


def _out_proj_kernel(yl_ref, yr_ref, h0_ref, wo_ref, g_ref, b_ref, wr_ref, br_ref,
                     h1_ref, h1p_ref, ti_ref, tg_ref, rank_ref, cnt_ref, carry, *, tm):
    @pl.when(pl.program_id(0) == 0)
    def _():
        carry[...] = jnp.zeros_like(carry)

    mix = (jnp.dot(yl_ref[...], wo_ref[0:LRU_WIDTH, :], preferred_element_type=F32)
           + jnp.dot(yr_ref[...], wo_ref[LRU_WIDTH:, :], preferred_element_type=F32))
    h1 = _layer_norm_rows(DEEPNORM_ALPHA * h0_ref[...] + mix, g_ref[...], b_ref[...])
    h1_ref[...] = h1
    _store_rows(h1p_ref, 0, _pack_halves(h1))

    h_hi, h_lo = _split2(h1)
    w_hi = wr_ref[0]
    w_lo = wr_ref[1]
    logits = (jnp.dot(h_hi, w_hi, preferred_element_type=F32)
              + jnp.dot(h_hi, w_lo, preferred_element_type=F32)
              + jnp.dot(h_lo, w_hi, preferred_element_type=F32)) + br_ref[...]

    lane = lax.broadcasted_iota(jnp.int32, (tm, N_EXPERTS), 1).astype(F32)
    lane4 = lax.broadcasted_iota(jnp.int32, (tm, TOP_K), 1)
    work = logits
    vals, idxs, sels = [], [], []
    for _ in range(TOP_K):
        m = jnp.max(work, axis=-1, keepdims=True)
        idx = jnp.min(jnp.where(work == m, lane, float(N_EXPERTS)), axis=-1, keepdims=True)
        sel = lane == idx
        vals.append(m)
        idxs.append(idx)
        sels.append(sel)
        work = jnp.where(sel, -jnp.inf, work)
    exps = [jnp.exp(vv - vals[0]) for vv in vals]
    denom = exps[0] + exps[1] + exps[2] + exps[3]

    member = jnp.zeros((tm, N_EXPERTS), F32)
    for sel in sels:
        member = member + sel.astype(F32)
    ri = lax.broadcasted_iota(jnp.int32, (tm, tm), 0)
    ci = lax.broadcasted_iota(jnp.int32, (tm, tm), 1)
    before = (ci < ri).astype(BF16)
    rank_full = jnp.dot(before, member.astype(BF16), preferred_element_type=F32) + carry[...]
    carry[...] = carry[...] + jnp.sum(member, axis=0, keepdims=True)
    cnt_ref[...] = carry[...].astype(jnp.int32)

    ti = jnp.zeros((tm, TOP_K), jnp.int32)
    tg = jnp.zeros((tm, TOP_K), F32)
    rk = jnp.zeros((tm, TOP_K), jnp.int32)
    for kq in range(TOP_K):
        rank_k = jnp.sum(jnp.where(sels[kq], rank_full, 0.0), axis=-1, keepdims=True)
        ti = jnp.where(lane4 == kq, idxs[kq].astype(jnp.int32), ti)
        tg = jnp.where(lane4 == kq, exps[kq] / denom, tg)
        rk = jnp.where(lane4 == kq, rank_k.astype(jnp.int32), rk)
    ti_ref[...] = ti
    tg_ref[...] = tg
    rank_ref[...] = rk


def _out_proj(y_lru, y_rw, h0, w_out_bf16, g, b, w_router_split, b_router, tm):
    n = h0.shape[0]
    row = lambda w: pl.BlockSpec((tm, w), lambda i: (i, 0))
    vec = lambda w: pl.BlockSpec((1, w), lambda i: (0, 0))
    return pl.pallas_call(
        functools.partial(_out_proj_kernel, tm=tm),
        grid=(n // tm,),
        in_specs=[
            row(LRU_WIDTH), row(RWKV_WIDTH), row(D_MODEL),
            pl.BlockSpec((D_MODEL, D_MODEL), lambda i: (0, 0)),
            vec(D_MODEL), vec(D_MODEL),
            pl.BlockSpec((2, D_MODEL, N_EXPERTS), lambda i: (0, 0, 0)),
            vec(N_EXPERTS),
        ],
        out_specs=[row(D_MODEL), pl.BlockSpec((tm * ROW_TILES, LANE), lambda i: (i, 0)),
                   row(TOP_K), row(TOP_K), row(TOP_K), vec(N_EXPERTS)],
        out_shape=[
            jax.ShapeDtypeStruct((n, D_MODEL), F32),
            jax.ShapeDtypeStruct((n * ROW_TILES, LANE), U32),
            jax.ShapeDtypeStruct((n, TOP_K), jnp.int32),
            jax.ShapeDtypeStruct((n, TOP_K), F32),
            jax.ShapeDtypeStruct((n, TOP_K), jnp.int32),
            jax.ShapeDtypeStruct((1, N_EXPERTS), jnp.int32),
        ],
        scratch_shapes=[pltpu.VMEM((1, N_EXPERTS), F32)],
        compiler_params=_cparams(("arbitrary",)),
        name="out_proj",
    )(y_lru, y_rw, h0, w_out_bf16, g, b, w_router_split, b_router)


def _scatter_kernel(dest_ref, fill_ref, h_ref, xs_ref, zbuf, sem, zsem, *, tb, tm, nb):
    @pl.when(pl.program_id(0) == 0)
    def _():
        zbuf[...] = jnp.zeros_like(zbuf)

        def fill_copy(start):
            start = pl.multiple_of(start * ROW_TILES, SUBLANE)
            return pltpu.make_async_copy(zbuf, xs_ref.at[pl.ds(start, tm * ROW_TILES), :], zsem)

        def fill(e, carry):
            @pl.when(fill_ref[e] >= 0)
            def _():
                fill_copy(fill_ref[e]).start()
            return carry

        def fill_wait(e, carry):
            @pl.when(fill_ref[e] >= 0)
            def _():
                fill_copy(fill_ref[e]).wait()
            return carry

        def tail(blk, carry):
            fill_copy(blk * tm).start()
            return carry

        def tail_wait(blk, carry):
            fill_copy(blk * tm).wait()
            return carry

        n_used = fill_ref[N_EXPERTS]
        lax.fori_loop(0, N_EXPERTS, fill, 0)
        lax.fori_loop(n_used, nb, tail, 0)
        lax.fori_loop(0, N_EXPERTS, fill_wait, 0)
        lax.fori_loop(n_used, nb, tail_wait, 0)

    base = pl.program_id(0) * (tb * TOP_K)

    def copy(t, kq):
        slot = dest_ref[base + t * TOP_K + kq]
        return pltpu.make_async_copy(_row_tile(h_ref, t), _row_tile(xs_ref, slot), sem)

    def issue(t, carry):
        for kq in range(TOP_K):
            copy(t, kq).start()
        return carry

    lax.fori_loop(0, tb, issue, 0, unroll=4)
    for kq in range(TOP_K):
        pltpu.make_async_copy(h_ref, xs_ref.at[pl.ds(0, tb * ROW_TILES), :], sem).wait()


def _scatter(dest_flat, fill_start, h1p, n_slots, tb, tm):
    n = h1p.shape[0] // ROW_TILES
    nb = n_slots // tm
    grid_spec = pltpu.PrefetchScalarGridSpec(
        num_scalar_prefetch=2,
        grid=(n // tb,),
        in_specs=[pl.BlockSpec((tb * ROW_TILES, LANE), lambda i, d, f: (i, 0))],
        out_specs=pl.BlockSpec(memory_space=pl.ANY),
        scratch_shapes=[pltpu.VMEM((tm * ROW_TILES, LANE), U32), pltpu.SemaphoreType.DMA(()),
                        pltpu.SemaphoreType.DMA(())],
    )
    return pl.pallas_call(
        functools.partial(_scatter_kernel, tb=tb, tm=tm, nb=nb),
        grid_spec=grid_spec,
        out_shape=jax.ShapeDtypeStruct((n_slots * ROW_TILES, LANE), U32),
        compiler_params=_cparams(("arbitrary",)),
        name="scatter",
    )(dest_flat, fill_start, h1p)


def _gmm1_kernel(meta_ref, x_ref, wg_ref, wu_ref, bg_ref, bu_ref, o_ref, wgb, wub, *, nb, tm):
    i = pl.program_id(1)
    e = meta_ref[i]
    prev_e = meta_ref[jnp.maximum(i - 1, 0)]
    valid = meta_ref[nb + 1 + i]
    half = tm // 2

    @pl.when(jnp.logical_or(i == 0, e != prev_e))
    def _():
        wgb[...] = wg_ref[0].astype(BF16)
        wub[...] = wu_ref[0].astype(BF16)

    for part in range(2):
        rows = pl.ds(part * half, half)

        @pl.when(valid > part * half)
        def _():
            x_lo, x_hi = _unpack_halves(_load_rows(x_ref, part * half, half))
            x_lo = x_lo.astype(BF16)
            x_hi = x_hi.astype(BF16)
            gate = (jnp.dot(x_lo, wgb[0:HALF, :], preferred_element_type=F32)
                    + jnp.dot(x_hi, wgb[HALF:, :], preferred_element_type=F32) + bg_ref[0])
            up = (jnp.dot(x_lo, wub[0:HALF, :], preferred_element_type=F32)
                  + jnp.dot(x_hi, wub[HALF:, :], preferred_element_type=F32) + bu_ref[0])
            gate = jnp.minimum(gate, SWIGLU_LIMIT)
            up = jnp.clip(up, -SWIGLU_LIMIT, SWIGLU_LIMIT)
            act = gate * _sigmoid(SWIGLU_ALPHA * gate) * (up + 1.0)
            o_ref[rows, :] = act.astype(o_ref.dtype)

        @pl.when(valid <= part * half)
        def _():
            o_ref[rows, :] = jnp.zeros((half, o_ref.shape[1]), o_ref.dtype)


def _gmm1(meta, xs, w1, b1, nb, tm, tn):
    n_slots = xs.shape[0] // ROW_TILES
    nj = D_EXPERT // tn

    def row_blk(j, i, m):
        return jnp.minimum(i, m[nb] - 1)

    grid_spec = pltpu.PrefetchScalarGridSpec(
        num_scalar_prefetch=1,
        grid=(nj, nb),
        in_specs=[
            pl.BlockSpec((tm * ROW_TILES, LANE), lambda j, i, m: (row_blk(j, i, m), 0)),
            pl.BlockSpec((1, D_MODEL, tn), lambda j, i, m: (m[i], 0, j)),
            pl.BlockSpec((1, D_MODEL, tn), lambda j, i, m: (m[i], 0, j + nj)),
            pl.BlockSpec((1, 1, tn), lambda j, i, m: (m[i], 0, j)),
            pl.BlockSpec((1, 1, tn), lambda j, i, m: (m[i], 0, j + nj)),
        ],
        out_specs=pl.BlockSpec((tm, tn), lambda j, i, m: (i, j)),
        scratch_shapes=[pltpu.VMEM((D_MODEL, tn), BF16), pltpu.VMEM((D_MODEL, tn), BF16)],
    )
    return pl.pallas_call(
        functools.partial(_gmm1_kernel, nb=nb, tm=tm),
        grid_spec=grid_spec,
        out_shape=jax.ShapeDtypeStruct((n_slots, D_EXPERT), BF16),
        compiler_params=_cparams(("arbitrary", "arbitrary")),
        name="gmm1",
    )(meta, xs, w1, w1, b1, b1)


def _gmm2_kernel(meta_ref, a_ref, w_ref, b_ref, o_ref, wb, *, nb, tm):
    i = pl.program_id(1)
    e = meta_ref[i]
    prev_e = meta_ref[jnp.maximum(i - 1, 0)]
    valid = meta_ref[nb + 1 + i]
    half = tm // 2

    @pl.when(jnp.logical_or(i == 0, e != prev_e))
    def _():
        wb[...] = w_ref[0].astype(BF16)

    for part in range(2):
        @pl.when(valid > part * half)
        def _():
            a = a_ref[pl.ds(part * half, half), :]
            out = jnp.dot(a, wb[...], preferred_element_type=F32) + b_ref[0]
            _store_rows(o_ref, part * half, _pack_halves(out))

        @pl.when(valid <= part * half)
        def _():
            o_ref[pl.ds(part * half * ROW_TILES, half * ROW_TILES), :] = jnp.zeros(
                (half * ROW_TILES, LANE), o_ref.dtype)


def _gmm2(meta, act, w2, b2, nb, tm):
    n_slots = act.shape[0]

    def row_blk(j, i, m):
        return jnp.minimum(i, m[nb] - 1)

    grid_spec = pltpu.PrefetchScalarGridSpec(
        num_scalar_prefetch=1,
        grid=(1, nb),
        in_specs=[
            pl.BlockSpec((tm, D_EXPERT), lambda j, i, m: (row_blk(j, i, m), 0)),
            pl.BlockSpec((1, D_EXPERT, D_MODEL), lambda j, i, m: (m[i], 0, 0)),
            pl.BlockSpec((1, 1, D_MODEL), lambda j, i, m: (m[i], 0, 0)),
        ],
        out_specs=pl.BlockSpec((tm * ROW_TILES, LANE), lambda j, i, m: (i, 0)),
        scratch_shapes=[pltpu.VMEM((D_EXPERT, D_MODEL), BF16)],
    )
    return pl.pallas_call(
        functools.partial(_gmm2_kernel, nb=nb, tm=tm),
        grid_spec=grid_spec,
        out_shape=jax.ShapeDtypeStruct((n_slots * ROW_TILES, LANE), U32),
        compiler_params=_cparams(("arbitrary", "arbitrary")),
        name="gmm2",
    )(meta, act, w2, b2)


def _combine_kernel(dest_ref, h1_ref, tg_ref, g_ref, b_ref, ys_ref, o_ref, buf, sems, *, tb, n_steps):
    i = pl.program_id(0)

    def issue(step, half):
        base = step * (tb * TOP_K)

        def body(t, carry):
            for kq in range(TOP_K):
                slot = dest_ref[base + t * TOP_K + kq]
                pltpu.make_async_copy(_row_tile(ys_ref, slot), _row_tile(buf.at[half, kq], t),
                                      sems.at[half]).start()
            return carry

        lax.fori_loop(0, tb, body, 0, unroll=4)

    @pl.when(i == 0)
    def _():
        issue(0, 0)

    @pl.when(i + 1 < n_steps)
    def _():
        issue(i + 1, (i + 1) % 2)

    half = i % 2
    for kq in range(TOP_K):
        pltpu.make_async_copy(ys_ref.at[pl.ds(0, tb * ROW_TILES), :], buf.at[half, kq],
                              sems.at[half]).wait()

    tg = tg_ref[...]
    y = DEEPNORM_ALPHA * h1_ref[...]
    for kq in range(TOP_K):
        lo, hi = _unpack_halves(_load_rows(buf.at[half, kq], 0, tb))
        y = y + tg[:, kq:kq + 1] * jnp.concatenate([lo, hi], axis=1)
    o_ref[...] = _layer_norm_rows(y, g_ref[...], b_ref[...])


def _combine(dest_flat, h1, tg, g, b, ys, tb):
    n = h1.shape[0]
    grid_spec = pltpu.PrefetchScalarGridSpec(
        num_scalar_prefetch=1,
        grid=(n // tb,),
        in_specs=[
            pl.BlockSpec((tb, D_MODEL), lambda i, d: (i, 0)),
            pl.BlockSpec((tb, TOP_K), lambda i, d: (i, 0)),
            pl.BlockSpec((1, D_MODEL), lambda i, d: (0, 0)),
            pl.BlockSpec((1, D_MODEL), lambda i, d: (0, 0)),
            pl.BlockSpec(memory_space=pl.ANY),
        ],
        out_specs=pl.BlockSpec((tb, D_MODEL), lambda i, d: (i, 0)),
        scratch_shapes=[pltpu.VMEM((2, TOP_K, tb * ROW_TILES, LANE), U32),
                        pltpu.SemaphoreType.DMA((2,))],
    )
    return pl.pallas_call(
        functools.partial(_combine_kernel, tb=tb, n_steps=n // tb),
        grid_spec=grid_spec,
        out_shape=jax.ShapeDtypeStruct((n, D_MODEL), F32),
        compiler_params=_cparams(("arbitrary",)),
        name="combine",
    )(dest_flat, h1, tg, g, b, ys)


def _block_diag(w, group):
    nb = w.shape[0] // group
    w = w.reshape(nb, group, LRU_BLOCK_W, LRU_BLOCK_W)
    eye = jnp.eye(group, dtype=w.dtype)
    out = jnp.einsum('gajk,ab->gajbk', w, eye)
    return out.reshape(nb, group * LRU_BLOCK_W, group * LRU_BLOCK_W)


def _pad_rows(w, rows):
    return jnp.pad(w, ((0, rows - w.shape[0]), (0, 0)))


def _layer(h_in_x, l, prm, bsz, t):
    n = bsz * t
    row = lambda a: a.reshape(1, -1)

    w_in = prm['w_in'][l]
    o_rw = 2 * LRU_WIDTH
    o_l = o_rw + 3 * RWKV_WIDTH
    pad_c = lambda w: jnp.pad(w, ((0, 0), (0, LORA_PAD - w.shape[1])))
    w_in_p = jnp.concatenate([
        w_in[:, :o_l],
        pad_c(w_in[:, o_l:o_l + DECAY_LORA]),
        pad_c(w_in[:, o_l + DECAY_LORA:o_l + DECAY_LORA + AAA_LORA]),
        w_in[:, o_l + DECAY_LORA + AAA_LORA:],
    ], axis=1).astype(BF16)
    mu = prm['shift_mu'][l]
    pad_v = lambda v: jnp.pad(v, (0, LORA_PAD - v.shape[0]))
    mu_l = mu[3 * RWKV_WIDTH:]
    mu_a = jnp.concatenate([pad_v(mu_l[:DECAY_LORA]), pad_v(mu_l[DECAY_LORA:DECAY_LORA + AAA_LORA]),
                            mu_l[DECAY_LORA + AAA_LORA:]])

    tm_in = min(1024, n)
    h0, p = _in_proj(h_in_x, row(prm['ln_in_g']), row(prm['ln_in_b']), w_in_p, tm_in, 512)

    tt = min(512, t)
    group = MXU_DIM // LRU_BLOCK_W
    y_lru = _lru(p, prm['conv_w'][l], row(prm['conv_b'][l]),
                 _block_diag(prm['w_rgate'][l], group).astype(BF16), row(prm['b_rgate'][l]),
                 _block_diag(prm['w_igate'][l], group).astype(BF16), row(prm['b_igate'][l]),
                 row(prm['lru_lambda'][l]), bsz, t, tt)

    head_id = jnp.arange(MXU_DIM) // HEAD_SIZE
    ones_bd = (head_id[:, None] == head_id[None, :]).astype(BF16)
    tt_rw = min(256, t)
    r, k, v, kk, bvec, lw, g = _rw_prep(
        p, row(mu[:RWKV_WIDTH]), row(mu[RWKV_WIDTH:2 * RWKV_WIDTH]),
        row(mu[2 * RWKV_WIDTH:3 * RWKV_WIDTH]), row(mu_a),
        row(prm['w0'][l]), _pad_rows(prm['rw_decay_up'][l], LORA_PAD).astype(BF16),
        row(prm['a0'][l]), _pad_rows(prm['rw_aaa_up'][l], LORA_PAD).astype(BF16),
        prm['rw_gate_up'][l].astype(BF16), row(prm['k_k'][l]), row(prm['k_a'][l]),
        ones_bd, bsz, t, tt_rw)
    y = _rw_chunk(r, k, v, kk, bvec, lw, bsz, t, 4 if bsz % 4 == 0 else 1)
    y_rw = _rw_post(y, r, k, v, g, row(prm['r_k'][l]), row(prm['gn_g'][l]), row(prm['gn_b'][l]),
                    ones_bd, min(512, n))

    w_r = prm['w_router'][l]
    w_r_hi = w_r.astype(BF16)
    w_r_lo = (w_r - w_r_hi.astype(F32)).astype(BF16)
    tm_out = min(512, n)
    h1, h1p, top_i, top_g, rank, counts = _out_proj(
        y_lru, y_rw, h0, prm['w_out'][l].astype(BF16), row(prm['ln1_g'][l]), row(prm['ln1_b'][l]),
        jnp.stack([w_r_hi, w_r_lo]), row(prm['b_router'][l]), tm_out)

    tm = 512
    counts = counts.reshape(N_EXPERTS)
    padded = ((counts + tm - 1) // tm) * tm
    pad_end = jnp.cumsum(padded)
    pad_start = pad_end - padded
    dest = (pad_start[top_i] + rank).reshape(-1).astype(jnp.int32)
    nb = (n * TOP_K) // tm + N_EXPERTS
    n_slots = nb * tm
    n_used = (pad_end[-1] // tm).astype(jnp.int32)
    blk = jnp.minimum(jnp.arange(nb, dtype=jnp.int32), n_used - 1)
    blk_e = jnp.sum((pad_end[None, :] <= (blk * tm)[:, None]).astype(jnp.int32), axis=1)
    blk_e = jnp.minimum(blk_e, N_EXPERTS - 1)
    blk_ids = jnp.arange(nb, dtype=jnp.int32)
    valid = jnp.clip(pad_start[blk_e] + counts[blk_e] - blk_ids * tm, 0, tm)
    valid = jnp.where(blk_ids < n_used, valid, 0)
    meta = jnp.concatenate([blk_e.astype(jnp.int32), n_used.reshape(1), valid.astype(jnp.int32)])

    fill_start = jnp.concatenate([jnp.where(padded > 0, pad_end - tm, -1).astype(jnp.int32),
                                  n_used.reshape(1)])
    tb = min(128, n)
    xs = _scatter(dest, fill_start, h1p, n_slots, tb, tm)
    act = _gmm1(meta, xs, prm['w_exp1'][l], prm['b_exp1'][l].reshape(N_EXPERTS, 1, 2 * D_EXPERT),
                nb, tm, 1024)
    ys = _gmm2(meta, act, prm['w_exp2'][l], prm['b_exp2'][l].reshape(N_EXPERTS, 1, D_MODEL),
               nb, tm)
    return _combine(dest, h1, top_g, row(prm['ln2_g'][l]), row(prm['ln2_b'][l]),
                    ys, tb)


def kernel(x, ln_in_g, ln_in_b, w_in, conv_w, conv_b, w_rgate, b_rgate, w_igate, b_igate, lru_lambda, shift_mu, w0, rw_decay_up, a0, rw_aaa_up, rw_gate_up, k_k, k_a, r_k, gn_g, gn_b, w_out, ln1_g, ln1_b, w_router, b_router, w_exp1, b_exp1, w_exp2, b_exp2, ln2_g, ln2_b):
    bsz, t, d = x.shape
    prm = dict(ln_in_g=ln_in_g, ln_in_b=ln_in_b, w_in=w_in, conv_w=conv_w, conv_b=conv_b,
               w_rgate=w_rgate, b_rgate=b_rgate, w_igate=w_igate, b_igate=b_igate,
               lru_lambda=lru_lambda, shift_mu=shift_mu, w0=w0, rw_decay_up=rw_decay_up, a0=a0,
               rw_aaa_up=rw_aaa_up, rw_gate_up=rw_gate_up, k_k=k_k, k_a=k_a, r_k=r_k, gn_g=gn_g,
               gn_b=gn_b, w_out=w_out, ln1_g=ln1_g, ln1_b=ln1_b, w_router=w_router,
               b_router=b_router, w_exp1=w_exp1, b_exp1=b_exp1, w_exp2=w_exp2, b_exp2=b_exp2,
               ln2_g=ln2_g, ln2_b=ln2_b)
    out = _layer(x.reshape(bsz * t, d), 0, prm, bsz, t)
    return out.reshape(bsz, t, d)
```

```python
import functools

import jax
import jax.numpy as jnp
from jax import lax
from jax.experimental import pallas as pl
from jax.experimental.pallas import tpu as pltpu

D_MODEL = 2048
DEPTH = 1
CHUNK = 64
LRU_WIDTH = 1024
LRU_BLOCKS = 16
LRU_BLOCK_W = LRU_WIDTH // LRU_BLOCKS
CONV_WIDTH = 4
RG_C = 8.0
RWKV_WIDTH = D_MODEL - LRU_WIDTH
HEAD_SIZE = 64
RWKV_HEADS = RWKV_WIDTH // HEAD_SIZE
DECAY_LORA = 96
AAA_LORA = 96
GATE_LORA = 256
N_EXPERTS = 32
TOP_K = 4
D_EXPERT = D_MODEL
SWIGLU_LIMIT = 7.0
SWIGLU_ALPHA = 1.702
LN_EPS = 1e-5
GN_EPS = HEAD_SIZE * 1e-5
DEEPNORM_ALPHA = (2.0 * DEPTH) ** 0.25

LANE = 128
SUBLANE = 8
MXU_DIM = 256
LORA_PAD = 128
LORA_W = 2 * LORA_PAD + GATE_LORA
P_WIDTH = 2 * LRU_WIDTH + 3 * RWKV_WIDTH + LORA_W
VMEM_LIMIT = 56 * 1024 * 1024

F32 = jnp.float32
BF16 = jnp.bfloat16


def _cparams(sem):
    return pltpu.CompilerParams(dimension_semantics=sem, vmem_limit_bytes=VMEM_LIMIT)


def _layer_norm_rows(x, g, b):
    mu = jnp.mean(x, axis=-1, keepdims=True)
    xc = x - mu
    var = jnp.mean(xc * xc, axis=-1, keepdims=True)
    return xc * lax.rsqrt(var + LN_EPS) * g + b


def _softplus(z):
    return jnp.maximum(z, 0.0) + jnp.log1p(jnp.exp(-jnp.abs(z)))


def _sigmoid(z):
    return 1.0 / (1.0 + jnp.exp(-z))


def _split2(x):
    hi = x.astype(BF16)
    lo = (x - hi.astype(F32)).astype(BF16)
    return hi, lo


HALF = D_MODEL // 2
U32 = jnp.uint32


def _pack_halves(x):
    lo = lax.bitcast_convert_type(x[:, :HALF].astype(BF16).astype(F32), U32) >> 16
    hi = lax.bitcast_convert_type(x[:, HALF:].astype(BF16).astype(F32), U32) & jnp.uint32(0xFFFF0000)
    return hi | lo


def _unpack_halves(w):
    lo = lax.bitcast_convert_type(w << 16, F32)
    hi = lax.bitcast_convert_type(w & jnp.uint32(0xFFFF0000), F32)
    return lo, hi


ROW_TILES = HALF // LANE
assert ROW_TILES == SUBLANE


def _store_rows(ref, row0, packed):
    m = packed.shape[0]
    for c in range(ROW_TILES):
        ref[pl.ds(row0 * ROW_TILES + c, m, stride=ROW_TILES), :] = packed[:, c * LANE:(c + 1) * LANE]


def _load_rows(ref, row0, m):
    return jnp.concatenate(
        [ref[pl.ds(row0 * ROW_TILES + c, m, stride=ROW_TILES), :] for c in range(ROW_TILES)], axis=1)


def _row_tile(ref, row):
    return ref.at[pl.ds(pl.multiple_of(row * ROW_TILES, ROW_TILES), ROW_TILES), :]


def _head_sum(x, ones_ref):
    hi, lo = _split2(x)
    ones = ones_ref[...]
    parts = []
    for g in range(x.shape[1] // MXU_DIM):
        sl = slice(g * MXU_DIM, (g + 1) * MXU_DIM)
        parts.append(jnp.dot(hi[:, sl], ones, preferred_element_type=F32)
                     + jnp.dot(lo[:, sl], ones, preferred_element_type=F32))
    return jnp.concatenate(parts, axis=1)


def _in_proj_kernel(x_ref, g_ref, b_ref, w_ref, h_ref, p_ref, hb_ref):
    @pl.when(pl.program_id(1) == 0)
    def _():
        h = _layer_norm_rows(x_ref[...], g_ref[...], b_ref[...])
        h_ref[...] = h
        hb_ref[...] = h.astype(BF16)

    p_ref[...] = jnp.dot(hb_ref[...], w_ref[...], preferred_element_type=F32)


def _in_proj(x2, g, b, w_bf16, tm, tn):
    n = x2.shape[0]
    return pl.pallas_call(
        _in_proj_kernel,
        grid=(n // tm, P_WIDTH // tn),
        in_specs=[
            pl.BlockSpec((tm, D_MODEL), lambda i, j: (i, 0)),
            pl.BlockSpec((1, D_MODEL), lambda i, j: (0, 0)),
            pl.BlockSpec((1, D_MODEL), lambda i, j: (0, 0)),
            pl.BlockSpec((D_MODEL, tn), lambda i, j: (0, j)),
        ],
        out_specs=[
            pl.BlockSpec((tm, D_MODEL), lambda i, j: (i, 0)),
            pl.BlockSpec((tm, tn), lambda i, j: (i, j)),
        ],
        out_shape=[
            jax.ShapeDtypeStruct((n, D_MODEL), F32),
            jax.ShapeDtypeStruct((n, P_WIDTH), F32),
        ],
        scratch_shapes=[pltpu.VMEM((tm, D_MODEL), BF16)],
        compiler_params=_cparams(("parallel", "arbitrary")),
        name="in_proj",
    )(x2, g, b, w_bf16)


def _lru_kernel(u_ref, gi_ref, cw_ref, cb_ref, wr_ref, br_ref, wi_ref, bi_ref,
                lam_ref, o_ref, ubuf, a_s, b_s, carry, *, tt):
    first = pl.program_id(1) == 0

    @pl.when(first)
    def _():
        ubuf[0:SUBLANE, :] = jnp.zeros((SUBLANE, LRU_WIDTH), F32)
        carry[...] = jnp.zeros_like(carry)

    @pl.when(jnp.logical_not(first))
    def _():
        ubuf[0:SUBLANE, :] = ubuf[tt:tt + SUBLANE, :]

    ubuf[SUBLANE:, :] = u_ref[...]
    uc = cb_ref[...]
    for i in range(CONV_WIDTH):
        off = SUBLANE - (CONV_WIDTH - 1) + i
        uc = uc + cw_ref[i:i + 1, :] * ubuf[off:off + tt, :]

    ucb = uc.astype(BF16)
    n_grp = LRU_WIDTH // MXU_DIM

    def gate(w_ref, bias_ref):
        parts = [jnp.dot(ucb[:, g * MXU_DIM:(g + 1) * MXU_DIM], w_ref[g],
                         preferred_element_type=F32) for g in range(n_grp)]
        return _sigmoid(jnp.concatenate(parts, axis=1) + bias_ref[...])

    r_gate = gate(wr_ref, br_ref)
    i_gate = gate(wi_ref, bi_ref)
    log_a = (-RG_C * r_gate) * _softplus(-lam_ref[...])
    a_s[...] = jnp.exp(log_a)
    th = jnp.tanh(log_a)
    b_s[...] = jnp.sqrt(-2.0 * th / (1.0 - th)) * (i_gate * uc)

    row = lax.broadcasted_iota(jnp.int32, (SUBLANE, LRU_WIDTH), 0)

    def group(gidx, c):
        off = pl.multiple_of(gidx * SUBLANE, SUBLANE)
        a = a_s[pl.ds(off, SUBLANE), :]
        b = b_s[pl.ds(off, SUBLANE), :]
        for d in (1, 2, 4):
            keep = row >= d
            a_sh = pltpu.roll(a, d, axis=0)
            b_sh = pltpu.roll(b, d, axis=0)
            b = jnp.where(keep, a * b_sh + b, b)
            a = jnp.where(keep, a * a_sh, a)
        h = a * c + b
        b_s[pl.ds(off, SUBLANE), :] = h
        return h[SUBLANE - 1:SUBLANE, :]

    carry[...] = lax.fori_loop(0, tt // SUBLANE, group, carry[...])
    o_ref[...] = (b_s[...] * jax.nn.gelu(gi_ref[...])).astype(o_ref.dtype)


def _lru(p, conv_w, conv_b, wr_bd, br, wi_bd, bi, lam, bsz, t, tt):
    n = bsz * t
    nt = t // tt
    vec = lambda: pl.BlockSpec((1, LRU_WIDTH), lambda b, i: (0, 0))
    wspec = lambda: pl.BlockSpec((LRU_WIDTH // MXU_DIM, MXU_DIM, MXU_DIM), lambda b, i: (0, 0, 0))
    return pl.pallas_call(
        functools.partial(_lru_kernel, tt=tt),
        grid=(bsz, nt),
        in_specs=[
            pl.BlockSpec((tt, LRU_WIDTH), lambda b, i: (b * nt + i, 0)),
            pl.BlockSpec((tt, LRU_WIDTH), lambda b, i: (b * nt + i, 1)),
            pl.BlockSpec((CONV_WIDTH, LRU_WIDTH), lambda b, i: (0, 0)),
            vec(), wspec(), vec(), wspec(), vec(), vec(),
        ],
        out_specs=pl.BlockSpec((tt, LRU_WIDTH), lambda b, i: (b * nt + i, 0)),
        out_shape=jax.ShapeDtypeStruct((n, LRU_WIDTH), BF16),
        scratch_shapes=[
            pltpu.VMEM((tt + SUBLANE, LRU_WIDTH), F32),
            pltpu.VMEM((tt, LRU_WIDTH), F32),
            pltpu.VMEM((tt, LRU_WIDTH), F32),
            pltpu.VMEM((1, LRU_WIDTH), F32),
        ],
        compiler_params=_cparams(("parallel", "arbitrary")),
        name="lru",
    )(p, p, conv_w, conv_b, wr_bd, br, wi_bd, bi, lam)


def _rw_prep_kernel(pr_ref, pk_ref, pv_ref, pa_ref, mur_ref, muk_ref, muv_ref, mua_ref,
                    w0_ref, wdec_ref, a0_ref, waaa_ref, wgate_ref, kk_ref, ka_ref, ones_ref,
                    r_out, k_out, v_out, kkn_out, b_out, cl_out, g_out,
                    prev_r, prev_k, prev_v, prev_a, *, tt):
    first = pl.program_id(1) == 0

    @pl.when(first)
    def _():
        for ref in (prev_r, prev_k, prev_v, prev_a):
            ref[...] = jnp.zeros_like(ref)

    def shift(x_ref, prev_ref, mu_ref):
        x = x_ref[...]
        row = lax.broadcasted_iota(jnp.int32, x.shape, 0)
        prev = jnp.where(row == 0, prev_ref[...], pltpu.roll(x, 1, axis=0))
        prev_ref[...] = x[tt - 1:tt, :]
        return x + (prev - x) * mu_ref[...]

    r = shift(pr_ref, prev_r, mur_ref)
    k = shift(pk_ref, prev_k, muk_ref)
    v = shift(pv_ref, prev_v, muv_ref)
    ad = shift(pa_ref, prev_a, mua_ref)

    wd = jnp.tanh(ad[:, 0:LORA_PAD]).astype(BF16)
    aa = ad[:, LORA_PAD:2 * LORA_PAD].astype(BF16)
    gd = _sigmoid(ad[:, 2 * LORA_PAD:]).astype(BF16)
    w_pre = w0_ref[...] + jnp.dot(wd, wdec_ref[...], preferred_element_type=F32)
    w_log = -_softplus(-w_pre) - 0.5
    a = _sigmoid(a0_ref[...] + jnp.dot(aa, waaa_ref[...], preferred_element_type=F32))
    g = jnp.dot(gd, wgate_ref[...], preferred_element_type=F32)

    kk = k * kk_ref[...]
    norm = jnp.sqrt(_head_sum(kk * kk, ones_ref))
    kk = kk / jnp.maximum(norm, 1e-12)

    lw = -jnp.exp(w_log)
    ri = lax.broadcasted_iota(jnp.int32, (tt, tt), 0)
    ci = lax.broadcasted_iota(jnp.int32, (tt, tt), 1)
    tri = jnp.logical_and(ri // CHUNK == ci // CHUNK, ri >= ci).astype(BF16)
    lw_hi = lw.astype(BF16)
    rem = lw - lw_hi.astype(F32)
    lw_mid = rem.astype(BF16)
    lw_lo = (rem - lw_mid.astype(F32)).astype(BF16)
    cl = (jnp.dot(tri, lw_hi, preferred_element_type=F32)
          + jnp.dot(tri, lw_mid, preferred_element_type=F32)
          + jnp.dot(tri, lw_lo, preferred_element_type=F32))

    r_out[...] = r
    k_out[...] = k * (1.0 + (a - 1.0) * ka_ref[...])
    v_out[...] = v
    kkn_out[...] = kk
    b_out[...] = kk * a
    cl_out[...] = cl
    g_out[...] = g


def _rw_prep(p, mu_r, mu_k, mu_v, mu_a, w0, wdec, a0, waaa, wgate, k_k, k_a, ones_bd, bsz, t, tt):
    n = bsz * t
    nt = t // tt
    cb = 2 * LRU_WIDTH // RWKV_WIDTH
    row = lambda c: pl.BlockSpec((tt, RWKV_WIDTH), lambda b, i: (b * nt + i, c))
    vec = lambda w: pl.BlockSpec((1, w), lambda b, i: (0, 0))
    full = lambda s: pl.BlockSpec(s, lambda b, i: (0, 0))
    lora_cb = (2 * LRU_WIDTH + 3 * RWKV_WIDTH) // LORA_W
    out = jax.ShapeDtypeStruct((n, RWKV_WIDTH), F32)
    return pl.pallas_call(
        functools.partial(_rw_prep_kernel, tt=tt),
        grid=(bsz, nt),
        in_specs=[
            row(cb), row(cb + 1), row(cb + 2),
            pl.BlockSpec((tt, LORA_W), lambda b, i: (b * nt + i, lora_cb)),
            vec(RWKV_WIDTH), vec(RWKV_WIDTH), vec(RWKV_WIDTH), vec(LORA_W),
            vec(RWKV_WIDTH), full((LORA_PAD, RWKV_WIDTH)),
            vec(RWKV_WIDTH), full((LORA_PAD, RWKV_WIDTH)),
            full((GATE_LORA, RWKV_WIDTH)),
            vec(RWKV_WIDTH), vec(RWKV_WIDTH),
            full((MXU_DIM, MXU_DIM)),
        ],
        out_specs=[pl.BlockSpec((tt, RWKV_WIDTH), lambda b, i: (b * nt + i, 0))] * 7,
        out_shape=[out] * 7,
        scratch_shapes=[pltpu.VMEM((1, RWKV_WIDTH), F32)] * 3 + [pltpu.VMEM((1, LORA_W), F32)],
        compiler_params=_cparams(("parallel", "arbitrary")),
        name="rw_prep",
    )(p, p, p, p, mu_r, mu_k, mu_v, mu_a, w0, wdec, a0, waaa, wgate, k_k, k_a, ones_bd)


def _rw_chunk_kernel(r_ref, k_ref, v_ref, kk_ref, b_ref, cl_ref, y_ref, *s_refs):
    c = CHUNK
    hs = HEAD_SIZE
    n_seq = r_ref.shape[0]

    @pl.when(pl.program_id(1) == 0)
    def _():
        for s_ref in s_refs:
            s_ref[...] = jnp.zeros_like(s_ref)

    r2 = lax.broadcasted_iota(jnp.int32, (2 * c, 2 * c), 0)
    c2 = lax.broadcasted_iota(jnp.int32, (2 * c, 2 * c), 1)
    tq = jnp.where(r2 >= c, r2 - c, r2)
    tk = jnp.where(c2 >= c, c2 - c, c2)
    mask = tk < tq + jnp.where(r2 >= c, 1, 0)
    zeros_cc = jnp.zeros((c, hs), BF16)

    dn_t = (((1,), (1,)), ((), ()))
    dn_l = (((0,), (0,)), ((), ()))

    row_w = lax.broadcasted_iota(jnp.int32, (c, RWKV_WIDTH), 0)

    def scaled_operands(q):
        cl = cl_ref[q]
        cl_ex = jnp.where(row_w == 0, 0.0, pltpu.roll(cl, 1, axis=0))
        cl_last = cl[c - 1:c, :]
        g_inv = jnp.exp(-cl)
        g_dec = jnp.exp(cl_last - cl)
        bv = b_ref[q]
        kx = k_ref[q]
        rg = r_ref[q] * jnp.exp(cl)
        return dict(
            g_last=jnp.exp(cl_last),
            v=v_ref[q].astype(BF16),
            rg=rg,
            lhs_a=(-kk_ref[q] * jnp.exp(cl_ex)).astype(BF16),
            lhs_r=rg.astype(BF16),
            rhs_b=(bv * g_inv).astype(BF16),
            rhs_k=(kx * g_inv).astype(BF16),
            dec_b=(bv * g_dec).astype(BF16),
            dec_k=(kx * g_dec).astype(BF16),
        )

    ops = [scaled_operands(q) for q in range(n_seq)]

    chains = [(q, h) for q in range(n_seq) for h in range(RWKV_HEADS)]
    sl_of = lambda h: slice(h * hs, (h + 1) * hs)
    s0s = [s_refs[q * RWKV_HEADS + h][...] for q, h in chains]
    aas = []
    for q, h in chains:
        o, sl = ops[q], sl_of(h)
        lhs = jnp.concatenate([o['lhs_a'][:, sl], o['lhs_r'][:, sl]], axis=0)
        rhs = jnp.concatenate([o['rhs_b'][:, sl], o['rhs_k'][:, sl]], axis=0)
        aa = lax.dot_general(lhs, rhs, dn_t, preferred_element_type=F32)
        aas.append(jnp.where(mask, aa, 0.0))
    v_hs = [ops[q]['v'][:, sl_of(h)] for q, h in chains]
    a_rs = [aa[c:, :].astype(BF16) for aa in aas]
    ps = [aa[:c, :c] for aa in aas]
    xs = []
    for i, (q, h) in enumerate(chains):
        akv = jnp.dot(aas[i][:c, c:].astype(BF16), v_hs[i], preferred_element_type=F32)
        xs.append(jnp.concatenate([ops[q]['lhs_a'][:, sl_of(h)].astype(F32), akv], axis=1))
    n_sq = 6
    for it in range(n_sq):
        pbs = [p.astype(BF16) for p in ps]
        xs = [x + jnp.dot(pb, x.astype(BF16), preferred_element_type=F32)
              for x, pb in zip(xs, pbs)]
        if it + 1 < n_sq:
            ps = [jnp.dot(pb, pb, preferred_element_type=F32) for pb in pbs]
    for i, (q, h) in enumerate(chains):
        o, sl = ops[q], sl_of(h)
        rhs2 = jnp.concatenate(
            [xs[i].astype(BF16), jnp.concatenate([zeros_cc, v_hs[i]], axis=1)], axis=0)
        qy = jnp.dot(a_rs[i], rhs2, preferred_element_type=F32)
        dec = jnp.concatenate([o['dec_b'][:, sl], o['dec_k'][:, sl]], axis=0)
        mnt = lax.dot_general(rhs2, dec, dn_l, preferred_element_type=F32)
        s0b = s0s[i].astype(BF16)
        qq = (o['rg'][:, sl] + qy[:, :hs]).astype(BF16)
        y_ref[q, :, sl] = lax.dot_general(qq, s0b, dn_t, preferred_element_type=F32) + qy[:, hs:]
        s_refs[i][...] = (s0s[i] * o['g_last'][:, sl]
                          + jnp.dot(s0b, mnt[:hs, :].astype(BF16), preferred_element_type=F32)
                          + mnt[hs:, :])


def _rw_chunk(r, k, v, kk, b, cl, bsz, t, n_seq):
    nc = t // CHUNK
    shape3 = (bsz, t, RWKV_WIDTH)
    spec = lambda: pl.BlockSpec((n_seq, CHUNK, RWKV_WIDTH), lambda bb, i: (bb, i, 0))
    y = pl.pallas_call(
        _rw_chunk_kernel,
        grid=(bsz // n_seq, nc),
        in_specs=[spec() for _ in range(6)],
        out_specs=spec(),
        out_shape=jax.ShapeDtypeStruct(shape3, F32),
        scratch_shapes=[pltpu.VMEM((HEAD_SIZE, HEAD_SIZE), F32)
                        for _ in range(n_seq * RWKV_HEADS)],
        compiler_params=_cparams(("parallel", "arbitrary")),
        name="rw_chunk",
    )(*[a.reshape(shape3) for a in (r, k, v, kk, b, cl)])
    return y.reshape(bsz * t, RWKV_WIDTH)


def _rw_post_kernel(y_ref, r_ref, k_ref, v_ref, g_ref, rk_ref, gg_ref, gb_ref, ones_ref, o_ref):
    y = y_ref[...]
    inv = 1.0 / HEAD_SIZE
    mu = _head_sum(y, ones_ref) * inv
    yc = y - mu
    var = _head_sum(yc * yc, ones_ref) * inv
    yn = yc * lax.rsqrt(var + GN_EPS) * gg_ref[...] + gb_ref[...]
    bonus = _head_sum(r_ref[...] * k_ref[...] * rk_ref[...], ones_ref) * v_ref[...]
    o_ref[...] = ((yn + bonus) * g_ref[...]).astype(o_ref.dtype)


def _rw_post(y, r, k, v, g, r_k, gn_g, gn_b, ones_bd, tt):
    n = y.shape[0]
    row = lambda: pl.BlockSpec((tt, RWKV_WIDTH), lambda i: (i, 0))
    vec = lambda: pl.BlockSpec((1, RWKV_WIDTH), lambda i: (0, 0))
    return pl.pallas_call(
        _rw_post_kernel,
        grid=(n // tt,),
        in_specs=[row(), row(), row(), row(), row(), vec(), vec(), vec(),
                  pl.BlockSpec((MXU_DIM, MXU_DIM), lambda i: (0, 0))],
        out_specs=row(),
        out_shape=jax.ShapeDtypeStruct((n, RWKV_WIDTH), BF16),
        compiler_params=_cparams(("parallel",)),
        name="rw_post",
    )(y, r, k, v, g, r_k, gn_g, gn_b, ones_bd)


def _out_proj_kernel(yl_ref, yr_ref, h0_ref, wo_ref, g_ref, b_ref, wr_ref, br_ref,
                     h1_ref, h1p_ref, ti_ref, tg_ref, rank_ref, cnt_ref, carry, *, tm):
    @pl.when(pl.program_id(0) == 0)
    def _():
        carry[...] = jnp.zeros_like(carry)

    mix = jnp.dot(jnp.concatenate([yl_ref[...], yr_ref[...]], axis=1), wo_ref[...],
                  preferred_element_type=F32)
    h1 = _layer_norm_rows(DEEPNORM_ALPHA * h0_ref[...] + mix, g_ref[...], b_ref[...])
    h1_ref[...] = h1
    _store_rows(h1p_ref, 0, _pack_halves(h1))

    h_hi, h_lo = _split2(h1)
    w_hi = wr_ref[0]
    w_lo = wr_ref[1]
    logits = (jnp.dot(h_hi, w_hi, preferred_element_type=F32)
              + jnp.dot(h_hi, w_lo, preferred_element_type=F32)
              + jnp.dot(h_lo, w_hi, preferred_element_type=F32)) + br_ref[...]

    lane = lax.broadcasted_iota(jnp.int32, (tm, N_EXPERTS), 1).astype(F32)
    lane4 = lax.broadcasted_iota(jnp.int32, (tm, TOP_K), 1)
    work = logits
    vals, idxs, sels = [], [], []
    for _ in range(TOP_K):
        m = jnp.max(work, axis=-1, keepdims=True)
        idx = jnp.min(jnp.where(work == m, lane, float(N_EXPERTS)), axis=-1, keepdims=True)
        sel = lane == idx
        vals.append(m)
        idxs.append(idx)
        sels.append(sel)
        work = jnp.where(sel, -jnp.inf, work)
    exps = [jnp.exp(vv - vals[0]) for vv in vals]
    denom = exps[0] + exps[1] + exps[2] + exps[3]

    member = jnp.zeros((tm, N_EXPERTS), F32)
    for sel in sels:
        member = member + sel.astype(F32)
    ri = lax.broadcasted_iota(jnp.int32, (tm, tm), 0)
    ci = lax.broadcasted_iota(jnp.int32, (tm, tm), 1)
    before = (ci < ri).astype(BF16)
    rank_full = jnp.dot(before, member.astype(BF16), preferred_element_type=F32) + carry[...]
    carry[...] = carry[...] + jnp.sum(member, axis=0, keepdims=True)
    cnt_ref[...] = carry[...].astype(jnp.int32)

    ti = jnp.zeros((tm, TOP_K), jnp.int32)
    tg = jnp.zeros((tm, TOP_K), F32)
    rk = jnp.zeros((tm, TOP_K), jnp.int32)
    for kq in range(TOP_K):
        rank_k = jnp.sum(jnp.where(sels[kq], rank_full, 0.0), axis=-1, keepdims=True)
        ti = jnp.where(lane4 == kq, idxs[kq].astype(jnp.int32), ti)
        tg = jnp.where(lane4 == kq, exps[kq] / denom, tg)
        rk = jnp.where(lane4 == kq, rank_k.astype(jnp.int32), rk)
    ti_ref[...] = ti
    tg_ref[...] = tg
    rank_ref[...] = rk


def _out_proj(y_lru, y_rw, h0, w_out_bf16, g, b, w_router_split, b_router, tm):
    n = h0.shape[0]
    row = lambda w: pl.BlockSpec((tm, w), lambda i: (i, 0))
    vec = lambda w: pl.BlockSpec((1, w), lambda i: (0, 0))
    return pl.pallas_call(
        functools.partial(_out_proj_kernel, tm=tm),
        grid=(n // tm,),
        in_specs=[
            row(LRU_WIDTH), row(RWKV_WIDTH), row(D_MODEL),
            pl.BlockSpec((D_MODEL, D_MODEL), lambda i: (0, 0)),
            vec(D_MODEL), vec(D_MODEL),
            pl.BlockSpec((2, D_MODEL, N_EXPERTS), lambda i: (0, 0, 0)),
            vec(N_EXPERTS),
        ],
        out_specs=[row(D_MODEL), pl.BlockSpec((tm * ROW_TILES, LANE), lambda i: (i, 0)),
                   row(TOP_K), row(TOP_K), row(TOP_K), vec(N_EXPERTS)],
        out_shape=[
            jax.ShapeDtypeStruct((n, D_MODEL), F32),
            jax.ShapeDtypeStruct((n * ROW_TILES, LANE), U32),
            jax.ShapeDtypeStruct((n, TOP_K), jnp.int32),
            jax.ShapeDtypeStruct((n, TOP_K), F32),
            jax.ShapeDtypeStruct((n, TOP_K), jnp.int32),
            jax.ShapeDtypeStruct((1, N_EXPERTS), jnp.int32),
        ],
        scratch_shapes=[pltpu.VMEM((1, N_EXPERTS), F32)],
        compiler_params=_cparams(("arbitrary",)),
        name="out_proj",
    )(y_lru, y_rw, h0, w_out_bf16, g, b, w_router_split, b_router)


def _scatter_kernel(dest_ref, fill_ref, h_ref, xs_ref, zbuf, sem, zsem, *, tb, tm, nb):
    @pl.when(pl.program_id(0) == 0)
    def _():
        zbuf[...] = jnp.zeros_like(zbuf)

        def fill_copy(start):
            start = pl.multiple_of(start * ROW_TILES, SUBLANE)
            return pltpu.make_async_copy(zbuf, xs_ref.at[pl.ds(start, tm * ROW_TILES), :], zsem)

        def fill(e, carry):
            @pl.when(fill_ref[e] >= 0)
            def _():
                fill_copy(fill_ref[e]).start()
            return carry

        def fill_wait(e, carry):
            @pl.when(fill_ref[e] >= 0)
            def _():
                fill_copy(fill_ref[e]).wait()
            return carry

        def tail(blk, carry):
            fill_copy(blk * tm).start()
            return carry

        def tail_wait(blk, carry):
            fill_copy(blk * tm).wait()
            return carry

        n_used = fill_ref[N_EXPERTS]
        lax.fori_loop(0, N_EXPERTS, fill, 0)
        lax.fori_loop(n_used, nb, tail, 0)
        lax.fori_loop(0, N_EXPERTS, fill_wait, 0)
        lax.fori_loop(n_used, nb, tail_wait, 0)

    base = pl.program_id(0) * (tb * TOP_K)

    def copy(t, kq):
        slot = dest_ref[base + t * TOP_K + kq]
        return pltpu.make_async_copy(_row_tile(h_ref, t), _row_tile(xs_ref, slot), sem)

    def issue(t, carry):
        for kq in range(TOP_K):
            copy(t, kq).start(priority=kq % 2)
        return carry

    lax.fori_loop(0, tb, issue, 0, unroll=4)
    for kq in range(TOP_K):
        pltpu.make_async_copy(h_ref, xs_ref.at[pl.ds(0, tb * ROW_TILES), :], sem).wait()


def _scatter(dest_flat, fill_start, h1p, n_slots, tb, tm):
    n = h1p.shape[0] // ROW_TILES
    nb = n_slots // tm
    grid_spec = pltpu.PrefetchScalarGridSpec(
        num_scalar_prefetch=2,
        grid=(n // tb,),
        in_specs=[pl.BlockSpec((tb * ROW_TILES, LANE), lambda i, d, f: (i, 0))],
        out_specs=pl.BlockSpec(memory_space=pl.ANY),
        scratch_shapes=[pltpu.VMEM((tm * ROW_TILES, LANE), U32), pltpu.SemaphoreType.DMA(()),
                        pltpu.SemaphoreType.DMA(())],
    )
    return pl.pallas_call(
        functools.partial(_scatter_kernel, tb=tb, tm=tm, nb=nb),
        grid_spec=grid_spec,
        out_shape=jax.ShapeDtypeStruct((n_slots * ROW_TILES, LANE), U32),
        compiler_params=_cparams(("arbitrary",)),
        name="scatter",
    )(dest_flat, fill_start, h1p)


def _gmm1_kernel(meta_ref, x_ref, wg_ref, wu_ref, bg_ref, bu_ref, o_ref, wgb, wub, *, nb, tm):
    i = pl.program_id(1)
    e = meta_ref[i]
    prev_e = meta_ref[jnp.maximum(i - 1, 0)]

    @pl.when(jnp.logical_or(i == 0, e != prev_e))
    def _():
        wgb[...] = wg_ref[0].astype(BF16)
        wub[...] = wu_ref[0].astype(BF16)

    @pl.when(i < meta_ref[nb])
    def _():
        x_lo, x_hi = _unpack_halves(_load_rows(x_ref, 0, tm))
        x_lo = x_lo.astype(BF16)
        x_hi = x_hi.astype(BF16)
        gate = (jnp.dot(x_lo, wgb[0:HALF, :], preferred_element_type=F32)
                + jnp.dot(x_hi, wgb[HALF:, :], preferred_element_type=F32) + bg_ref[0])
        up = (jnp.dot(x_lo, wub[0:HALF, :], preferred_element_type=F32)
              + jnp.dot(x_hi, wub[HALF:, :], preferred_element_type=F32) + bu_ref[0])
        gate = jnp.minimum(gate, SWIGLU_LIMIT)
        up = jnp.clip(up, -SWIGLU_LIMIT, SWIGLU_LIMIT)
        act = gate * _sigmoid(SWIGLU_ALPHA * gate) * (up + 1.0)
        o_ref[...] = act.astype(o_ref.dtype)

    @pl.when(i >= meta_ref[nb])
    def _():
        o_ref[...] = jnp.zeros_like(o_ref)


def _gmm1(meta, xs, w1, b1, nb, tm, tn):
    n_slots = xs.shape[0] // ROW_TILES
    nj = D_EXPERT // tn

    def row_blk(j, i, m):
        return jnp.minimum(i, m[nb] - 1)

    grid_spec = pltpu.PrefetchScalarGridSpec(
        num_scalar_prefetch=1,
        grid=(nj, nb),
        in_specs=[
            pl.BlockSpec((tm * ROW_TILES, LANE), lambda j, i, m: (row_blk(j, i, m), 0)),
            pl.BlockSpec((1, D_MODEL, tn), lambda j, i, m: (m[i], 0, j)),
            pl.BlockSpec((1, D_MODEL, tn), lambda j, i, m: (m[i], 0, j + nj)),
            pl.BlockSpec((1, 1, tn), lambda j, i, m: (m[i], 0, j)),
            pl.BlockSpec((1, 1, tn), lambda j, i, m: (m[i], 0, j + nj)),
        ],
        out_specs=pl.BlockSpec((tm, tn), lambda j, i, m: (i, j)),
        scratch_shapes=[pltpu.VMEM((D_MODEL, tn), BF16), pltpu.VMEM((D_MODEL, tn), BF16)],
    )
    return pl.pallas_call(
        functools.partial(_gmm1_kernel, nb=nb, tm=tm),
        grid_spec=grid_spec,
        out_shape=jax.ShapeDtypeStruct((n_slots, D_EXPERT), BF16),
        compiler_params=_cparams(("arbitrary", "arbitrary")),
        name="gmm1",
    )(meta, xs, w1, w1, b1, b1)


def _gmm2_kernel(meta_ref, a_ref, w_ref, b_ref, o_ref, wb, *, nb):
    i = pl.program_id(1)
    e = meta_ref[i]
    prev_e = meta_ref[jnp.maximum(i - 1, 0)]

    @pl.when(jnp.logical_or(i == 0, e != prev_e))
    def _():
        wb[...] = w_ref[0].astype(BF16)

    @pl.when(i < meta_ref[nb])
    def _():
        out = jnp.dot(a_ref[...], wb[...], preferred_element_type=F32) + b_ref[0]
        _store_rows(o_ref, 0, _pack_halves(out))

    @pl.when(i >= meta_ref[nb])
    def _():
        o_ref[...] = jnp.zeros_like(o_ref)


def _gmm2(meta, act, w2, b2, nb, tm):
    n_slots = act.shape[0]

    def row_blk(j, i, m):
        return jnp.minimum(i, m[nb] - 1)

    grid_spec = pltpu.PrefetchScalarGridSpec(
        num_scalar_prefetch=1,
        grid=(1, nb),
        in_specs=[
            pl.BlockSpec((tm, D_EXPERT), lambda j, i, m: (row_blk(j, i, m), 0)),
            pl.BlockSpec((1, D_EXPERT, D_MODEL), lambda j, i, m: (m[i], 0, 0)),
            pl.BlockSpec((1, 1, D_MODEL), lambda j, i, m: (m[i], 0, 0)),
        ],
        out_specs=pl.BlockSpec((tm * ROW_TILES, LANE), lambda j, i, m: (i, 0)),
        scratch_shapes=[pltpu.VMEM((D_EXPERT, D_MODEL), BF16)],
    )
    return pl.pallas_call(
        functools.partial(_gmm2_kernel, nb=nb),
        grid_spec=grid_spec,
        out_shape=jax.ShapeDtypeStruct((n_slots * ROW_TILES, LANE), U32),
        compiler_params=_cparams(("arbitrary", "arbitrary")),
        name="gmm2",
    )(meta, act, w2, b2)


def _combine_kernel(dest_ref, h1_ref, tg_ref, g_ref, b_ref, ys_ref, o_ref, buf, sems, *, tb, n_steps):
    i = pl.program_id(0)

    def issue(step, half):
        base = step * (tb * TOP_K)

        def body(t, carry):
            for kq in range(TOP_K):
                slot = dest_ref[base + t * TOP_K + kq]
                pltpu.make_async_copy(_row_tile(ys_ref, slot), _row_tile(buf.at[half, kq], t),
                                      sems.at[half]).start(priority=kq % 2)
            return carry

        lax.fori_loop(0, tb, body, 0, unroll=4)

    @pl.when(i == 0)
    def _():
        issue(0, 0)

    @pl.when(i + 1 < n_steps)
    def _():
        issue(i + 1, (i + 1) % 2)

    half = i % 2
    for kq in range(TOP_K):
        pltpu.make_async_copy(ys_ref.at[pl.ds(0, tb * ROW_TILES), :], buf.at[half, kq],
                              sems.at[half]).wait()

    tg = tg_ref[...]
    y = DEEPNORM_ALPHA * h1_ref[...]
    for kq in range(TOP_K):
        lo, hi = _unpack_halves(_load_rows(buf.at[half, kq], 0, tb))
        y = y + tg[:, kq:kq + 1] * jnp.concatenate([lo, hi], axis=1)
    o_ref[...] = _layer_norm_rows(y, g_ref[...], b_ref[...])


def _combine(dest_flat, h1, tg, g, b, ys, tb):
    n = h1.shape[0]
    grid_spec = pltpu.PrefetchScalarGridSpec(
        num_scalar_prefetch=1,
        grid=(n // tb,),
        in_specs=[
            pl.BlockSpec((tb, D_MODEL), lambda i, d: (i, 0)),
            pl.BlockSpec((tb, TOP_K), lambda i, d: (i, 0)),
            pl.BlockSpec((1, D_MODEL), lambda i, d: (0, 0)),
            pl.BlockSpec((1, D_MODEL), lambda i, d: (0, 0)),
            pl.BlockSpec(memory_space=pl.ANY),
        ],
        out_specs=pl.BlockSpec((tb, D_MODEL), lambda i, d: (i, 0)),
        scratch_shapes=[pltpu.VMEM((2, TOP_K, tb * ROW_TILES, LANE), U32),
                        pltpu.SemaphoreType.DMA((2,))],
    )
    return pl.pallas_call(
        functools.partial(_combine_kernel, tb=tb, n_steps=n // tb),
        grid_spec=grid_spec,
        out_shape=jax.ShapeDtypeStruct((n, D_MODEL), F32),
        compiler_params=_cparams(("arbitrary",)),
        name="combine",
    )(dest_flat, h1, tg, g, b, ys)


def _block_diag(w, group):
    nb = w.shape[0] // group
    w = w.reshape(nb, group, LRU_BLOCK_W, LRU_BLOCK_W)
    eye = jnp.eye(group, dtype=w.dtype)
    out = jnp.einsum('gajk,ab->gajbk', w, eye)
    return out.reshape(nb, group * LRU_BLOCK_W, group * LRU_BLOCK_W)


def _pad_rows(w, rows):
    return jnp.pad(w, ((0, rows - w.shape[0]), (0, 0)))


def _layer(h_in_x, l, prm, bsz, t):
    n = bsz * t
    row = lambda a: a.reshape(1, -1)

    w_in = prm['w_in'][l]
    o_rw = 2 * LRU_WIDTH
    o_l = o_rw + 3 * RWKV_WIDTH
    pad_c = lambda w: jnp.pad(w, ((0, 0), (0, LORA_PAD - w.shape[1])))
    w_in_p = jnp.concatenate([
        w_in[:, :o_l],
        pad_c(w_in[:, o_l:o_l + DECAY_LORA]),
        pad_c(w_in[:, o_l + DECAY_LORA:o_l + DECAY_LORA + AAA_LORA]),
        w_in[:, o_l + DECAY_LORA + AAA_LORA:],
    ], axis=1).astype(BF16)
    mu = prm['shift_mu'][l]
    pad_v = lambda v: jnp.pad(v, (0, LORA_PAD - v.shape[0]))
    mu_l = mu[3 * RWKV_WIDTH:]
    mu_a = jnp.concatenate([pad_v(mu_l[:DECAY_LORA]), pad_v(mu_l[DECAY_LORA:DECAY_LORA + AAA_LORA]),
                            mu_l[DECAY_LORA + AAA_LORA:]])

    tm_in = min(1024, n)
    h0, p = _in_proj(h_in_x, row(prm['ln_in_g']), row(prm['ln_in_b']), w_in_p, tm_in, 512)

    tt = min(512, t)
    group = MXU_DIM // LRU_BLOCK_W
    y_lru = _lru(p, prm['conv_w'][l], row(prm['conv_b'][l]),
                 _block_diag(prm['w_rgate'][l], group).astype(BF16), row(prm['b_rgate'][l]),
                 _block_diag(prm['w_igate'][l], group).astype(BF16), row(prm['b_igate'][l]),
                 row(prm['lru_lambda'][l]), bsz, t, tt)

    head_id = jnp.arange(MXU_DIM) // HEAD_SIZE
    ones_bd = (head_id[:, None] == head_id[None, :]).astype(BF16)
    tt_rw = min(256, t)
    r, k, v, kk, bvec, cl, g = _rw_prep(
        p, row(mu[:RWKV_WIDTH]), row(mu[RWKV_WIDTH:2 * RWKV_WIDTH]),
        row(mu[2 * RWKV_WIDTH:3 * RWKV_WIDTH]), row(mu_a),
        row(prm['w0'][l]), _pad_rows(prm['rw_decay_up'][l], LORA_PAD).astype(BF16),
        row(prm['a0'][l]), _pad_rows(prm['rw_aaa_up'][l], LORA_PAD).astype(BF16),
        prm['rw_gate_up'][l].astype(BF16), row(prm['k_k'][l]), row(prm['k_a'][l]),
        ones_bd, bsz, t, tt_rw)
    y = _rw_chunk(r, k, v, kk, bvec, cl, bsz, t, 4 if bsz % 4 == 0 else 1)
    y_rw = _rw_post(y, r, k, v, g, row(prm['r_k'][l]), row(prm['gn_g'][l]), row(prm['gn_b'][l]),
                    ones_bd, min(512, n))

    w_r = prm['w_router'][l]
    w_r_hi = w_r.astype(BF16)
    w_r_lo = (w_r - w_r_hi.astype(F32)).astype(BF16)
    tm_out = min(512, n)
    h1, h1p, top_i, top_g, rank, counts = _out_proj(
        y_lru, y_rw, h0, prm['w_out'][l].astype(BF16), row(prm['ln1_g'][l]), row(prm['ln1_b'][l]),
        jnp.stack([w_r_hi, w_r_lo]), row(prm['b_router'][l]), tm_out)

    tm = 512
    counts = counts.reshape(N_EXPERTS)
    padded = ((counts + tm - 1) // tm) * tm
    pad_end = jnp.cumsum(padded)
    pad_start = pad_end - padded
    dest = (pad_start[top_i] + rank).reshape(-1).astype(jnp.int32)
    nb = (n * TOP_K) // tm + N_EXPERTS
    n_slots = nb * tm
    n_used = (pad_end[-1] // tm).astype(jnp.int32)
    blk = jnp.minimum(jnp.arange(nb, dtype=jnp.int32), n_used - 1)
    blk_e = jnp.sum((pad_end[None, :] <= (blk * tm)[:, None]).astype(jnp.int32), axis=1)
    blk_e = jnp.minimum(blk_e, N_EXPERTS - 1)
    meta = jnp.concatenate([blk_e.astype(jnp.int32), n_used.reshape(1)])

    fill_start = jnp.concatenate([jnp.where(padded > 0, pad_end - tm, -1).astype(jnp.int32),
                                  n_used.reshape(1)])
    tb = min(128, n)
    xs = _scatter(dest, fill_start, h1p, n_slots, min(256, n), tm)
    act = _gmm1(meta, xs, prm['w_exp1'][l], prm['b_exp1'][l].reshape(N_EXPERTS, 1, 2 * D_EXPERT),
                nb, tm, 1024)
    ys = _gmm2(meta, act, prm['w_exp2'][l], prm['b_exp2'][l].reshape(N_EXPERTS, 1, D_MODEL),
               nb, tm)
    return _combine(dest, h1, top_g, row(prm['ln2_g'][l]), row(prm['ln2_b'][l]),
                    ys, tb)


def kernel(x, ln_in_g, ln_in_b, w_in, conv_w, conv_b, w_rgate, b_rgate, w_igate, b_igate, lru_lambda, shift_mu, w0, rw_decay_up, a0, rw_aaa_up, rw_gate_up, k_k, k_a, r_k, gn_g, gn_b, w_out, ln1_g, ln1_b, w_router, b_router, w_exp1, b_exp1, w_exp2, b_exp2, ln2_g, ln2_b):
    bsz, t, d = x.shape
    prm = dict(ln_in_g=ln_in_g, ln_in_b=ln_in_b, w_in=w_in, conv_w=conv_w, conv_b=conv_b,
               w_rgate=w_rgate, b_rgate=b_rgate, w_igate=w_igate, b_igate=b_igate,
               lru_lambda=lru_lambda, shift_mu=shift_mu, w0=w0, rw_decay_up=rw_decay_up, a0=a0,
               rw_aaa_up=rw_aaa_up, rw_gate_up=rw_gate_up, k_k=k_k, k_a=k_a, r_k=r_k, gn_g=gn_g,
               gn_b=gn_b, w_out=w_out, ln1_g=ln1_g, ln1_b=ln1_b, w_router=w_router,
               b_router=b_router, w_exp1=w_exp1, b_exp1=b_exp1, w_exp2=w_exp2, b_exp2=b_exp2,
               ln2_g=ln2_g, ln2_b=ln2_b)
    out = _layer(x.reshape(bsz * t, d), 0, prm, bsz, t)
    return out.reshape(bsz, t, d)
```

```python
import functools

import jax
import jax.numpy as jnp
from jax import lax
from jax.experimental import pallas as pl
from jax.experimental.pallas import tpu as pltpu

D_MODEL = 2048
DEPTH = 1
CHUNK = 64
LRU_WIDTH = 1024
LRU_BLOCKS = 16
LRU_BLOCK_W = LRU_WIDTH // LRU_BLOCKS
CONV_WIDTH = 4
RG_C = 8.0
RWKV_WIDTH = D_MODEL - LRU_WIDTH
HEAD_SIZE = 64
RWKV_HEADS = RWKV_WIDTH // HEAD_SIZE
DECAY_LORA = 96
AAA_LORA = 96
GATE_LORA = 256
N_EXPERTS = 32
TOP_K = 4
D_EXPERT = D_MODEL
SWIGLU_LIMIT = 7.0
SWIGLU_ALPHA = 1.702
LN_EPS = 1e-5
GN_EPS = HEAD_SIZE * 1e-5
DEEPNORM_ALPHA = (2.0 * DEPTH) ** 0.25

LANE = 128
SUBLANE = 8
MXU_DIM = 256
LORA_PAD = 128
LORA_W = 2 * LORA_PAD + GATE_LORA
P_WIDTH = 2 * LRU_WIDTH + 3 * RWKV_WIDTH + LORA_W
VMEM_LIMIT = 56 * 1024 * 1024

F32 = jnp.float32
BF16 = jnp.bfloat16


def _cparams(sem):
    return pltpu.CompilerParams(dimension_semantics=sem, vmem_limit_bytes=VMEM_LIMIT)


def _layer_norm_rows(x, g, b):
    mu = jnp.mean(x, axis=-1, keepdims=True)
    xc = x - mu
    var = jnp.mean(xc * xc, axis=-1, keepdims=True)
    return xc * lax.rsqrt(var + LN_EPS) * g + b


def _softplus(z):
    return jnp.maximum(z, 0.0) + jnp.log1p(jnp.exp(-jnp.abs(z)))


def _sigmoid(z):
    return 1.0 / (1.0 + jnp.exp(-z))


def _split2(x):
    hi = x.astype(BF16)
    lo = (x - hi.astype(F32)).astype(BF16)
    return hi, lo


HALF = D_MODEL // 2
U32 = jnp.uint32
RANK_SPAN = 1 << 20


def _pack_halves(x):
    lo = lax.bitcast_convert_type(x[:, :HALF].astype(BF16).astype(F32), U32) >> 16
    hi = lax.bitcast_convert_type(x[:, HALF:].astype(BF16).astype(F32), U32) & jnp.uint32(0xFFFF0000)
    return hi | lo


def _unpack_halves(w):
    lo = lax.bitcast_convert_type(w << 16, F32)
    hi = lax.bitcast_convert_type(w & jnp.uint32(0xFFFF0000), F32)
    return lo, hi


ROW_TILES = HALF // LANE
assert ROW_TILES == SUBLANE


def _store_rows(ref, row0, packed):
    m = packed.shape[0]
    for c in range(ROW_TILES):
        ref[pl.ds(row0 * ROW_TILES + c, m, stride=ROW_TILES), :] = packed[:, c * LANE:(c + 1) * LANE]


def _load_rows(ref, row0, m):
    return jnp.concatenate(
        [ref[pl.ds(row0 * ROW_TILES + c, m, stride=ROW_TILES), :] for c in range(ROW_TILES)], axis=1)


def _row_tile(ref, row):
    return ref.at[pl.ds(pl.multiple_of(row * ROW_TILES, ROW_TILES), ROW_TILES), :]


def _head_sum(x, ones_ref):
    hi, lo = _split2(x)
    ones = ones_ref[...]
    parts = []
    for g in range(x.shape[1] // MXU_DIM):
        sl = slice(g * MXU_DIM, (g + 1) * MXU_DIM)
        parts.append(jnp.dot(hi[:, sl], ones, preferred_element_type=F32)
                     + jnp.dot(lo[:, sl], ones, preferred_element_type=F32))
    return jnp.concatenate(parts, axis=1)


def _in_proj_kernel(x_ref, g_ref, b_ref, w_ref, h_ref, p_ref, hb_ref):
    @pl.when(pl.program_id(1) == 0)
    def _():
        h = _layer_norm_rows(x_ref[...], g_ref[...], b_ref[...])
        h_ref[...] = h
        hb_ref[...] = h.astype(BF16)

    p_ref[...] = jnp.dot(hb_ref[...], w_ref[...], preferred_element_type=F32)


def _in_proj(x2, g, b, w_bf16, tm, tn):
    n = x2.shape[0]
    return pl.pallas_call(
        _in_proj_kernel,
        grid=(n // tm, P_WIDTH // tn),
        in_specs=[
            pl.BlockSpec((tm, D_MODEL), lambda i, j: (i, 0)),
            pl.BlockSpec((1, D_MODEL), lambda i, j: (0, 0)),
            pl.BlockSpec((1, D_MODEL), lambda i, j: (0, 0)),
            pl.BlockSpec((D_MODEL, tn), lambda i, j: (0, j)),
        ],
        out_specs=[
            pl.BlockSpec((tm, D_MODEL), lambda i, j: (i, 0)),
            pl.BlockSpec((tm, tn), lambda i, j: (i, j)),
        ],
        out_shape=[
            jax.ShapeDtypeStruct((n, D_MODEL), F32),
            jax.ShapeDtypeStruct((n, P_WIDTH), F32),
        ],
        scratch_shapes=[pltpu.VMEM((tm, D_MODEL), BF16)],
        compiler_params=_cparams(("parallel", "arbitrary")),
        name="in_proj",
    )(x2, g, b, w_bf16)


def _lru_kernel(u_ref, gi_ref, cw_ref, cb_ref, wr_ref, br_ref, wi_ref, bi_ref,
                lam_ref, o_ref, ubuf, a_s, b_s, carry, *, tt):
    first = pl.program_id(1) == 0

    @pl.when(first)
    def _():
        ubuf[0:SUBLANE, :] = jnp.zeros((SUBLANE, LRU_WIDTH), F32)
        carry[...] = jnp.zeros_like(carry)

    @pl.when(jnp.logical_not(first))
    def _():
        ubuf[0:SUBLANE, :] = ubuf[tt:tt + SUBLANE, :]

    ubuf[SUBLANE:, :] = u_ref[...]
    uc = cb_ref[...]
    for i in range(CONV_WIDTH):
        off = SUBLANE - (CONV_WIDTH - 1) + i
        uc = uc + cw_ref[i:i + 1, :] * ubuf[off:off + tt, :]

    ucb = uc.astype(BF16)
    n_grp = LRU_WIDTH // MXU_DIM

    def gate(w_ref, bias_ref):
        parts = [jnp.dot(ucb[:, g * MXU_DIM:(g + 1) * MXU_DIM], w_ref[g],
                         preferred_element_type=F32) for g in range(n_grp)]
        return _sigmoid(jnp.concatenate(parts, axis=1) + bias_ref[...])

    r_gate = gate(wr_ref, br_ref)
    i_gate = gate(wi_ref, bi_ref)
    log_a = (-RG_C * r_gate) * _softplus(-lam_ref[...])
    a_s[...] = jnp.exp(log_a)
    th = jnp.tanh(log_a)
    b_s[...] = jnp.sqrt(-2.0 * th / (1.0 - th)) * (i_gate * uc)

    row = lax.broadcasted_iota(jnp.int32, (SUBLANE, LRU_WIDTH), 0)

    def group(gidx, c):
        off = pl.multiple_of(gidx * SUBLANE, SUBLANE)
        a = a_s[pl.ds(off, SUBLANE), :]
        b = b_s[pl.ds(off, SUBLANE), :]
        for d in (1, 2, 4):
            keep = row >= d
            a_sh = pltpu.roll(a, d, axis=0)
            b_sh = pltpu.roll(b, d, axis=0)
            b = jnp.where(keep, a * b_sh + b, b)
            a = jnp.where(keep, a * a_sh, a)
        h = a * c + b
        b_s[pl.ds(off, SUBLANE), :] = h
        return h[SUBLANE - 1:SUBLANE, :]

    carry[...] = lax.fori_loop(0, tt // SUBLANE, group, carry[...])
    o_ref[...] = (b_s[...] * jax.nn.gelu(gi_ref[...])).astype(o_ref.dtype)


def _lru(p, conv_w, conv_b, wr_bd, br, wi_bd, bi, lam, bsz, t, tt):
    n = bsz * t
    nt = t // tt
    vec = lambda: pl.BlockSpec((1, LRU_WIDTH), lambda b, i: (0, 0))
    wspec = lambda: pl.BlockSpec((LRU_WIDTH // MXU_DIM, MXU_DIM, MXU_DIM), lambda b, i: (0, 0, 0))
    return pl.pallas_call(
        functools.partial(_lru_kernel, tt=tt),
        grid=(bsz, nt),
        in_specs=[
            pl.BlockSpec((tt, LRU_WIDTH), lambda b, i: (b * nt + i, 0)),
            pl.BlockSpec((tt, LRU_WIDTH), lambda b, i: (b * nt + i, 1)),
            pl.BlockSpec((CONV_WIDTH, LRU_WIDTH), lambda b, i: (0, 0)),
            vec(), wspec(), vec(), wspec(), vec(), vec(),
        ],
        out_specs=pl.BlockSpec((tt, LRU_WIDTH), lambda b, i: (b * nt + i, 0)),
        out_shape=jax.ShapeDtypeStruct((n, LRU_WIDTH), BF16),
        scratch_shapes=[
            pltpu.VMEM((tt + SUBLANE, LRU_WIDTH), F32),
            pltpu.VMEM((tt, LRU_WIDTH), F32),
            pltpu.VMEM((tt, LRU_WIDTH), F32),
            pltpu.VMEM((1, LRU_WIDTH), F32),
        ],
        compiler_params=_cparams(("parallel", "arbitrary")),
        name="lru",
    )(p, p, conv_w, conv_b, wr_bd, br, wi_bd, bi, lam)


def _rw_prep_kernel(pr_ref, pk_ref, pv_ref, pa_ref, mur_ref, muk_ref, muv_ref, mua_ref,
                    w0_ref, wdec_ref, a0_ref, waaa_ref, wgate_ref, kk_ref, ka_ref, ones_ref,
                    r_out, k_out, v_out, kkn_out, b_out, cl_out, g_out,
                    prev_r, prev_k, prev_v, prev_a, *, tt):
    first = pl.program_id(1) == 0

    @pl.when(first)
    def _():
        for ref in (prev_r, prev_k, prev_v, prev_a):
            ref[...] = jnp.zeros_like(ref)

    def shift(x_ref, prev_ref, mu_ref):
        x = x_ref[...]
        row = lax.broadcasted_iota(jnp.int32, x.shape, 0)
        prev = jnp.where(row == 0, prev_ref[...], pltpu.roll(x, 1, axis=0))
        prev_ref[...] = x[tt - 1:tt, :]
        return x + (prev - x) * mu_ref[...]

    r = shift(pr_ref, prev_r, mur_ref)
    k = shift(pk_ref, prev_k, muk_ref)
    v = shift(pv_ref, prev_v, muv_ref)
    ad = shift(pa_ref, prev_a, mua_ref)

    wd = jnp.tanh(ad[:, 0:LORA_PAD]).astype(BF16)
    aa = ad[:, LORA_PAD:2 * LORA_PAD].astype(BF16)
    gd = _sigmoid(ad[:, 2 * LORA_PAD:]).astype(BF16)
    w_pre = w0_ref[...] + jnp.dot(wd, wdec_ref[...], preferred_element_type=F32)
    w_log = -_softplus(-w_pre) - 0.5
    a = _sigmoid(a0_ref[...] + jnp.dot(aa, waaa_ref[...], preferred_element_type=F32))
    g = jnp.dot(gd, wgate_ref[...], preferred_element_type=F32)

    kk = k * kk_ref[...]
    norm = jnp.sqrt(_head_sum(kk * kk, ones_ref))
    kk = kk / jnp.maximum(norm, 1e-12)

    lw = -jnp.exp(w_log)
    ri = lax.broadcasted_iota(jnp.int32, (tt, tt), 0)
    ci = lax.broadcasted_iota(jnp.int32, (tt, tt), 1)
    tri = jnp.logical_and(ri // CHUNK == ci // CHUNK, ri >= ci).astype(BF16)
    lw_hi = lw.astype(BF16)
    rem = lw - lw_hi.astype(F32)
    lw_mid = rem.astype(BF16)
    lw_lo = (rem - lw_mid.astype(F32)).astype(BF16)
    cl = (jnp.dot(tri, lw_hi, preferred_element_type=F32)
          + jnp.dot(tri, lw_mid, preferred_element_type=F32)
          + jnp.dot(tri, lw_lo, preferred_element_type=F32))

    r_out[...] = r
    k_out[...] = k * (1.0 + (a - 1.0) * ka_ref[...])
    v_out[...] = v
    kkn_out[...] = kk
    b_out[...] = kk * a
    cl_out[...] = cl
    g_out[...] = g


def _rw_prep(p, mu_r, mu_k, mu_v, mu_a, w0, wdec, a0, waaa, wgate, k_k, k_a, ones_bd, bsz, t, tt):
    n = bsz * t
    nt = t // tt
    cb = 2 * LRU_WIDTH // RWKV_WIDTH
    row = lambda c: pl.BlockSpec((tt, RWKV_WIDTH), lambda b, i: (b * nt + i, c))
    vec = lambda w: pl.BlockSpec((1, w), lambda b, i: (0, 0))
    full = lambda s: pl.BlockSpec(s, lambda b, i: (0, 0))
    lora_cb = (2 * LRU_WIDTH + 3 * RWKV_WIDTH) // LORA_W
    out = jax.ShapeDtypeStruct((n, RWKV_WIDTH), F32)
    return pl.pallas_call(
        functools.partial(_rw_prep_kernel, tt=tt),
        grid=(bsz, nt),
        in_specs=[
            row(cb), row(cb + 1), row(cb + 2),
            pl.BlockSpec((tt, LORA_W), lambda b, i: (b * nt + i, lora_cb)),
            vec(RWKV_WIDTH), vec(RWKV_WIDTH), vec(RWKV_WIDTH), vec(LORA_W),
            vec(RWKV_WIDTH), full((LORA_PAD, RWKV_WIDTH)),
            vec(RWKV_WIDTH), full((LORA_PAD, RWKV_WIDTH)),
            full((GATE_LORA, RWKV_WIDTH)),
            vec(RWKV_WIDTH), vec(RWKV_WIDTH),
            full((MXU_DIM, MXU_DIM)),
        ],
        out_specs=[pl.BlockSpec((tt, RWKV_WIDTH), lambda b, i: (b * nt + i, 0))] * 7,
        out_shape=[out] * 7,
        scratch_shapes=[pltpu.VMEM((1, RWKV_WIDTH), F32)] * 3 + [pltpu.VMEM((1, LORA_W), F32)],
        compiler_params=_cparams(("parallel", "arbitrary")),
        name="rw_prep",
    )(p, p, p, p, mu_r, mu_k, mu_v, mu_a, w0, wdec, a0, waaa, wgate, k_k, k_a, ones_bd)


def _rw_chunk_kernel(r_ref, k_ref, v_ref, kk_ref, b_ref, cl_ref, y_ref, *s_refs):
    c = CHUNK
    hs = HEAD_SIZE
    n_seq = r_ref.shape[0]

    @pl.when(pl.program_id(1) == 0)
    def _():
        for s_ref in s_refs:
            s_ref[...] = jnp.zeros_like(s_ref)

    r2 = lax.broadcasted_iota(jnp.int32, (2 * c, 2 * c), 0)
    c2 = lax.broadcasted_iota(jnp.int32, (2 * c, 2 * c), 1)
    tq = jnp.where(r2 >= c, r2 - c, r2)
    tk = jnp.where(c2 >= c, c2 - c, c2)
    mask = tk < tq + jnp.where(r2 >= c, 1, 0)
    zeros_cc = jnp.zeros((c, hs), BF16)

    dn_t = (((1,), (1,)), ((), ()))
    dn_l = (((0,), (0,)), ((), ()))

    row_w = lax.broadcasted_iota(jnp.int32, (c, RWKV_WIDTH), 0)

    def scaled_operands(q):
        cl = cl_ref[q]
        cl_ex = jnp.where(row_w == 0, 0.0, pltpu.roll(cl, 1, axis=0))
        cl_last = cl[c - 1:c, :]
        g_inv = jnp.exp(-cl)
        g_dec = jnp.exp(cl_last - cl)
        bv = b_ref[q]
        kx = k_ref[q]
        rg = r_ref[q] * jnp.exp(cl)
        return dict(
            g_last=jnp.exp(cl_last),
            v=v_ref[q].astype(BF16),
            rg=rg,
            lhs_a=(-kk_ref[q] * jnp.exp(cl_ex)).astype(BF16),
            lhs_r=rg.astype(BF16),
            rhs_b=(bv * g_inv).astype(BF16),
            rhs_k=(kx * g_inv).astype(BF16),
            dec_b=(bv * g_dec).astype(BF16),
            dec_k=(kx * g_dec).astype(BF16),
        )

    ops = [scaled_operands(q) for q in range(n_seq)]

    chains = [(q, h) for q in range(n_seq) for h in range(RWKV_HEADS)]
    sl_of = lambda h: slice(h * hs, (h + 1) * hs)
    s0s = [s_refs[q * RWKV_HEADS + h][...] for q, h in chains]
    aas = []
    for q, h in chains:
        o, sl = ops[q], sl_of(h)
        lhs = jnp.concatenate([o['lhs_a'][:, sl], o['lhs_r'][:, sl]], axis=0)
        rhs = jnp.concatenate([o['rhs_b'][:, sl], o['rhs_k'][:, sl]], axis=0)
        aa = lax.dot_general(lhs, rhs, dn_t, preferred_element_type=F32)
        aas.append(jnp.where(mask, aa, 0.0))
    v_hs = [ops[q]['v'][:, sl_of(h)] for q, h in chains]
    a_rs = [aa[c:, :].astype(BF16) for aa in aas]
    ps = [aa[:c, :c] for aa in aas]
    xs = []
    for i, (q, h) in enumerate(chains):
        akv = jnp.dot(aas[i][:c, c:].astype(BF16), v_hs[i], preferred_element_type=F32)
        xs.append(jnp.concatenate([ops[q]['lhs_a'][:, sl_of(h)].astype(F32), akv], axis=1))
    n_sq = 6
    for it in range(n_sq):
        pbs = [p.astype(BF16) for p in ps]
        xs = [x + jnp.dot(pb, x.astype(BF16), preferred_element_type=F32)
              for x, pb in zip(xs, pbs)]
        if it + 1 < n_sq:
            ps = [jnp.dot(pb, pb, preferred_element_type=F32) for pb in pbs]
    for i, (q, h) in enumerate(chains):
        o, sl = ops[q], sl_of(h)
        rhs2 = jnp.concatenate(
            [xs[i].astype(BF16), jnp.concatenate([zeros_cc, v_hs[i]], axis=1)], axis=0)
        qy = jnp.dot(a_rs[i], rhs2, preferred_element_type=F32)
        dec = jnp.concatenate([o['dec_b'][:, sl], o['dec_k'][:, sl]], axis=0)
        mnt = lax.dot_general(rhs2, dec, dn_l, preferred_element_type=F32)
        s0b = s0s[i].astype(BF16)
        qq = (o['rg'][:, sl] + qy[:, :hs]).astype(BF16)
        y_ref[q, :, sl] = lax.dot_general(qq, s0b, dn_t, preferred_element_type=F32) + qy[:, hs:]
        s_refs[i][...] = (s0s[i] * o['g_last'][:, sl]
                          + jnp.dot(s0b, mnt[:hs, :].astype(BF16), preferred_element_type=F32)
                          + mnt[hs:, :])


def _rw_chunk(r, k, v, kk, b, cl, bsz, t, n_seq):
    nc = t // CHUNK
    shape3 = (bsz, t, RWKV_WIDTH)
    spec = lambda: pl.BlockSpec((n_seq, CHUNK, RWKV_WIDTH), lambda bb, i: (bb, i, 0))
    y = pl.pallas_call(
        _rw_chunk_kernel,
        grid=(bsz // n_seq, nc),
        in_specs=[spec() for _ in range(6)],
        out_specs=spec(),
        out_shape=jax.ShapeDtypeStruct(shape3, F32),
        scratch_shapes=[pltpu.VMEM((HEAD_SIZE, HEAD_SIZE), F32)
                        for _ in range(n_seq * RWKV_HEADS)],
        compiler_params=_cparams(("parallel", "arbitrary")),
        name="rw_chunk",
    )(*[a.reshape(shape3) for a in (r, k, v, kk, b, cl)])
    return y.reshape(bsz * t, RWKV_WIDTH)


def _rw_post_kernel(y_ref, r_ref, k_ref, v_ref, g_ref, rk_ref, gg_ref, gb_ref, ones_ref, o_ref):
    y = y_ref[...]
    inv = 1.0 / HEAD_SIZE
    mu = _head_sum(y, ones_ref) * inv
    yc = y - mu
    var = _head_sum(yc * yc, ones_ref) * inv
    yn = yc * lax.rsqrt(var + GN_EPS) * gg_ref[...] + gb_ref[...]
    bonus = _head_sum(r_ref[...] * k_ref[...] * rk_ref[...], ones_ref) * v_ref[...]
    o_ref[...] = ((yn + bonus) * g_ref[...]).astype(o_ref.dtype)


def _rw_post(y, r, k, v, g, r_k, gn_g, gn_b, ones_bd, tt):
    n = y.shape[0]
    row = lambda: pl.BlockSpec((tt, RWKV_WIDTH), lambda i: (i, 0))
    vec = lambda: pl.BlockSpec((1, RWKV_WIDTH), lambda i: (0, 0))
    return pl.pallas_call(
        _rw_post_kernel,
        grid=(n // tt,),
        in_specs=[row(), row(), row(), row(), row(), vec(), vec(), vec(),
                  pl.BlockSpec((MXU_DIM, MXU_DIM), lambda i: (0, 0))],
        out_specs=row(),
        out_shape=jax.ShapeDtypeStruct((n, RWKV_WIDTH), BF16),
        compiler_params=_cparams(("parallel",)),
        name="rw_post",
    )(y, r, k, v, g, r_k, gn_g, gn_b, ones_bd)


def _out_proj_kernel(yl_ref, yr_ref, h0_ref, wo_ref, g_ref, b_ref, wr_ref, br_ref,
                     h1_ref, h1p_ref, code_ref, tg_ref, cnt_ref, carry, *, tm):
    @pl.when(pl.program_id(0) == 0)
    def _():
        carry[...] = jnp.zeros_like(carry)

    mix = jnp.dot(jnp.concatenate([yl_ref[...], yr_ref[...]], axis=1), wo_ref[...],
                  preferred_element_type=F32)
    h1 = _layer_norm_rows(DEEPNORM_ALPHA * h0_ref[...] + mix, g_ref[...], b_ref[...])
    h1_ref[...] = h1
    _store_rows(h1p_ref, 0, _pack_halves(h1))

    h_hi, h_lo = _split2(h1)
    both = jnp.dot(h_hi, wr_ref[...], preferred_element_type=F32)
    logits = (both[:, :N_EXPERTS] + both[:, N_EXPERTS:]
              + jnp.dot(h_lo, wr_ref[:, 0:N_EXPERTS], preferred_element_type=F32)) + br_ref[...]

    lane = lax.broadcasted_iota(jnp.int32, (tm, N_EXPERTS), 1).astype(F32)
    lane4 = lax.broadcasted_iota(jnp.int32, (tm, TOP_K), 1)
    work = logits
    vals, idxs, sels = [], [], []
    for _ in range(TOP_K):
        m = jnp.max(work, axis=-1, keepdims=True)
        idx = jnp.min(jnp.where(work == m, lane, float(N_EXPERTS)), axis=-1, keepdims=True)
        sel = lane == idx
        vals.append(m)
        idxs.append(idx)
        sels.append(sel)
        work = jnp.where(sel, -jnp.inf, work)
    exps = [jnp.exp(vv - vals[0]) for vv in vals]
    denom = exps[0] + exps[1] + exps[2] + exps[3]

    member = jnp.zeros((tm, N_EXPERTS), F32)
    for sel in sels:
        member = member + sel.astype(F32)
    ri = lax.broadcasted_iota(jnp.int32, (tm, tm), 0)
    ci = lax.broadcasted_iota(jnp.int32, (tm, tm), 1)
    before = (ci < ri).astype(BF16)
    rank_full = jnp.dot(before, member.astype(BF16), preferred_element_type=F32) + carry[...]
    carry[...] = carry[...] + jnp.sum(member, axis=0, keepdims=True)
    cnt_ref[...] = carry[...].astype(jnp.int32)

    code = jnp.zeros((tm, TOP_K), jnp.int32)
    tg = jnp.zeros((tm, TOP_K), F32)
    for kq in range(TOP_K):
        rank_k = jnp.sum(jnp.where(sels[kq], rank_full, 0.0), axis=-1, keepdims=True)
        code_k = idxs[kq].astype(jnp.int32) * RANK_SPAN + rank_k.astype(jnp.int32)
        code = jnp.where(lane4 == kq, code_k, code)
        tg = jnp.where(lane4 == kq, exps[kq] / denom, tg)
    code_ref[...] = code
    tg_ref[...] = tg


def _out_proj(y_lru, y_rw, h0, w_out_bf16, g, b, w_router_split, b_router, tm):
    n = h0.shape[0]
    row = lambda w: pl.BlockSpec((tm, w), lambda i: (i, 0))
    vec = lambda w: pl.BlockSpec((1, w), lambda i: (0, 0))
    return pl.pallas_call(
        functools.partial(_out_proj_kernel, tm=tm),
        grid=(n // tm,),
        in_specs=[
            row(LRU_WIDTH), row(RWKV_WIDTH), row(D_MODEL),
            pl.BlockSpec((D_MODEL, D_MODEL), lambda i: (0, 0)),
            vec(D_MODEL), vec(D_MODEL),
            pl.BlockSpec((D_MODEL, 2 * N_EXPERTS), lambda i: (0, 0)),
            vec(N_EXPERTS),
        ],
        out_specs=[row(D_MODEL), pl.BlockSpec((tm * ROW_TILES, LANE), lambda i: (i, 0)),
                   row(TOP_K), row(TOP_K), vec(N_EXPERTS)],
        out_shape=[
            jax.ShapeDtypeStruct((n, D_MODEL), F32),
            jax.ShapeDtypeStruct((n * ROW_TILES, LANE), U32),
            jax.ShapeDtypeStruct((n, TOP_K), jnp.int32),
            jax.ShapeDtypeStruct((n, TOP_K), F32),
            jax.ShapeDtypeStruct((1, N_EXPERTS), jnp.int32),
        ],
        scratch_shapes=[pltpu.VMEM((1, N_EXPERTS), F32)],
        compiler_params=_cparams(("arbitrary",)),
        name="out_proj",
    )(y_lru, y_rw, h0, w_out_bf16, g, b, w_router_split, b_router)


def _scatter_kernel(dest_ref, fill_ref, h_ref, xs_ref, zbuf, sem, zsem, *, tb, tm, nb):
    @pl.when(pl.program_id(0) == 0)
    def _():
        zbuf[...] = jnp.zeros_like(zbuf)

        def fill_copy(start):
            start = pl.multiple_of(start * ROW_TILES, SUBLANE)
            return pltpu.make_async_copy(zbuf, xs_ref.at[pl.ds(start, tm * ROW_TILES), :], zsem)

        def fill(e, carry):
            @pl.when(fill_ref[e] >= 0)
            def _():
                fill_copy(fill_ref[e]).start()
            return carry

        def fill_wait(e, carry):
            @pl.when(fill_ref[e] >= 0)
            def _():
                fill_copy(fill_ref[e]).wait()
            return carry

        def tail(blk, carry):
            fill_copy(blk * tm).start()
            return carry

        def tail_wait(blk, carry):
            fill_copy(blk * tm).wait()
            return carry

        n_used = fill_ref[N_EXPERTS]
        lax.fori_loop(0, N_EXPERTS, fill, 0)
        lax.fori_loop(n_used, nb, tail, 0)
        lax.fori_loop(0, N_EXPERTS, fill_wait, 0)
        lax.fori_loop(n_used, nb, tail_wait, 0)

    base = pl.program_id(0) * (tb * TOP_K)

    def copy(t, kq):
        slot = dest_ref[base + t * TOP_K + kq]
        return pltpu.make_async_copy(_row_tile(h_ref, t), _row_tile(xs_ref, slot), sem)

    def issue(t, carry):
        for kq in range(TOP_K):
            copy(t, kq).start(priority=kq % 2)
        return carry

    lax.fori_loop(0, tb, issue, 0, unroll=4)
    for kq in range(TOP_K):
        pltpu.make_async_copy(h_ref, xs_ref.at[pl.ds(0, tb * ROW_TILES), :], sem).wait()


def _scatter(dest_flat, fill_start, h1p, n_slots, tb, tm):
    n = h1p.shape[0] // ROW_TILES
    nb = n_slots // tm
    grid_spec = pltpu.PrefetchScalarGridSpec(
        num_scalar_prefetch=2,
        grid=(n // tb,),
        in_specs=[pl.BlockSpec((tb * ROW_TILES, LANE), lambda i, d, f: (i, 0))],
        out_specs=pl.BlockSpec(memory_space=pl.ANY),
        scratch_shapes=[pltpu.VMEM((tm * ROW_TILES, LANE), U32), pltpu.SemaphoreType.DMA(()),
                        pltpu.SemaphoreType.DMA(())],
    )
    return pl.pallas_call(
        functools.partial(_scatter_kernel, tb=tb, tm=tm, nb=nb),
        grid_spec=grid_spec,
        out_shape=jax.ShapeDtypeStruct((n_slots * ROW_TILES, LANE), U32),
        compiler_params=_cparams(("arbitrary",)),
        name="scatter",
    )(dest_flat, fill_start, h1p)


def _gmm1_kernel(meta_ref, x_ref, wg_ref, wu_ref, bg_ref, bu_ref, o_ref, wgb, wub, *, nb, tm):
    i = pl.program_id(1)
    e = meta_ref[i]
    prev_e = meta_ref[jnp.maximum(i - 1, 0)]

    @pl.when(jnp.logical_or(i == 0, e != prev_e))
    def _():
        wgb[...] = wg_ref[0].astype(BF16)
        wub[...] = wu_ref[0].astype(BF16)

    @pl.when(i < meta_ref[nb])
    def _():
        x_lo, x_hi = _unpack_halves(_load_rows(x_ref, 0, tm))
        x_lo = x_lo.astype(BF16)
        x_hi = x_hi.astype(BF16)
        gate = (jnp.dot(x_lo, wgb[0:HALF, :], preferred_element_type=F32)
                + jnp.dot(x_hi, wgb[HALF:, :], preferred_element_type=F32) + bg_ref[0])
        up = (jnp.dot(x_lo, wub[0:HALF, :], preferred_element_type=F32)
              + jnp.dot(x_hi, wub[HALF:, :], preferred_element_type=F32) + bu_ref[0])
        gate = jnp.minimum(gate, SWIGLU_LIMIT)
        up = jnp.clip(up, -SWIGLU_LIMIT, SWIGLU_LIMIT)
        act = gate * _sigmoid(SWIGLU_ALPHA * gate) * (up + 1.0)
        o_ref[...] = act.astype(o_ref.dtype)

    @pl.when(i >= meta_ref[nb])
    def _():
        o_ref[...] = jnp.zeros_like(o_ref)


def _gmm1(meta, xs, w1, b1, nb, tm, tn):
    n_slots = xs.shape[0] // ROW_TILES
    nj = D_EXPERT // tn

    def row_blk(j, i, m):
        return jnp.minimum(i, m[nb] - 1)

    grid_spec = pltpu.PrefetchScalarGridSpec(
        num_scalar_prefetch=1,
        grid=(nj, nb),
        in_specs=[
            pl.BlockSpec((tm * ROW_TILES, LANE), lambda j, i, m: (row_blk(j, i, m), 0)),
            pl.BlockSpec((1, D_MODEL, tn), lambda j, i, m: (m[i], 0, j)),
            pl.BlockSpec((1, D_MODEL, tn), lambda j, i, m: (m[i], 0, j + nj)),
            pl.BlockSpec((1, 1, tn), lambda j, i, m: (m[i], 0, j)),
            pl.BlockSpec((1, 1, tn), lambda j, i, m: (m[i], 0, j + nj)),
        ],
        out_specs=pl.BlockSpec((tm, tn), lambda j, i, m: (i, j)),
        scratch_shapes=[pltpu.VMEM((D_MODEL, tn), BF16), pltpu.VMEM((D_MODEL, tn), BF16)],
    )
    return pl.pallas_call(
        functools.partial(_gmm1_kernel, nb=nb, tm=tm),
        grid_spec=grid_spec,
        out_shape=jax.ShapeDtypeStruct((n_slots, D_EXPERT), BF16),
        compiler_params=_cparams(("arbitrary", "arbitrary")),
        name="gmm1",
    )(meta, xs, w1, w1, b1, b1)


def _gmm2_kernel(meta_ref, a_ref, w_ref, b_ref, o_ref, wb, *, nb):
    i = pl.program_id(1)
    e = meta_ref[i]
    prev_e = meta_ref[jnp.maximum(i - 1, 0)]

    @pl.when(jnp.logical_or(i == 0, e != prev_e))
    def _():
        wb[...] = w_ref[0].astype(BF16)

    @pl.when(i < meta_ref[nb])
    def _():
        out = jnp.dot(a_ref[...], wb[...], preferred_element_type=F32) + b_ref[0]
        _store_rows(o_ref, 0, _pack_halves(out))

    @pl.when(i >= meta_ref[nb])
    def _():
        o_ref[...] = jnp.zeros_like(o_ref)


def _gmm2(meta, act, w2, b2, nb, tm):
    n_slots = act.shape[0]

    def row_blk(j, i, m):
        return jnp.minimum(i, m[nb] - 1)

    grid_spec = pltpu.PrefetchScalarGridSpec(
        num_scalar_prefetch=1,
        grid=(1, nb),
        in_specs=[
            pl.BlockSpec((tm, D_EXPERT), lambda j, i, m: (row_blk(j, i, m), 0)),
            pl.BlockSpec((1, D_EXPERT, D_MODEL), lambda j, i, m: (m[i], 0, 0)),
            pl.BlockSpec((1, 1, D_MODEL), lambda j, i, m: (m[i], 0, 0)),
        ],
        out_specs=pl.BlockSpec((tm * ROW_TILES, LANE), lambda j, i, m: (i, 0)),
        scratch_shapes=[pltpu.VMEM((D_EXPERT, D_MODEL), BF16)],
    )
    return pl.pallas_call(
        functools.partial(_gmm2_kernel, nb=nb),
        grid_spec=grid_spec,
        out_shape=jax.ShapeDtypeStruct((n_slots * ROW_TILES, LANE), U32),
        compiler_params=_cparams(("arbitrary", "arbitrary")),
        name="gmm2",
    )(meta, act, w2, b2)


def _combine_kernel(dest_ref, h1_ref, tg_ref, g_ref, b_ref, ys_ref, o_ref, buf, sems, *, tb, n_steps):
    i = pl.program_id(0)

    def issue(step, half):
        base = step * (tb * TOP_K)

        def body(t, carry):
            for kq in range(TOP_K):
                slot = dest_ref[base + t * TOP_K + kq]
                pltpu.make_async_copy(_row_tile(ys_ref, slot), _row_tile(buf.at[half, kq], t),
                                      sems.at[half]).start(priority=kq % 2)
            return carry

        lax.fori_loop(0, tb, body, 0, unroll=4)

    @pl.when(i == 0)
    def _():
        issue(0, 0)

    @pl.when(i + 1 < n_steps)
    def _():
        issue(i + 1, (i + 1) % 2)

    half = i % 2
    for kq in range(TOP_K):
        pltpu.make_async_copy(ys_ref.at[pl.ds(0, tb * ROW_TILES), :], buf.at[half, kq],
                              sems.at[half]).wait()

    tg = tg_ref[...]
    y = DEEPNORM_ALPHA * h1_ref[...]
    for kq in range(TOP_K):
        lo, hi = _unpack_halves(_load_rows(buf.at[half, kq], 0, tb))
        y = y + tg[:, kq:kq + 1] * jnp.concatenate([lo, hi], axis=1)
    o_ref[...] = _layer_norm_rows(y, g_ref[...], b_ref[...])


def _combine(dest_flat, h1, tg, g, b, ys, tb):
    n = h1.shape[0]
    grid_spec = pltpu.PrefetchScalarGridSpec(
        num_scalar_prefetch=1,
        grid=(n // tb,),
        in_specs=[
            pl.BlockSpec((tb, D_MODEL), lambda i, d: (i, 0)),
            pl.BlockSpec((tb, TOP_K), lambda i, d: (i, 0)),
            pl.BlockSpec((1, D_MODEL), lambda i, d: (0, 0)),
            pl.BlockSpec((1, D_MODEL), lambda i, d: (0, 0)),
            pl.BlockSpec(memory_space=pl.ANY),
        ],
        out_specs=pl.BlockSpec((tb, D_MODEL), lambda i, d: (i, 0)),
        scratch_shapes=[pltpu.VMEM((2, TOP_K, tb * ROW_TILES, LANE), U32),
                        pltpu.SemaphoreType.DMA((2,))],
    )
    return pl.pallas_call(
        functools.partial(_combine_kernel, tb=tb, n_steps=n // tb),
        grid_spec=grid_spec,
        out_shape=jax.ShapeDtypeStruct((n, D_MODEL), F32),
        compiler_params=_cparams(("arbitrary",)),
        name="combine",
    )(dest_flat, h1, tg, g, b, ys)


def _block_diag(w, group):
    nb = w.shape[0] // group
    w = w.reshape(nb, group, LRU_BLOCK_W, LRU_BLOCK_W)
    eye = jnp.eye(group, dtype=w.dtype)
    out = jnp.einsum('gajk,ab->gajbk', w, eye)
    return out.reshape(nb, group * LRU_BLOCK_W, group * LRU_BLOCK_W)


def _pad_rows(w, rows):
    return jnp.pad(w, ((0, rows - w.shape[0]), (0, 0)))


def _layer(h_in_x, l, prm, bsz, t):
    n = bsz * t
    row = lambda a: a.reshape(1, -1)

    w_in = prm['w_in'][l]
    o_rw = 2 * LRU_WIDTH
    o_l = o_rw + 3 * RWKV_WIDTH
    pad_c = lambda w: jnp.pad(w, ((0, 0), (0, LORA_PAD - w.shape[1])))
    w_in_p = jnp.concatenate([
        w_in[:, :o_l],
        pad_c(w_in[:, o_l:o_l + DECAY_LORA]),
        pad_c(w_in[:, o_l + DECAY_LORA:o_l + DECAY_LORA + AAA_LORA]),
        w_in[:, o_l + DECAY_LORA + AAA_LORA:],
    ], axis=1).astype(BF16)
    mu = prm['shift_mu'][l]
    pad_v = lambda v: jnp.pad(v, (0, LORA_PAD - v.shape[0]))
    mu_l = mu[3 * RWKV_WIDTH:]
    mu_a = jnp.concatenate([pad_v(mu_l[:DECAY_LORA]), pad_v(mu_l[DECAY_LORA:DECAY_LORA + AAA_LORA]),
                            mu_l[DECAY_LORA + AAA_LORA:]])

    tm_in = min(1024, n)
    h0, p = _in_proj(h_in_x, row(prm['ln_in_g']), row(prm['ln_in_b']), w_in_p, tm_in, 512)

    tt = min(512, t)
    group = MXU_DIM // LRU_BLOCK_W
    y_lru = _lru(p, prm['conv_w'][l], row(prm['conv_b'][l]),
                 _block_diag(prm['w_rgate'][l], group).astype(BF16), row(prm['b_rgate'][l]),
                 _block_diag(prm['w_igate'][l], group).astype(BF16), row(prm['b_igate'][l]),
                 row(prm['lru_lambda'][l]), bsz, t, tt)

    head_id = jnp.arange(MXU_DIM) // HEAD_SIZE
    ones_bd = (head_id[:, None] == head_id[None, :]).astype(BF16)
    tt_rw = min(256, t)
    r, k, v, kk, bvec, cl, g = _rw_prep(
        p, row(mu[:RWKV_WIDTH]), row(mu[RWKV_WIDTH:2 * RWKV_WIDTH]),
        row(mu[2 * RWKV_WIDTH:3 * RWKV_WIDTH]), row(mu_a),
        row(prm['w0'][l]), _pad_rows(prm['rw_decay_up'][l], LORA_PAD).astype(BF16),
        row(prm['a0'][l]), _pad_rows(prm['rw_aaa_up'][l], LORA_PAD).astype(BF16),
        prm['rw_gate_up'][l].astype(BF16), row(prm['k_k'][l]), row(prm['k_a'][l]),
        ones_bd, bsz, t, tt_rw)
    y = _rw_chunk(r, k, v, kk, bvec, cl, bsz, t, 4 if bsz % 4 == 0 else 1)
    y_rw = _rw_post(y, r, k, v, g, row(prm['r_k'][l]), row(prm['gn_g'][l]), row(prm['gn_b'][l]),
                    ones_bd, min(512, n))

    w_r = prm['w_router'][l]
    w_r_hi = w_r.astype(BF16)
    w_r_lo = (w_r - w_r_hi.astype(F32)).astype(BF16)
    tm_out = min(512, n)
    assert n < RANK_SPAN
    h1, h1p, code, top_g, counts = _out_proj(
        y_lru, y_rw, h0, prm['w_out'][l].astype(BF16), row(prm['ln1_g'][l]), row(prm['ln1_b'][l]),
        jnp.concatenate([w_r_hi, w_r_lo], axis=1), row(prm['b_router'][l]), tm_out)

    tm = 512
    counts = counts.reshape(N_EXPERTS)
    padded = ((counts + tm - 1) // tm) * tm
    pad_end = jnp.cumsum(padded)
    pad_start = pad_end - padded
    code = code.reshape(n * TOP_K // LANE, LANE)
    dest = (pad_start[code // RANK_SPAN] + code % RANK_SPAN).reshape(-1).astype(jnp.int32)
    nb = (n * TOP_K) // tm + N_EXPERTS
    n_slots = nb * tm
    n_used = (pad_end[-1] // tm).astype(jnp.int32)
    blk = jnp.minimum(jnp.arange(nb, dtype=jnp.int32), n_used - 1)
    blk_e = jnp.sum((pad_end[None, :] <= (blk * tm)[:, None]).astype(jnp.int32), axis=1)
    blk_e = jnp.minimum(blk_e, N_EXPERTS - 1)
    meta = jnp.concatenate([blk_e.astype(jnp.int32), n_used.reshape(1)])

    fill_start = jnp.concatenate([jnp.where(padded > 0, pad_end - tm, -1).astype(jnp.int32),
                                  n_used.reshape(1)])
    tb = min(128, n)
    xs = _scatter(dest, fill_start, h1p, n_slots, min(256, n), tm)
    act = _gmm1(meta, xs, prm['w_exp1'][l], prm['b_exp1'][l].reshape(N_EXPERTS, 1, 2 * D_EXPERT),
                nb, tm, 1024)
    ys = _gmm2(meta, act, prm['w_exp2'][l], prm['b_exp2'][l].reshape(N_EXPERTS, 1, D_MODEL),
               nb, tm)
    return _combine(dest, h1, top_g, row(prm['ln2_g'][l]), row(prm['ln2_b'][l]),
                    ys, tb)


def kernel(x, ln_in_g, ln_in_b, w_in, conv_w, conv_b, w_rgate, b_rgate, w_igate, b_igate, lru_lambda, shift_mu, w0, rw_decay_up, a0, rw_aaa_up, rw_gate_up, k_k, k_a, r_k, gn_g, gn_b, w_out, ln1_g, ln1_b, w_router, b_router, w_exp1, b_exp1, w_exp2, b_exp2, ln2_g, ln2_b):
    bsz, t, d = x.shape
    prm = dict(ln_in_g=ln_in_g, ln_in_b=ln_in_b, w_in=w_in, conv_w=conv_w, conv_b=conv_b,
               w_rgate=w_rgate, b_rgate=b_rgate, w_igate=w_igate, b_igate=b_igate,
               lru_lambda=lru_lambda, shift_mu=shift_mu, w0=w0, rw_decay_up=rw_decay_up, a0=a0,
               rw_aaa_up=rw_aaa_up, rw_gate_up=rw_gate_up, k_k=k_k, k_a=k_a, r_k=r_k, gn_g=gn_g,
               gn_b=gn_b, w_out=w_out, ln1_g=ln1_g, ln1_b=ln1_b, w_router=w_router,
               b_router=b_router, w_exp1=w_exp1, b_exp1=b_exp1, w_exp2=w_exp2, b_exp2=b_exp2,
               ln2_g=ln2_g, ln2_b=ln2_b)
    out = _layer(x.reshape(bsz * t, d), 0, prm, bsz, t)
    return out.reshape(bsz, t, d)
```

```python
import functools

import jax
import jax.numpy as jnp
from jax import lax
from jax.experimental import pallas as pl
from jax.experimental.pallas import tpu as pltpu

D_MODEL = 2048
DEPTH = 1
CHUNK = 64
LRU_WIDTH = 1024
LRU_BLOCKS = 16
LRU_BLOCK_W = LRU_WIDTH // LRU_BLOCKS
CONV_WIDTH = 4
RG_C = 8.0
RWKV_WIDTH = D_MODEL - LRU_WIDTH
HEAD_SIZE = 64
RWKV_HEADS = RWKV_WIDTH // HEAD_SIZE
DECAY_LORA = 96
AAA_LORA = 96
GATE_LORA = 256
N_EXPERTS = 32
TOP_K = 4
D_EXPERT = D_MODEL
SWIGLU_LIMIT = 7.0
SWIGLU_ALPHA = 1.702
LN_EPS = 1e-5
GN_EPS = HEAD_SIZE * 1e-5
DEEPNORM_ALPHA = (2.0 * DEPTH) ** 0.25

LANE = 128
SUBLANE = 8
MXU_DIM = 256
LORA_PAD = 128
LORA_W = 2 * LORA_PAD + GATE_LORA
P_WIDTH = 2 * LRU_WIDTH + 3 * RWKV_WIDTH + LORA_W
VMEM_LIMIT = 56 * 1024 * 1024

F32 = jnp.float32
BF16 = jnp.bfloat16


def _cparams(sem):
    return pltpu.CompilerParams(dimension_semantics=sem, vmem_limit_bytes=VMEM_LIMIT)


def _layer_norm_rows(x, g, b):
    mu = jnp.mean(x, axis=-1, keepdims=True)
    xc = x - mu
    var = jnp.mean(xc * xc, axis=-1, keepdims=True)
    return xc * lax.rsqrt(var + LN_EPS) * g + b


def _softplus(z):
    return jnp.maximum(z, 0.0) + jnp.log1p(jnp.exp(-jnp.abs(z)))


def _sigmoid(z):
    return 1.0 / (1.0 + jnp.exp(-z))


def _split2(x):
    hi = x.astype(BF16)
    lo = (x - hi.astype(F32)).astype(BF16)
    return hi, lo


HALF = D_MODEL // 2
U32 = jnp.uint32
RANK_SPAN = 1 << 20


def _pack_halves(x):
    lo = lax.bitcast_convert_type(x[:, :HALF].astype(BF16).astype(F32), U32) >> 16
    hi = lax.bitcast_convert_type(x[:, HALF:].astype(BF16).astype(F32), U32) & jnp.uint32(0xFFFF0000)
    return hi | lo


def _unpack_halves(w):
    lo = lax.bitcast_convert_type(w << 16, F32)
    hi = lax.bitcast_convert_type(w & jnp.uint32(0xFFFF0000), F32)
    return lo, hi


ROW_TILES = HALF // LANE
assert ROW_TILES == SUBLANE


def _store_rows(ref, row0, packed):
    m = packed.shape[0]
    for c in range(ROW_TILES):
        ref[pl.ds(row0 * ROW_TILES + c, m, stride=ROW_TILES), :] = packed[:, c * LANE:(c + 1) * LANE]


def _load_rows(ref, row0, m):
    return jnp.concatenate(
        [ref[pl.ds(row0 * ROW_TILES + c, m, stride=ROW_TILES), :] for c in range(ROW_TILES)], axis=1)


def _row_tile(ref, row):
    return ref.at[pl.ds(pl.multiple_of(row * ROW_TILES, ROW_TILES), ROW_TILES), :]


def _head_sum(x, ones_ref):
    hi, lo = _split2(x)
    ones = ones_ref[...]
    parts = []
    for g in range(x.shape[1] // MXU_DIM):
        sl = slice(g * MXU_DIM, (g + 1) * MXU_DIM)
        parts.append(jnp.dot(hi[:, sl], ones, preferred_element_type=F32)
                     + jnp.dot(lo[:, sl], ones, preferred_element_type=F32))
    return jnp.concatenate(parts, axis=1)


def _in_proj_kernel(x_ref, g_ref, b_ref, w_ref, h_ref, p_ref, hb_ref):
    @pl.when(pl.program_id(1) == 0)
    def _():
        h = _layer_norm_rows(x_ref[...], g_ref[...], b_ref[...])
        h_ref[...] = h
        hb_ref[...] = h.astype(BF16)

    p_ref[...] = jnp.dot(hb_ref[...], w_ref[...], preferred_element_type=F32)


def _in_proj(x2, g, b, w_bf16, tm, tn):
    n = x2.shape[0]
    return pl.pallas_call(
        _in_proj_kernel,
        grid=(n // tm, P_WIDTH // tn),
        in_specs=[
            pl.BlockSpec((tm, D_MODEL), lambda i, j: (i, 0)),
            pl.BlockSpec((1, D_MODEL), lambda i, j: (0, 0)),
            pl.BlockSpec((1, D_MODEL), lambda i, j: (0, 0)),
            pl.BlockSpec((D_MODEL, tn), lambda i, j: (0, j)),
        ],
        out_specs=[
            pl.BlockSpec((tm, D_MODEL), lambda i, j: (i, 0)),
            pl.BlockSpec((tm, tn), lambda i, j: (i, j)),
        ],
        out_shape=[
            jax.ShapeDtypeStruct((n, D_MODEL), F32),
            jax.ShapeDtypeStruct((n, P_WIDTH), F32),
        ],
        scratch_shapes=[pltpu.VMEM((tm, D_MODEL), BF16)],
        compiler_params=_cparams(("parallel", "arbitrary")),
        name="in_proj",
    )(x2, g, b, w_bf16)


def _lru_kernel(u_ref, gi_ref, cw_ref, cb_ref, wr_ref, br_ref, wi_ref, bi_ref,
                lam_ref, o_ref, ubuf, a_s, b_s, carry, *, tt):
    first = pl.program_id(1) == 0

    @pl.when(first)
    def _():
        ubuf[0:SUBLANE, :] = jnp.zeros((SUBLANE, LRU_WIDTH), F32)
        carry[...] = jnp.zeros_like(carry)

    @pl.when(jnp.logical_not(first))
    def _():
        ubuf[0:SUBLANE, :] = ubuf[tt:tt + SUBLANE, :]

    ubuf[SUBLANE:, :] = u_ref[...]
    uc = cb_ref[...]
    for i in range(CONV_WIDTH):
        off = SUBLANE - (CONV_WIDTH - 1) + i
        uc = uc + cw_ref[i:i + 1, :] * ubuf[off:off + tt, :]

    ucb = uc.astype(BF16)
    n_grp = LRU_WIDTH // MXU_DIM

    def gate(w_ref, bias_ref):
        parts = [jnp.dot(ucb[:, g * MXU_DIM:(g + 1) * MXU_DIM], w_ref[g],
                         preferred_element_type=F32) for g in range(n_grp)]
        return _sigmoid(jnp.concatenate(parts, axis=1) + bias_ref[...])

    r_gate = gate(wr_ref, br_ref)
    i_gate = gate(wi_ref, bi_ref)
    log_a = (-RG_C * r_gate) * _softplus(-lam_ref[...])
    a_s[...] = jnp.exp(log_a)
    th = jnp.tanh(log_a)
    b_s[...] = jnp.sqrt(-2.0 * th / (1.0 - th)) * (i_gate * uc)

    row = lax.broadcasted_iota(jnp.int32, (SUBLANE, LRU_WIDTH), 0)

    def group(gidx, c):
        off = pl.multiple_of(gidx * SUBLANE, SUBLANE)
        a = a_s[pl.ds(off, SUBLANE), :]
        b = b_s[pl.ds(off, SUBLANE), :]
        for d in (1, 2, 4):
            keep = row >= d
            a_sh = pltpu.roll(a, d, axis=0)
            b_sh = pltpu.roll(b, d, axis=0)
            b = jnp.where(keep, a * b_sh + b, b)
            a = jnp.where(keep, a * a_sh, a)
        h = a * c + b
        b_s[pl.ds(off, SUBLANE), :] = h
        return h[SUBLANE - 1:SUBLANE, :]

    carry[...] = lax.fori_loop(0, tt // SUBLANE, group, carry[...])
    o_ref[...] = (b_s[...] * jax.nn.gelu(gi_ref[...])).astype(o_ref.dtype)


def _lru(p, conv_w, conv_b, wr_bd, br, wi_bd, bi, lam, bsz, t, tt):
    n = bsz * t
    nt = t // tt
    vec = lambda: pl.BlockSpec((1, LRU_WIDTH), lambda b, i: (0, 0))
    wspec = lambda: pl.BlockSpec((LRU_WIDTH // MXU_DIM, MXU_DIM, MXU_DIM), lambda b, i: (0, 0, 0))
    return pl.pallas_call(
        functools.partial(_lru_kernel, tt=tt),
        grid=(bsz, nt),
        in_specs=[
            pl.BlockSpec((tt, LRU_WIDTH), lambda b, i: (b * nt + i, 0)),
            pl.BlockSpec((tt, LRU_WIDTH), lambda b, i: (b * nt + i, 1)),
            pl.BlockSpec((CONV_WIDTH, LRU_WIDTH), lambda b, i: (0, 0)),
            vec(), wspec(), vec(), wspec(), vec(), vec(),
        ],
        out_specs=pl.BlockSpec((tt, LRU_WIDTH), lambda b, i: (b * nt + i, 0)),
        out_shape=jax.ShapeDtypeStruct((n, LRU_WIDTH), BF16),
        scratch_shapes=[
            pltpu.VMEM((tt + SUBLANE, LRU_WIDTH), F32),
            pltpu.VMEM((tt, LRU_WIDTH), F32),
            pltpu.VMEM((tt, LRU_WIDTH), F32),
            pltpu.VMEM((1, LRU_WIDTH), F32),
        ],
        compiler_params=_cparams(("parallel", "arbitrary")),
        name="lru",
    )(p, p, conv_w, conv_b, wr_bd, br, wi_bd, bi, lam)


def _rw_prep_kernel(pr_ref, pk_ref, pv_ref, pa_ref, mur_ref, muk_ref, muv_ref, mua_ref,
                    w0_ref, wdec_ref, a0_ref, waaa_ref, wgate_ref, kk_ref, ka_ref, ones_ref,
                    r_out, k_out, v_out, kkn_out, b_out, cl_out, g_out,
                    prev_r, prev_k, prev_v, prev_a, *, tt):
    first = pl.program_id(1) == 0

    @pl.when(first)
    def _():
        for ref in (prev_r, prev_k, prev_v, prev_a):
            ref[...] = jnp.zeros_like(ref)

    def shift(x_ref, prev_ref, mu_ref):
        x = x_ref[...]
        row = lax.broadcasted_iota(jnp.int32, x.shape, 0)
        prev = jnp.where(row == 0, prev_ref[...], pltpu.roll(x, 1, axis=0))
        prev_ref[...] = x[tt - 1:tt, :]
        return x + (prev - x) * mu_ref[...]

    r = shift(pr_ref, prev_r, mur_ref)
    k = shift(pk_ref, prev_k, muk_ref)
    v = shift(pv_ref, prev_v, muv_ref)
    ad = shift(pa_ref, prev_a, mua_ref)

    wd = jnp.tanh(ad[:, 0:LORA_PAD]).astype(BF16)
    aa = ad[:, LORA_PAD:2 * LORA_PAD].astype(BF16)
    gd = _sigmoid(ad[:, 2 * LORA_PAD:]).astype(BF16)
    w_pre = w0_ref[...] + jnp.dot(wd, wdec_ref[...], preferred_element_type=F32)
    w_log = -_softplus(-w_pre) - 0.5
    a = _sigmoid(a0_ref[...] + jnp.dot(aa, waaa_ref[...], preferred_element_type=F32))
    g = jnp.dot(gd, wgate_ref[...], preferred_element_type=F32)

    kk = k * kk_ref[...]
    norm = jnp.sqrt(_head_sum(kk * kk, ones_ref))
    kk = kk / jnp.maximum(norm, 1e-12)

    lw = -jnp.exp(w_log)
    ri = lax.broadcasted_iota(jnp.int32, (tt, tt), 0)
    ci = lax.broadcasted_iota(jnp.int32, (tt, tt), 1)
    tri = jnp.logical_and(ri // CHUNK == ci // CHUNK, ri >= ci).astype(BF16)
    lw_hi = lw.astype(BF16)
    rem = lw - lw_hi.astype(F32)
    lw_mid = rem.astype(BF16)
    lw_lo = (rem - lw_mid.astype(F32)).astype(BF16)
    cl = (jnp.dot(tri, lw_hi, preferred_element_type=F32)
          + jnp.dot(tri, lw_mid, preferred_element_type=F32)
          + jnp.dot(tri, lw_lo, preferred_element_type=F32))

    r_out[...] = r
    k_out[...] = k * (1.0 + (a - 1.0) * ka_ref[...])
    v_out[...] = v
    kkn_out[...] = kk
    b_out[...] = kk * a
    cl_out[...] = cl
    g_out[...] = g


def _rw_prep(p, mu_r, mu_k, mu_v, mu_a, w0, wdec, a0, waaa, wgate, k_k, k_a, ones_bd, bsz, t, tt):
    n = bsz * t
    nt = t // tt
    cb = 2 * LRU_WIDTH // RWKV_WIDTH
    row = lambda c: pl.BlockSpec((tt, RWKV_WIDTH), lambda b, i: (b * nt + i, c))
    vec = lambda w: pl.BlockSpec((1, w), lambda b, i: (0, 0))
    full = lambda s: pl.BlockSpec(s, lambda b, i: (0, 0))
    lora_cb = (2 * LRU_WIDTH + 3 * RWKV_WIDTH) // LORA_W
    out = jax.ShapeDtypeStruct((n, RWKV_WIDTH), F32)
    return pl.pallas_call(
        functools.partial(_rw_prep_kernel, tt=tt),
        grid=(bsz, nt),
        in_specs=[
            row(cb), row(cb + 1), row(cb + 2),
            pl.BlockSpec((tt, LORA_W), lambda b, i: (b * nt + i, lora_cb)),
            vec(RWKV_WIDTH), vec(RWKV_WIDTH), vec(RWKV_WIDTH), vec(LORA_W),
            vec(RWKV_WIDTH), full((LORA_PAD, RWKV_WIDTH)),
            vec(RWKV_WIDTH), full((LORA_PAD, RWKV_WIDTH)),
            full((GATE_LORA, RWKV_WIDTH)),
            vec(RWKV_WIDTH), vec(RWKV_WIDTH),
            full((MXU_DIM, MXU_DIM)),
        ],
        out_specs=[pl.BlockSpec((tt, RWKV_WIDTH), lambda b, i: (b * nt + i, 0))] * 7,
        out_shape=[out] * 7,
        scratch_shapes=[pltpu.VMEM((1, RWKV_WIDTH), F32)] * 3 + [pltpu.VMEM((1, LORA_W), F32)],
        compiler_params=_cparams(("parallel", "arbitrary")),
        name="rw_prep",
    )(p, p, p, p, mu_r, mu_k, mu_v, mu_a, w0, wdec, a0, waaa, wgate, k_k, k_a, ones_bd)


def _rw_chunk_kernel(r_ref, k_ref, v_ref, kk_ref, b_ref, cl_ref, y_ref, *s_refs):
    c = CHUNK
    hs = HEAD_SIZE
    n_seq = r_ref.shape[0]

    @pl.when(pl.program_id(1) == 0)
    def _():
        for s_ref in s_refs:
            s_ref[...] = jnp.zeros_like(s_ref)

    r2 = lax.broadcasted_iota(jnp.int32, (2 * c, 2 * c), 0)
    c2 = lax.broadcasted_iota(jnp.int32, (2 * c, 2 * c), 1)
    tq = jnp.where(r2 >= c, r2 - c, r2)
    tk = jnp.where(c2 >= c, c2 - c, c2)
    mask = tk < tq + jnp.where(r2 >= c, 1, 0)
    zeros_cc = jnp.zeros((c, hs), BF16)

    dn_t = (((1,), (1,)), ((), ()))
    dn_l = (((0,), (0,)), ((), ()))

    row_w = lax.broadcasted_iota(jnp.int32, (c, RWKV_WIDTH), 0)

    def scaled_operands(q):
        cl = cl_ref[q]
        cl_ex = jnp.where(row_w == 0, 0.0, pltpu.roll(cl, 1, axis=0))
        cl_last = cl[c - 1:c, :]
        g_inv = jnp.exp(-cl)
        g_dec = jnp.exp(cl_last - cl)
        bv = b_ref[q]
        kx = k_ref[q]
        rg = r_ref[q] * jnp.exp(cl)
        return dict(
            g_last=jnp.exp(cl_last),
            v=v_ref[q].astype(BF16),
            rg=rg,
            lhs_a=(-kk_ref[q] * jnp.exp(cl_ex)).astype(BF16),
            lhs_r=rg.astype(BF16),
            rhs_b=(bv * g_inv).astype(BF16),
            rhs_k=(kx * g_inv).astype(BF16),
            dec_b=(bv * g_dec).astype(BF16),
            dec_k=(kx * g_dec).astype(BF16),
        )

    ops = [scaled_operands(q) for q in range(n_seq)]

    chains = [(q, h) for q in range(n_seq) for h in range(RWKV_HEADS)]
    sl_of = lambda h: slice(h * hs, (h + 1) * hs)
    s0s = [s_refs[q * RWKV_HEADS + h][...] for q, h in chains]
    aas = []
    for q, h in chains:
        o, sl = ops[q], sl_of(h)
        lhs = jnp.concatenate([o['lhs_a'][:, sl], o['lhs_r'][:, sl]], axis=0)
        rhs = jnp.concatenate([o['rhs_b'][:, sl], o['rhs_k'][:, sl]], axis=0)
        aa = lax.dot_general(lhs, rhs, dn_t, preferred_element_type=F32)
        aas.append(jnp.where(mask, aa, 0.0))
    v_hs = [ops[q]['v'][:, sl_of(h)] for q, h in chains]
    a_rs = [aa[c:, :].astype(BF16) for aa in aas]
    ps = [aa[:c, :c] for aa in aas]
    xs = []
    for i, (q, h) in enumerate(chains):
        akv = jnp.dot(aas[i][:c, c:].astype(BF16), v_hs[i], preferred_element_type=F32)
        xs.append(jnp.concatenate([ops[q]['lhs_a'][:, sl_of(h)].astype(F32), akv], axis=1))
    n_sq = 6
    for it in range(n_sq):
        pbs = [p.astype(BF16) for p in ps]
        xs = [x + jnp.dot(pb, x.astype(BF16), preferred_element_type=F32)
              for x, pb in zip(xs, pbs)]
        if it + 1 < n_sq:
            ps = [jnp.dot(pb, pb, preferred_element_type=F32) for pb in pbs]
    for i, (q, h) in enumerate(chains):
        o, sl = ops[q], sl_of(h)
        rhs2 = jnp.concatenate(
            [xs[i].astype(BF16), jnp.concatenate([zeros_cc, v_hs[i]], axis=1)], axis=0)
        qy = jnp.dot(a_rs[i], rhs2, preferred_element_type=F32)
        dec = jnp.concatenate([o['dec_b'][:, sl], o['dec_k'][:, sl]], axis=0)
        mnt = lax.dot_general(rhs2, dec, dn_l, preferred_element_type=F32)
        s0b = s0s[i].astype(BF16)
        qq = (o['rg'][:, sl] + qy[:, :hs]).astype(BF16)
        y_ref[q, :, sl] = lax.dot_general(qq, s0b, dn_t, preferred_element_type=F32) + qy[:, hs:]
        s_refs[i][...] = (s0s[i] * o['g_last'][:, sl]
                          + jnp.dot(s0b, mnt[:hs, :].astype(BF16), preferred_element_type=F32)
                          + mnt[hs:, :])


def _rw_chunk(r, k, v, kk, b, cl, bsz, t, n_seq):
    nc = t // CHUNK
    shape3 = (bsz, t, RWKV_WIDTH)
    spec = lambda: pl.BlockSpec((n_seq, CHUNK, RWKV_WIDTH), lambda bb, i: (bb, i, 0))
    y = pl.pallas_call(
        _rw_chunk_kernel,
        grid=(bsz // n_seq, nc),
        in_specs=[spec() for _ in range(6)],
        out_specs=spec(),
        out_shape=jax.ShapeDtypeStruct(shape3, F32),
        scratch_shapes=[pltpu.VMEM((HEAD_SIZE, HEAD_SIZE), F32)
                        for _ in range(n_seq * RWKV_HEADS)],
        compiler_params=_cparams(("parallel", "arbitrary")),
        name="rw_chunk",
    )(*[a.reshape(shape3) for a in (r, k, v, kk, b, cl)])
    return y.reshape(bsz * t, RWKV_WIDTH)


def _rw_post_kernel(y_ref, r_ref, k_ref, v_ref, g_ref, rk_ref, gg_ref, gb_ref, ones_ref, o_ref):
    y = y_ref[...]
    inv = 1.0 / HEAD_SIZE
    mu = _head_sum(y, ones_ref) * inv
    yc = y - mu
    var = _head_sum(yc * yc, ones_ref) * inv
    yn = yc * lax.rsqrt(var + GN_EPS) * gg_ref[...] + gb_ref[...]
    bonus = _head_sum(r_ref[...] * k_ref[...] * rk_ref[...], ones_ref) * v_ref[...]
    o_ref[...] = ((yn + bonus) * g_ref[...]).astype(o_ref.dtype)


def _rw_post(y, r, k, v, g, r_k, gn_g, gn_b, ones_bd, tt):
    n = y.shape[0]
    row = lambda: pl.BlockSpec((tt, RWKV_WIDTH), lambda i: (i, 0))
    vec = lambda: pl.BlockSpec((1, RWKV_WIDTH), lambda i: (0, 0))
    return pl.pallas_call(
        _rw_post_kernel,
        grid=(n // tt,),
        in_specs=[row(), row(), row(), row(), row(), vec(), vec(), vec(),
                  pl.BlockSpec((MXU_DIM, MXU_DIM), lambda i: (0, 0))],
        out_specs=row(),
        out_shape=jax.ShapeDtypeStruct((n, RWKV_WIDTH), BF16),
        compiler_params=_cparams(("parallel",)),
        name="rw_post",
    )(y, r, k, v, g, r_k, gn_g, gn_b, ones_bd)


def _out_proj_kernel(yl_ref, yr_ref, h0_ref, wo_ref, g_ref, b_ref, wr_ref, br_ref,
                     h1_ref, h1p_ref, code_ref, tg_ref, cnt_ref, carry, *, tm):
    @pl.when(pl.program_id(0) == 0)
    def _():
        carry[...] = jnp.zeros_like(carry)

    mix = jnp.dot(jnp.concatenate([yl_ref[...], yr_ref[...]], axis=1), wo_ref[...],
                  preferred_element_type=F32)
    h1 = _layer_norm_rows(DEEPNORM_ALPHA * h0_ref[...] + mix, g_ref[...], b_ref[...])
    h1_ref[...] = h1
    _store_rows(h1p_ref, 0, _pack_halves(h1))

    h_hi, h_lo = _split2(h1)
    both = jnp.dot(h_hi, wr_ref[...], preferred_element_type=F32)
    logits = (both[:, :N_EXPERTS] + both[:, N_EXPERTS:]
              + jnp.dot(h_lo, wr_ref[:, 0:N_EXPERTS], preferred_element_type=F32)) + br_ref[...]

    lane = lax.broadcasted_iota(jnp.int32, (tm, N_EXPERTS), 1).astype(F32)
    lane4 = lax.broadcasted_iota(jnp.int32, (tm, TOP_K), 1)
    work = logits
    vals, idxs, sels = [], [], []
    for _ in range(TOP_K):
        m = jnp.max(work, axis=-1, keepdims=True)
        idx = jnp.min(jnp.where(work == m, lane, float(N_EXPERTS)), axis=-1, keepdims=True)
        sel = lane == idx
        vals.append(m)
        idxs.append(idx)
        sels.append(sel)
        work = jnp.where(sel, -jnp.inf, work)
    exps = [jnp.exp(vv - vals[0]) for vv in vals]
    denom = exps[0] + exps[1] + exps[2] + exps[3]

    member = jnp.zeros((tm, N_EXPERTS), F32)
    for sel in sels:
        member = member + sel.astype(F32)
    ri = lax.broadcasted_iota(jnp.int32, (tm, tm), 0)
    ci = lax.broadcasted_iota(jnp.int32, (tm, tm), 1)
    before = (ci < ri).astype(BF16)
    rank_full = jnp.dot(before, member.astype(BF16), preferred_element_type=F32) + carry[...]
    carry[...] = carry[...] + jnp.sum(member, axis=0, keepdims=True)
    cnt_ref[...] = carry[...].astype(jnp.int32)

    code = jnp.zeros((tm, TOP_K), jnp.int32)
    tg = jnp.zeros((tm, TOP_K), F32)
    for kq in range(TOP_K):
        rank_k = jnp.sum(jnp.where(sels[kq], rank_full, 0.0), axis=-1, keepdims=True)
        code_k = idxs[kq].astype(jnp.int32) * RANK_SPAN + rank_k.astype(jnp.int32)
        code = jnp.where(lane4 == kq, code_k, code)
        tg = jnp.where(lane4 == kq, exps[kq] / denom, tg)
    code_ref[...] = code
    tg_ref[...] = tg


def _out_proj(y_lru, y_rw, h0, w_out_bf16, g, b, w_router_split, b_router, tm):
    n = h0.shape[0]
    row = lambda w: pl.BlockSpec((tm, w), lambda i: (i, 0))
    vec = lambda w: pl.BlockSpec((1, w), lambda i: (0, 0))
    return pl.pallas_call(
        functools.partial(_out_proj_kernel, tm=tm),
        grid=(n // tm,),
        in_specs=[
            row(LRU_WIDTH), row(RWKV_WIDTH), row(D_MODEL),
            pl.BlockSpec((D_MODEL, D_MODEL), lambda i: (0, 0)),
            vec(D_MODEL), vec(D_MODEL),
            pl.BlockSpec((D_MODEL, 2 * N_EXPERTS), lambda i: (0, 0)),
            vec(N_EXPERTS),
        ],
        out_specs=[row(D_MODEL), pl.BlockSpec((tm * ROW_TILES, LANE), lambda i: (i, 0)),
                   row(TOP_K), row(TOP_K), vec(N_EXPERTS)],
        out_shape=[
            jax.ShapeDtypeStruct((n, D_MODEL), F32),
            jax.ShapeDtypeStruct((n * ROW_TILES, LANE), U32),
            jax.ShapeDtypeStruct((n, TOP_K), jnp.int32),
            jax.ShapeDtypeStruct((n, TOP_K), F32),
            jax.ShapeDtypeStruct((1, N_EXPERTS), jnp.int32),
        ],
        scratch_shapes=[pltpu.VMEM((1, N_EXPERTS), F32)],
        compiler_params=_cparams(("arbitrary",)),
        name="out_proj",
    )(y_lru, y_rw, h0, w_out_bf16, g, b, w_router_split, b_router)


def _scatter_kernel(dest_ref, fill_ref, h_ref, xs_ref, zbuf, sem, zsem, *, tb, tm, nb):
    @pl.when(pl.program_id(0) == 0)
    def _():
        zbuf[...] = jnp.zeros_like(zbuf)

        def fill_copy(start):
            start = pl.multiple_of(start * ROW_TILES, SUBLANE)
            return pltpu.make_async_copy(zbuf, xs_ref.at[pl.ds(start, tm * ROW_TILES), :], zsem)

        def fill(e, carry):
            @pl.when(fill_ref[e] >= 0)
            def _():
                fill_copy(fill_ref[e]).start()
            return carry

        def fill_wait(e, carry):
            @pl.when(fill_ref[e] >= 0)
            def _():
                fill_copy(fill_ref[e]).wait()
            return carry

        def tail(blk, carry):
            fill_copy(blk * tm).start()
            return carry

        def tail_wait(blk, carry):
            fill_copy(blk * tm).wait()
            return carry

        n_used = fill_ref[N_EXPERTS]
        lax.fori_loop(0, N_EXPERTS, fill, 0)
        lax.fori_loop(n_used, nb, tail, 0)
        lax.fori_loop(0, N_EXPERTS, fill_wait, 0)
        lax.fori_loop(n_used, nb, tail_wait, 0)

    base = pl.program_id(0) * (tb * TOP_K)

    def copy(t, kq):
        slot = dest_ref[base + t * TOP_K + kq]
        return pltpu.make_async_copy(_row_tile(h_ref, t), _row_tile(xs_ref, slot), sem)

    def issue(t, carry):
        for kq in range(TOP_K):
            copy(t, kq).start(priority=kq % 2)
        return carry

    lax.fori_loop(0, tb, issue, 0, unroll=4)
    for kq in range(TOP_K):
        pltpu.make_async_copy(h_ref, xs_ref.at[pl.ds(0, tb * ROW_TILES), :], sem).wait()


def _scatter(dest_flat, fill_start, h1p, n_slots, tb, tm):
    n = h1p.shape[0] // ROW_TILES
    nb = n_slots // tm
    grid_spec = pltpu.PrefetchScalarGridSpec(
        num_scalar_prefetch=2,
        grid=(n // tb,),
        in_specs=[pl.BlockSpec((tb * ROW_TILES, LANE), lambda i, d, f: (i, 0))],
        out_specs=pl.BlockSpec(memory_space=pl.ANY),
        scratch_shapes=[pltpu.VMEM((tm * ROW_TILES, LANE), U32), pltpu.SemaphoreType.DMA(()),
                        pltpu.SemaphoreType.DMA(())],
    )
    return pl.pallas_call(
        functools.partial(_scatter_kernel, tb=tb, tm=tm, nb=nb),
        grid_spec=grid_spec,
        out_shape=jax.ShapeDtypeStruct((n_slots * ROW_TILES, LANE), U32),
        compiler_params=_cparams(("arbitrary",)),
        name="scatter",
    )(dest_flat, fill_start, h1p)


def _gmm1_kernel(meta_ref, x_ref, wg_ref, wu_ref, bg_ref, bu_ref, o_ref, wgb, wub, *, nb, tm):
    i = pl.program_id(1)
    e = meta_ref[i]
    prev_e = meta_ref[jnp.maximum(i - 1, 0)]

    @pl.when(jnp.logical_or(i == 0, e != prev_e))
    def _():
        wgb[...] = wg_ref[0].astype(BF16)
        wub[...] = wu_ref[0].astype(BF16)

    @pl.when(i < meta_ref[nb])
    def _():
        x_lo, x_hi = _unpack_halves(_load_rows(x_ref, 0, tm))
        x_lo = x_lo.astype(BF16)
        x_hi = x_hi.astype(BF16)
        gate = (jnp.dot(x_lo, wgb[0:HALF, :], preferred_element_type=F32)
                + jnp.dot(x_hi, wgb[HALF:, :], preferred_element_type=F32) + bg_ref[0])
        up = (jnp.dot(x_lo, wub[0:HALF, :], preferred_element_type=F32)
              + jnp.dot(x_hi, wub[HALF:, :], preferred_element_type=F32) + bu_ref[0])
        gate = jnp.minimum(gate, SWIGLU_LIMIT)
        up = jnp.clip(up, -SWIGLU_LIMIT, SWIGLU_LIMIT)
        act = gate * _sigmoid(SWIGLU_ALPHA * gate) * (up + 1.0)
        o_ref[...] = act.astype(o_ref.dtype)

    @pl.when(i >= meta_ref[nb])
    def _():
        o_ref[...] = jnp.zeros_like(o_ref)


def _gmm1(meta, xs, w1, b1, nb, tm, tn):
    n_slots = xs.shape[0] // ROW_TILES
    nj = D_EXPERT // tn

    def row_blk(j, i, m):
        return jnp.minimum(i, m[nb] - 1)

    grid_spec = pltpu.PrefetchScalarGridSpec(
        num_scalar_prefetch=1,
        grid=(nj, nb),
        in_specs=[
            pl.BlockSpec((tm * ROW_TILES, LANE), lambda j, i, m: (row_blk(j, i, m), 0)),
            pl.BlockSpec((1, D_MODEL, tn), lambda j, i, m: (m[i], 0, j)),
            pl.BlockSpec((1, D_MODEL, tn), lambda j, i, m: (m[i], 0, j + nj)),
            pl.BlockSpec((1, 1, tn), lambda j, i, m: (m[i], 0, j)),
            pl.BlockSpec((1, 1, tn), lambda j, i, m: (m[i], 0, j + nj)),
        ],
        out_specs=pl.BlockSpec((tm, tn), lambda j, i, m: (i, j)),
        scratch_shapes=[pltpu.VMEM((D_MODEL, tn), BF16), pltpu.VMEM((D_MODEL, tn), BF16)],
    )
    return pl.pallas_call(
        functools.partial(_gmm1_kernel, nb=nb, tm=tm),
        grid_spec=grid_spec,
        out_shape=jax.ShapeDtypeStruct((n_slots, D_EXPERT), BF16),
        compiler_params=_cparams(("arbitrary", "arbitrary")),
        name="gmm1",
    )(meta, xs, w1, w1, b1, b1)


def _gmm2_kernel(meta_ref, a_ref, w_ref, b_ref, o_ref, wb, *, nb):
    i = pl.program_id(1)
    e = meta_ref[i]
    prev_e = meta_ref[jnp.maximum(i - 1, 0)]

    @pl.when(jnp.logical_or(i == 0, e != prev_e))
    def _():
        wb[...] = w_ref[0].astype(BF16)

    @pl.when(i < meta_ref[nb])
    def _():
        out = jnp.dot(a_ref[...], wb[...], preferred_element_type=F32) + b_ref[0]
        _store_rows(o_ref, 0, _pack_halves(out))

    @pl.when(i >= meta_ref[nb])
    def _():
        o_ref[...] = jnp.zeros_like(o_ref)


def _gmm2(meta, act, w2, b2, nb, tm):
    n_slots = act.shape[0]

    def row_blk(j, i, m):
        return jnp.minimum(i, m[nb] - 1)

    grid_spec = pltpu.PrefetchScalarGridSpec(
        num_scalar_prefetch=1,
        grid=(1, nb),
        in_specs=[
            pl.BlockSpec((tm, D_EXPERT), lambda j, i, m: (row_blk(j, i, m), 0)),
            pl.BlockSpec((1, D_EXPERT, D_MODEL), lambda j, i, m: (m[i], 0, 0)),
            pl.BlockSpec((1, 1, D_MODEL), lambda j, i, m: (m[i], 0, 0)),
        ],
        out_specs=pl.BlockSpec((tm * ROW_TILES, LANE), lambda j, i, m: (i, 0)),
        scratch_shapes=[pltpu.VMEM((D_EXPERT, D_MODEL), BF16)],
    )
    return pl.pallas_call(
        functools.partial(_gmm2_kernel, nb=nb),
        grid_spec=grid_spec,
        out_shape=jax.ShapeDtypeStruct((n_slots * ROW_TILES, LANE), U32),
        compiler_params=_cparams(("arbitrary", "arbitrary")),
        name="gmm2",
    )(meta, act, w2, b2)


def _combine_kernel(dest_ref, h1_ref, tg_ref, g_ref, b_ref, ys_ref, o_ref, buf, sems, *, tb, n_steps):
    i = pl.program_id(0)

    def issue(step, half):
        base = step * (tb * TOP_K)

        def body(t, carry):
            for kq in range(TOP_K):
                slot = dest_ref[base + t * TOP_K + kq]
                pltpu.make_async_copy(_row_tile(ys_ref, slot), _row_tile(buf.at[half, kq], t),
                                      sems.at[half]).start(priority=kq % 2)
            return carry

        lax.fori_loop(0, tb, body, 0, unroll=4)

    @pl.when(i == 0)
    def _():
        issue(0, 0)

    @pl.when(i + 1 < n_steps)
    def _():
        issue(i + 1, (i + 1) % 2)

    half = i % 2
    for kq in range(TOP_K):
        pltpu.make_async_copy(ys_ref.at[pl.ds(0, tb * ROW_TILES), :], buf.at[half, kq],
                              sems.at[half]).wait()

    tg = tg_ref[...]
    y = DEEPNORM_ALPHA * h1_ref[...]
    for kq in range(TOP_K):
        lo, hi = _unpack_halves(_load_rows(buf.at[half, kq], 0, tb))
        y = y + tg[:, kq:kq + 1] * jnp.concatenate([lo, hi], axis=1)
    o_ref[...] = _layer_norm_rows(y, g_ref[...], b_ref[...])


def _combine(dest_flat, h1, tg, g, b, ys, tb):
    n = h1.shape[0]
    grid_spec = pltpu.PrefetchScalarGridSpec(
        num_scalar_prefetch=1,
        grid=(n // tb,),
        in_specs=[
            pl.BlockSpec((tb, D_MODEL), lambda i, d: (i, 0)),
            pl.BlockSpec((tb, TOP_K), lambda i, d: (i, 0)),
            pl.BlockSpec((1, D_MODEL), lambda i, d: (0, 0)),
            pl.BlockSpec((1, D_MODEL), lambda i, d: (0, 0)),
            pl.BlockSpec(memory_space=pl.ANY),
        ],
        out_specs=pl.BlockSpec((tb, D_MODEL), lambda i, d: (i, 0)),
        scratch_shapes=[pltpu.VMEM((2, TOP_K, tb * ROW_TILES, LANE), U32),
                        pltpu.SemaphoreType.DMA((2,))],
    )
    return pl.pallas_call(
        functools.partial(_combine_kernel, tb=tb, n_steps=n // tb),
        grid_spec=grid_spec,
        out_shape=jax.ShapeDtypeStruct((n, D_MODEL), F32),
        compiler_params=_cparams(("arbitrary",)),
        name="combine",
    )(dest_flat, h1, tg, g, b, ys)


def _block_diag(w, group):
    nb = w.shape[0] // group
    w = w.reshape(nb, group, LRU_BLOCK_W, LRU_BLOCK_W)
    eye = jnp.eye(group, dtype=w.dtype)
    out = jnp.einsum('gajk,ab->gajbk', w, eye)
    return out.reshape(nb, group * LRU_BLOCK_W, group * LRU_BLOCK_W)


def _pad_rows(w, rows):
    return jnp.pad(w, ((0, rows - w.shape[0]), (0, 0)))


def _layer(h_in_x, l, prm, bsz, t):
    n = bsz * t
    row = lambda a: a.reshape(1, -1)

    w_in = prm['w_in'][l]
    o_rw = 2 * LRU_WIDTH
    o_l = o_rw + 3 * RWKV_WIDTH
    pad_c = lambda w: jnp.pad(w, ((0, 0), (0, LORA_PAD - w.shape[1])))
    w_in_p = jnp.concatenate([
        w_in[:, :o_l],
        pad_c(w_in[:, o_l:o_l + DECAY_LORA]),
        pad_c(w_in[:, o_l + DECAY_LORA:o_l + DECAY_LORA + AAA_LORA]),
        w_in[:, o_l + DECAY_LORA + AAA_LORA:],
    ], axis=1).astype(BF16)
    mu = prm['shift_mu'][l]
    pad_v = lambda v: jnp.pad(v, (0, LORA_PAD - v.shape[0]))
    mu_l = mu[3 * RWKV_WIDTH:]
    mu_a = jnp.concatenate([pad_v(mu_l[:DECAY_LORA]), pad_v(mu_l[DECAY_LORA:DECAY_LORA + AAA_LORA]),
                            mu_l[DECAY_LORA + AAA_LORA:]])

    tm_in = min(1024, n)
    h0, p = _in_proj(h_in_x, row(prm['ln_in_g']), row(prm['ln_in_b']), w_in_p, tm_in, 512)

    tt = min(1024, t)
    group = MXU_DIM // LRU_BLOCK_W
    y_lru = _lru(p, prm['conv_w'][l], row(prm['conv_b'][l]),
                 _block_diag(prm['w_rgate'][l], group).astype(BF16), row(prm['b_rgate'][l]),
                 _block_diag(prm['w_igate'][l], group).astype(BF16), row(prm['b_igate'][l]),
                 row(prm['lru_lambda'][l]), bsz, t, tt)

    head_id = jnp.arange(MXU_DIM) // HEAD_SIZE
    ones_bd = (head_id[:, None] == head_id[None, :]).astype(BF16)
    tt_rw = min(512, t)
    r, k, v, kk, bvec, cl, g = _rw_prep(
        p, row(mu[:RWKV_WIDTH]), row(mu[RWKV_WIDTH:2 * RWKV_WIDTH]),
        row(mu[2 * RWKV_WIDTH:3 * RWKV_WIDTH]), row(mu_a),
        row(prm['w0'][l]), _pad_rows(prm['rw_decay_up'][l], LORA_PAD).astype(BF16),
        row(prm['a0'][l]), _pad_rows(prm['rw_aaa_up'][l], LORA_PAD).astype(BF16),
        prm['rw_gate_up'][l].astype(BF16), row(prm['k_k'][l]), row(prm['k_a'][l]),
        ones_bd, bsz, t, tt_rw)
    y = _rw_chunk(r, k, v, kk, bvec, cl, bsz, t, 4 if bsz % 4 == 0 else 1)
    y_rw = _rw_post(y, r, k, v, g, row(prm['r_k'][l]), row(prm['gn_g'][l]), row(prm['gn_b'][l]),
                    ones_bd, min(1024, n))

    w_r = prm['w_router'][l]
    w_r_hi = w_r.astype(BF16)
    w_r_lo = (w_r - w_r_hi.astype(F32)).astype(BF16)
    tm_out = min(512, n)
    assert n < RANK_SPAN
    h1, h1p, code, top_g, counts = _out_proj(
        y_lru, y_rw, h0, prm['w_out'][l].astype(BF16), row(prm['ln1_g'][l]), row(prm['ln1_b'][l]),
        jnp.concatenate([w_r_hi, w_r_lo], axis=1), row(prm['b_router'][l]), tm_out)

    tm = 512
    counts = counts.reshape(N_EXPERTS)
    padded = ((counts + tm - 1) // tm) * tm
    pad_end = jnp.cumsum(padded)
    pad_start = pad_end - padded
    code = code.reshape(n * TOP_K // LANE, LANE)
    dest = (pad_start[code // RANK_SPAN] + code % RANK_SPAN).reshape(-1).astype(jnp.int32)
    nb = (n * TOP_K) // tm + N_EXPERTS
    n_slots = nb * tm
    n_used = (pad_end[-1] // tm).astype(jnp.int32)
    blk = jnp.minimum(jnp.arange(nb, dtype=jnp.int32), n_used - 1)
    blk_e = jnp.sum((pad_end[None, :] <= (blk * tm)[:, None]).astype(jnp.int32), axis=1)
    blk_e = jnp.minimum(blk_e, N_EXPERTS - 1)
    meta = jnp.concatenate([blk_e.astype(jnp.int32), n_used.reshape(1)])

    fill_start = jnp.concatenate([jnp.where(padded > 0, pad_end - tm, -1).astype(jnp.int32),
                                  n_used.reshape(1)])
    tb = min(256, n)
    xs = _scatter(dest, fill_start, h1p, n_slots, min(512, n), tm)
    act = _gmm1(meta, xs, prm['w_exp1'][l], prm['b_exp1'][l].reshape(N_EXPERTS, 1, 2 * D_EXPERT),
                nb, tm, 1024)
    ys = _gmm2(meta, act, prm['w_exp2'][l], prm['b_exp2'][l].reshape(N_EXPERTS, 1, D_MODEL),
               nb, tm)
    return _combine(dest, h1, top_g, row(prm['ln2_g'][l]), row(prm['ln2_b'][l]),
                    ys, tb)


def kernel(x, ln_in_g, ln_in_b, w_in, conv_w, conv_b, w_rgate, b_rgate, w_igate, b_igate, lru_lambda, shift_mu, w0, rw_decay_up, a0, rw_aaa_up, rw_gate_up, k_k, k_a, r_k, gn_g, gn_b, w_out, ln1_g, ln1_b, w_router, b_router, w_exp1, b_exp1, w_exp2, b_exp2, ln2_g, ln2_b):
    bsz, t, d = x.shape
    prm = dict(ln_in_g=ln_in_g, ln_in_b=ln_in_b, w_in=w_in, conv_w=conv_w, conv_b=conv_b,
               w_rgate=w_rgate, b_rgate=b_rgate, w_igate=w_igate, b_igate=b_igate,
               lru_lambda=lru_lambda, shift_mu=shift_mu, w0=w0, rw_decay_up=rw_decay_up, a0=a0,
               rw_aaa_up=rw_aaa_up, rw_gate_up=rw_gate_up, k_k=k_k, k_a=k_a, r_k=r_k, gn_g=gn_g,
               gn_b=gn_b, w_out=w_out, ln1_g=ln1_g, ln1_b=ln1_b, w_router=w_router,
               b_router=b_router, w_exp1=w_exp1, b_exp1=b_exp1, w_exp2=w_exp2, b_exp2=b_exp2,
               ln2_g=ln2_g, ln2_b=ln2_b)
    out = _layer(x.reshape(bsz * t, d), 0, prm, bsz, t)
    return out.reshape(bsz, t, d)
```

```python
import functools

import jax
import jax.numpy as jnp
from jax import lax
from jax.experimental import pallas as pl
from jax.experimental.pallas import tpu as pltpu

D_MODEL = 2048
DEPTH = 1
CHUNK = 64
LRU_WIDTH = 1024
LRU_BLOCKS = 16
LRU_BLOCK_W = LRU_WIDTH // LRU_BLOCKS
CONV_WIDTH = 4
RG_C = 8.0
RWKV_WIDTH = D_MODEL - LRU_WIDTH
HEAD_SIZE = 64
RWKV_HEADS = RWKV_WIDTH // HEAD_SIZE
DECAY_LORA = 96
AAA_LORA = 96
GATE_LORA = 256
N_EXPERTS = 32
TOP_K = 4
D_EXPERT = D_MODEL
SWIGLU_LIMIT = 7.0
SWIGLU_ALPHA = 1.702
LN_EPS = 1e-5
GN_EPS = HEAD_SIZE * 1e-5
DEEPNORM_ALPHA = (2.0 * DEPTH) ** 0.25

LANE = 128
SUBLANE = 8
MXU_DIM = 256
LORA_PAD = 128
LORA_W = 2 * LORA_PAD + GATE_LORA
P_WIDTH = 2 * LRU_WIDTH + 3 * RWKV_WIDTH + LORA_W
VMEM_LIMIT = 56 * 1024 * 1024

F32 = jnp.float32
BF16 = jnp.bfloat16


def _cparams(sem):
    return pltpu.CompilerParams(dimension_semantics=sem, vmem_limit_bytes=VMEM_LIMIT)


def _layer_norm_rows(x, g, b):
    mu = jnp.mean(x, axis=-1, keepdims=True)
    xc = x - mu
    var = jnp.mean(xc * xc, axis=-1, keepdims=True)
    return xc * lax.rsqrt(var + LN_EPS) * g + b


def _softplus(z):
    return jnp.maximum(z, 0.0) + jnp.log1p(jnp.exp(-jnp.abs(z)))


def _sigmoid(z):
    return 1.0 / (1.0 + jnp.exp(-z))


def _split2(x):
    hi = x.astype(BF16)
    lo = (x - hi.astype(F32)).astype(BF16)
    return hi, lo


HALF = D_MODEL // 2
U32 = jnp.uint32
RANK_SPAN = 1 << 20


def _pack_halves(x):
    lo = lax.bitcast_convert_type(x[:, :HALF].astype(BF16).astype(F32), U32) >> 16
    hi = lax.bitcast_convert_type(x[:, HALF:].astype(BF16).astype(F32), U32) & jnp.uint32(0xFFFF0000)
    return hi | lo


def _unpack_halves(w):
    lo = lax.bitcast_convert_type(w << 16, F32)
    hi = lax.bitcast_convert_type(w & jnp.uint32(0xFFFF0000), F32)
    return lo, hi


ROW_TILES = HALF // LANE
assert ROW_TILES == SUBLANE


def _store_rows(ref, row0, packed):
    m = packed.shape[0]
    for c in range(ROW_TILES):
        ref[pl.ds(row0 * ROW_TILES + c, m, stride=ROW_TILES), :] = packed[:, c * LANE:(c + 1) * LANE]


def _load_rows(ref, row0, m):
    return jnp.concatenate(
        [ref[pl.ds(row0 * ROW_TILES + c, m, stride=ROW_TILES), :] for c in range(ROW_TILES)], axis=1)


def _row_tile(ref, row):
    return ref.at[pl.ds(pl.multiple_of(row * ROW_TILES, ROW_TILES), ROW_TILES), :]


def _head_sum(x, ones_ref):
    hi, lo = _split2(x)
    ones = ones_ref[...]
    parts = []
    for g in range(x.shape[1] // MXU_DIM):
        sl = slice(g * MXU_DIM, (g + 1) * MXU_DIM)
        parts.append(jnp.dot(hi[:, sl], ones, preferred_element_type=F32)
                     + jnp.dot(lo[:, sl], ones, preferred_element_type=F32))
    return jnp.concatenate(parts, axis=1)


def _in_proj_kernel(x_ref, g_ref, b_ref, w_ref, h_ref, p_ref, hb_ref):
    @pl.when(pl.program_id(1) == 0)
    def _():
        h = _layer_norm_rows(x_ref[...], g_ref[...], b_ref[...])
        h_ref[...] = h
        hb_ref[...] = h.astype(BF16)

    p_ref[...] = jnp.dot(hb_ref[...], w_ref[...], preferred_element_type=F32)


def _in_proj(x2, g, b, w_bf16, tm, tn):
    n = x2.shape[0]
    return pl.pallas_call(
        _in_proj_kernel,
        grid=(n // tm, P_WIDTH // tn),
        in_specs=[
            pl.BlockSpec((tm, D_MODEL), lambda i, j: (i, 0)),
            pl.BlockSpec((1, D_MODEL), lambda i, j: (0, 0)),
            pl.BlockSpec((1, D_MODEL), lambda i, j: (0, 0)),
            pl.BlockSpec((D_MODEL, tn), lambda i, j: (0, j)),
        ],
        out_specs=[
            pl.BlockSpec((tm, D_MODEL), lambda i, j: (i, 0)),
            pl.BlockSpec((tm, tn), lambda i, j: (i, j)),
        ],
        out_shape=[
            jax.ShapeDtypeStruct((n, D_MODEL), F32),
            jax.ShapeDtypeStruct((n, P_WIDTH), F32),
        ],
        scratch_shapes=[pltpu.VMEM((tm, D_MODEL), BF16)],
        compiler_params=_cparams(("parallel", "arbitrary")),
        name="in_proj",
    )(x2, g, b, w_bf16)


def _lru_kernel(u_ref, gi_ref, cw_ref, cb_ref, wr_ref, br_ref, wi_ref, bi_ref,
                lam_ref, o_ref, ubuf, a_s, b_s, carry, *, tt):
    first = pl.program_id(1) == 0

    @pl.when(first)
    def _():
        ubuf[0:SUBLANE, :] = jnp.zeros((SUBLANE, LRU_WIDTH), F32)
        carry[...] = jnp.zeros_like(carry)

    @pl.when(jnp.logical_not(first))
    def _():
        ubuf[0:SUBLANE, :] = ubuf[tt:tt + SUBLANE, :]

    ubuf[SUBLANE:, :] = u_ref[...]
    uc = cb_ref[...]
    for i in range(CONV_WIDTH):
        off = SUBLANE - (CONV_WIDTH - 1) + i
        uc = uc + cw_ref[i:i + 1, :] * ubuf[off:off + tt, :]

    ucb = uc.astype(BF16)
    n_grp = LRU_WIDTH // MXU_DIM

    def gate(w_ref, bias_ref):
        parts = [jnp.dot(ucb[:, g * MXU_DIM:(g + 1) * MXU_DIM], w_ref[g],
                         preferred_element_type=F32) for g in range(n_grp)]
        return _sigmoid(jnp.concatenate(parts, axis=1) + bias_ref[...])

    r_gate = gate(wr_ref, br_ref)
    i_gate = gate(wi_ref, bi_ref)
    log_a = (-RG_C * r_gate) * _softplus(-lam_ref[...])
    a_s[...] = jnp.exp(log_a)
    th = jnp.tanh(log_a)
    b_s[...] = jnp.sqrt(-2.0 * th / (1.0 - th)) * (i_gate * uc)

    row = lax.broadcasted_iota(jnp.int32, (SUBLANE, LRU_WIDTH), 0)

    def group(gidx, c):
        off = pl.multiple_of(gidx * SUBLANE, SUBLANE)
        a = a_s[pl.ds(off, SUBLANE), :]
        b = b_s[pl.ds(off, SUBLANE), :]
        for d in (1, 2, 4):
            keep = row >= d
            a_sh = pltpu.roll(a, d, axis=0)
            b_sh = pltpu.roll(b, d, axis=0)
            b = jnp.where(keep, a * b_sh + b, b)
            a = jnp.where(keep, a * a_sh, a)
        h = a * c + b
        b_s[pl.ds(off, SUBLANE), :] = h
        return h[SUBLANE - 1:SUBLANE, :]

    carry[...] = lax.fori_loop(0, tt // SUBLANE, group, carry[...])
    o_ref[...] = (b_s[...] * jax.nn.gelu(gi_ref[...])).astype(o_ref.dtype)


def _lru(p, conv_w, conv_b, wr_bd, br, wi_bd, bi, lam, bsz, t, tt):
    n = bsz * t
    nt = t // tt
    vec = lambda: pl.BlockSpec((1, LRU_WIDTH), lambda b, i: (0, 0))
    wspec = lambda: pl.BlockSpec((LRU_WIDTH // MXU_DIM, MXU_DIM, MXU_DIM), lambda b, i: (0, 0, 0))
    return pl.pallas_call(
        functools.partial(_lru_kernel, tt=tt),
        grid=(bsz, nt),
        in_specs=[
            pl.BlockSpec((tt, LRU_WIDTH), lambda b, i: (b * nt + i, 0)),
            pl.BlockSpec((tt, LRU_WIDTH), lambda b, i: (b * nt + i, 1)),
            pl.BlockSpec((CONV_WIDTH, LRU_WIDTH), lambda b, i: (0, 0)),
            vec(), wspec(), vec(), wspec(), vec(), vec(),
        ],
        out_specs=pl.BlockSpec((tt, LRU_WIDTH), lambda b, i: (b * nt + i, 0)),
        out_shape=jax.ShapeDtypeStruct((n, LRU_WIDTH), BF16),
        scratch_shapes=[
            pltpu.VMEM((tt + SUBLANE, LRU_WIDTH), F32),
            pltpu.VMEM((tt, LRU_WIDTH), F32),
            pltpu.VMEM((tt, LRU_WIDTH), F32),
            pltpu.VMEM((1, LRU_WIDTH), F32),
        ],
        compiler_params=_cparams(("parallel", "arbitrary")),
        name="lru",
    )(p, p, conv_w, conv_b, wr_bd, br, wi_bd, bi, lam)


def _rw_prep_kernel(pr_ref, pk_ref, pv_ref, pa_ref, mur_ref, muk_ref, muv_ref, mua_ref,
                    w0_ref, wdec_ref, a0_ref, waaa_ref, wgate_ref, kk_ref, ka_ref, ones_ref,
                    r_out, k_out, v_out, kkn_out, b_out, cl_out, g_out,
                    prev_r, prev_k, prev_v, prev_a, *, tt):
    first = pl.program_id(1) == 0

    @pl.when(first)
    def _():
        for ref in (prev_r, prev_k, prev_v, prev_a):
            ref[...] = jnp.zeros_like(ref)

    def shift(x_ref, prev_ref, mu_ref):
        x = x_ref[...]
        row = lax.broadcasted_iota(jnp.int32, x.shape, 0)
        prev = jnp.where(row == 0, prev_ref[...], pltpu.roll(x, 1, axis=0))
        prev_ref[...] = x[tt - 1:tt, :]
        return x + (prev - x) * mu_ref[...]

    r = shift(pr_ref, prev_r, mur_ref)
    k = shift(pk_ref, prev_k, muk_ref)
    v = shift(pv_ref, prev_v, muv_ref)
    ad = shift(pa_ref, prev_a, mua_ref)

    wd = jnp.tanh(ad[:, 0:LORA_PAD]).astype(BF16)
    aa = ad[:, LORA_PAD:2 * LORA_PAD].astype(BF16)
    gd = _sigmoid(ad[:, 2 * LORA_PAD:]).astype(BF16)
    w_pre = w0_ref[...] + jnp.dot(wd, wdec_ref[...], preferred_element_type=F32)
    w_log = -_softplus(-w_pre) - 0.5
    a = _sigmoid(a0_ref[...] + jnp.dot(aa, waaa_ref[...], preferred_element_type=F32))
    g = jnp.dot(gd, wgate_ref[...], preferred_element_type=F32)

    kk = k * kk_ref[...]
    norm = jnp.sqrt(_head_sum(kk * kk, ones_ref))
    kk = kk / jnp.maximum(norm, 1e-12)

    lw = -jnp.exp(w_log)
    ri = lax.broadcasted_iota(jnp.int32, (tt, tt), 0)
    ci = lax.broadcasted_iota(jnp.int32, (tt, tt), 1)
    tri = jnp.logical_and(ri // CHUNK == ci // CHUNK, ri >= ci).astype(BF16)
    lw_hi = lw.astype(BF16)
    rem = lw - lw_hi.astype(F32)
    lw_mid = rem.astype(BF16)
    lw_lo = (rem - lw_mid.astype(F32)).astype(BF16)
    cl = (jnp.dot(tri, lw_hi, preferred_element_type=F32)
          + jnp.dot(tri, lw_mid, preferred_element_type=F32)
          + jnp.dot(tri, lw_lo, preferred_element_type=F32))

    r_out[...] = r
    k_out[...] = k * (1.0 + (a - 1.0) * ka_ref[...])
    v_out[...] = v
    kkn_out[...] = kk
    b_out[...] = kk * a
    cl_out[...] = cl
    g_out[...] = g


def _rw_prep(p, mu_r, mu_k, mu_v, mu_a, w0, wdec, a0, waaa, wgate, k_k, k_a, ones_bd, bsz, t, tt):
    n = bsz * t
    nt = t // tt
    cb = 2 * LRU_WIDTH // RWKV_WIDTH
    row = lambda c: pl.BlockSpec((tt, RWKV_WIDTH), lambda b, i: (b * nt + i, c))
    vec = lambda w: pl.BlockSpec((1, w), lambda b, i: (0, 0))
    full = lambda s: pl.BlockSpec(s, lambda b, i: (0, 0))
    lora_cb = (2 * LRU_WIDTH + 3 * RWKV_WIDTH) // LORA_W
    out = jax.ShapeDtypeStruct((n, RWKV_WIDTH), F32)
    return pl.pallas_call(
        functools.partial(_rw_prep_kernel, tt=tt),
        grid=(bsz, nt),
        in_specs=[
            row(cb), row(cb + 1), row(cb + 2),
            pl.BlockSpec((tt, LORA_W), lambda b, i: (b * nt + i, lora_cb)),
            vec(RWKV_WIDTH), vec(RWKV_WIDTH), vec(RWKV_WIDTH), vec(LORA_W),
            vec(RWKV_WIDTH), full((LORA_PAD, RWKV_WIDTH)),
            vec(RWKV_WIDTH), full((LORA_PAD, RWKV_WIDTH)),
            full((GATE_LORA, RWKV_WIDTH)),
            vec(RWKV_WIDTH), vec(RWKV_WIDTH),
            full((MXU_DIM, MXU_DIM)),
        ],
        out_specs=[pl.BlockSpec((tt, RWKV_WIDTH), lambda b, i: (b * nt + i, 0))] * 7,
        out_shape=[out] * 7,
        scratch_shapes=[pltpu.VMEM((1, RWKV_WIDTH), F32)] * 3 + [pltpu.VMEM((1, LORA_W), F32)],
        compiler_params=_cparams(("parallel", "arbitrary")),
        name="rw_prep",
    )(p, p, p, p, mu_r, mu_k, mu_v, mu_a, w0, wdec, a0, waaa, wgate, k_k, k_a, ones_bd)


def _rw_chunk_kernel(r_ref, k_ref, v_ref, kk_ref, b_ref, cl_ref, y_ref, *s_refs):
    c = CHUNK
    hs = HEAD_SIZE
    n_seq = r_ref.shape[0]

    @pl.when(pl.program_id(1) == 0)
    def _():
        for s_ref in s_refs:
            s_ref[...] = jnp.zeros_like(s_ref)

    r2 = lax.broadcasted_iota(jnp.int32, (2 * c, 2 * c), 0)
    c2 = lax.broadcasted_iota(jnp.int32, (2 * c, 2 * c), 1)
    tq = jnp.where(r2 >= c, r2 - c, r2)
    tk = jnp.where(c2 >= c, c2 - c, c2)
    mask = tk < tq + jnp.where(r2 >= c, 1, 0)
    zeros_cc = jnp.zeros((c, hs), BF16)
    ri_c = lax.broadcasted_iota(jnp.int32, (c, c), 0)
    ci_c = lax.broadcasted_iota(jnp.int32, (c, c), 1)

    dn_t = (((1,), (1,)), ((), ()))
    dn_l = (((0,), (0,)), ((), ()))

    row_w = lax.broadcasted_iota(jnp.int32, (c, RWKV_WIDTH), 0)

    def scaled_operands(q):
        cl = cl_ref[q]
        cl_ex = jnp.where(row_w == 0, 0.0, pltpu.roll(cl, 1, axis=0))
        cl_last = cl[c - 1:c, :]
        g_inv = jnp.exp(-cl)
        g_dec = jnp.exp(cl_last - cl)
        bv = b_ref[q]
        kx = k_ref[q]
        rg = r_ref[q] * jnp.exp(cl)
        return dict(
            g_last=jnp.exp(cl_last),
            v=v_ref[q].astype(BF16),
            rg=rg,
            lhs_a=(-kk_ref[q] * jnp.exp(cl_ex)).astype(BF16),
            lhs_r=rg.astype(BF16),
            rhs_b=(bv * g_inv).astype(BF16),
            rhs_k=(kx * g_inv).astype(BF16),
            dec_b=(bv * g_dec).astype(BF16),
            dec_k=(kx * g_dec).astype(BF16),
        )

    ops = [scaled_operands(q) for q in range(n_seq)]

    chains = [(q, h) for q in range(n_seq) for h in range(RWKV_HEADS)]
    sl_of = lambda h: slice(h * hs, (h + 1) * hs)
    s0s = [s_refs[q * RWKV_HEADS + h][...] for q, h in chains]
    aas = []
    for q, h in chains:
        o, sl = ops[q], sl_of(h)
        lhs = jnp.concatenate([o['lhs_a'][:, sl], o['lhs_r'][:, sl]], axis=0)
        rhs = jnp.concatenate([o['rhs_b'][:, sl], o['rhs_k'][:, sl]], axis=0)
        aa = lax.dot_general(lhs, rhs, dn_t, preferred_element_type=F32)
        aas.append(jnp.where(mask, aa, 0.0))
    v_hs = [ops[q]['v'][:, sl_of(h)] for q, h in chains]
    a_rs = [aa[c:, :].astype(BF16) for aa in aas]
    ps = [aa[:c, :c] for aa in aas]
    x0s = []
    for i, (q, h) in enumerate(chains):
        akv = jnp.dot(aas[i][:c, c:].astype(BF16), v_hs[i], preferred_element_type=F32)
        x0s.append(jnp.concatenate([ops[q]['lhs_a'][:, sl_of(h)], akv.astype(BF16)], axis=1))

    def mm(a, b):
        return jnp.dot(a.astype(BF16), b.astype(BF16), preferred_element_type=F32)

    blk = 4
    diag = (ri_c // blk) == (ci_c // blk)
    l4s = [jnp.where(diag, p, 0.0) for p in ps]
    p2s = [mm(l4, l4) for l4 in l4s]
    ts = [jnp.where(ri_c == ci_c, 1.0, l4) for l4 in l4s]
    ts = [t + mm(t, p2) for t, p2 in zip(ts, p2s)]
    s = blk
    while s < c:
        off = jnp.logical_and((ri_c // (2 * s)) == (ci_c // (2 * s)),
                              jnp.logical_and((ri_c // s) % 2 == 1, (ci_c // s) % 2 == 0))
        ys = [mm(jnp.where(off, p, 0.0), t) for p, t in zip(ps, ts)]
        ts = [t + mm(t, y) for t, y in zip(ts, ys)]
        s *= 2
    xs = [jnp.dot(t.astype(BF16), x0, preferred_element_type=F32) for t, x0 in zip(ts, x0s)]
    for i, (q, h) in enumerate(chains):
        o, sl = ops[q], sl_of(h)
        rhs2 = jnp.concatenate(
            [xs[i].astype(BF16), jnp.concatenate([zeros_cc, v_hs[i]], axis=1)], axis=0)
        qy = jnp.dot(a_rs[i], rhs2, preferred_element_type=F32)
        dec = jnp.concatenate([o['dec_b'][:, sl], o['dec_k'][:, sl]], axis=0)
        mnt = lax.dot_general(rhs2, dec, dn_l, preferred_element_type=F32)
        s0b = s0s[i].astype(BF16)
        qq = (o['rg'][:, sl] + qy[:, :hs]).astype(BF16)
        y_ref[q, :, sl] = lax.dot_general(qq, s0b, dn_t, preferred_element_type=F32) + qy[:, hs:]
        s_refs[i][...] = (s0s[i] * o['g_last'][:, sl]
                          + jnp.dot(s0b, mnt[:hs, :].astype(BF16), preferred_element_type=F32)
                          + mnt[hs:, :])


def _rw_chunk(r, k, v, kk, b, cl, bsz, t, n_seq):
    nc = t // CHUNK
    shape3 = (bsz, t, RWKV_WIDTH)
    spec = lambda: pl.BlockSpec((n_seq, CHUNK, RWKV_WIDTH), lambda bb, i: (bb, i, 0))
    y = pl.pallas_call(
        _rw_chunk_kernel,
        grid=(bsz // n_seq, nc),
        in_specs=[spec() for _ in range(6)],
        out_specs=spec(),
        out_shape=jax.ShapeDtypeStruct(shape3, F32),
        scratch_shapes=[pltpu.VMEM((HEAD_SIZE, HEAD_SIZE), F32)
                        for _ in range(n_seq * RWKV_HEADS)],
        compiler_params=_cparams(("parallel", "arbitrary")),
        name="rw_chunk",
    )(*[a.reshape(shape3) for a in (r, k, v, kk, b, cl)])
    return y.reshape(bsz * t, RWKV_WIDTH)


def _rw_post_kernel(y_ref, r_ref, k_ref, v_ref, g_ref, rk_ref, gg_ref, gb_ref, ones_ref, o_ref):
    y = y_ref[...]
    inv = 1.0 / HEAD_SIZE
    mu = _head_sum(y, ones_ref) * inv
    yc = y - mu
    var = _head_sum(yc * yc, ones_ref) * inv
    yn = yc * lax.rsqrt(var + GN_EPS) * gg_ref[...] + gb_ref[...]
    bonus = _head_sum(r_ref[...] * k_ref[...] * rk_ref[...], ones_ref) * v_ref[...]
    o_ref[...] = ((yn + bonus) * g_ref[...]).astype(o_ref.dtype)


def _rw_post(y, r, k, v, g, r_k, gn_g, gn_b, ones_bd, tt):
    n = y.shape[0]
    row = lambda: pl.BlockSpec((tt, RWKV_WIDTH), lambda i: (i, 0))
    vec = lambda: pl.BlockSpec((1, RWKV_WIDTH), lambda i: (0, 0))
    return pl.pallas_call(
        _rw_post_kernel,
        grid=(n // tt,),
        in_specs=[row(), row(), row(), row(), row(), vec(), vec(), vec(),
                  pl.BlockSpec((MXU_DIM, MXU_DIM), lambda i: (0, 0))],
        out_specs=row(),
        out_shape=jax.ShapeDtypeStruct((n, RWKV_WIDTH), BF16),
        compiler_params=_cparams(("parallel",)),
        name="rw_post",
    )(y, r, k, v, g, r_k, gn_g, gn_b, ones_bd)


def _out_proj_kernel(yl_ref, yr_ref, h0_ref, wo_ref, g_ref, b_ref, wr_ref, br_ref,
                     h1_ref, h1p_ref, code_ref, tg_ref, cnt_ref, carry, *, tm):
    @pl.when(pl.program_id(0) == 0)
    def _():
        carry[...] = jnp.zeros_like(carry)

    mix = jnp.dot(jnp.concatenate([yl_ref[...], yr_ref[...]], axis=1), wo_ref[...],
                  preferred_element_type=F32)
    h1 = _layer_norm_rows(DEEPNORM_ALPHA * h0_ref[...] + mix, g_ref[...], b_ref[...])
    h1_ref[...] = h1
    _store_rows(h1p_ref, 0, _pack_halves(h1))

    h_hi, h_lo = _split2(h1)
    both = jnp.dot(h_hi, wr_ref[...], preferred_element_type=F32)
    logits = (both[:, :N_EXPERTS] + both[:, N_EXPERTS:]
              + jnp.dot(h_lo, wr_ref[:, 0:N_EXPERTS], preferred_element_type=F32)) + br_ref[...]

    lane = lax.broadcasted_iota(jnp.int32, (tm, N_EXPERTS), 1).astype(F32)
    lane4 = lax.broadcasted_iota(jnp.int32, (tm, TOP_K), 1)
    work = logits
    vals, idxs, sels = [], [], []
    for _ in range(TOP_K):
        m = jnp.max(work, axis=-1, keepdims=True)
        idx = jnp.min(jnp.where(work == m, lane, float(N_EXPERTS)), axis=-1, keepdims=True)
        sel = lane == idx
        vals.append(m)
        idxs.append(idx)
        sels.append(sel)
        work = jnp.where(sel, -jnp.inf, work)
    exps = [jnp.exp(vv - vals[0]) for vv in vals]
    denom = exps[0] + exps[1] + exps[2] + exps[3]

    member = jnp.zeros((tm, N_EXPERTS), F32)
    for sel in sels:
        member = member + sel.astype(F32)
    ri = lax.broadcasted_iota(jnp.int32, (tm, tm), 0)
    ci = lax.broadcasted_iota(jnp.int32, (tm, tm), 1)
    before = (ci < ri).astype(BF16)
    rank_full = jnp.dot(before, member.astype(BF16), preferred_element_type=F32) + carry[...]
    carry[...] = carry[...] + jnp.sum(member, axis=0, keepdims=True)
    cnt_ref[...] = carry[...].astype(jnp.int32)

    code = jnp.zeros((tm, TOP_K), jnp.int32)
    tg = jnp.zeros((tm, TOP_K), F32)
    for kq in range(TOP_K):
        rank_k = jnp.sum(jnp.where(sels[kq], rank_full, 0.0), axis=-1, keepdims=True)
        code_k = idxs[kq].astype(jnp.int32) * RANK_SPAN + rank_k.astype(jnp.int32)
        code = jnp.where(lane4 == kq, code_k, code)
        tg = jnp.where(lane4 == kq, exps[kq] / denom, tg)
    code_ref[...] = code
    tg_ref[...] = tg


def _out_proj(y_lru, y_rw, h0, w_out_bf16, g, b, w_router_split, b_router, tm):
    n = h0.shape[0]
    row = lambda w: pl.BlockSpec((tm, w), lambda i: (i, 0))
    vec = lambda w: pl.BlockSpec((1, w), lambda i: (0, 0))
    return pl.pallas_call(
        functools.partial(_out_proj_kernel, tm=tm),
        grid=(n // tm,),
        in_specs=[
            row(LRU_WIDTH), row(RWKV_WIDTH), row(D_MODEL),
            pl.BlockSpec((D_MODEL, D_MODEL), lambda i: (0, 0)),
            vec(D_MODEL), vec(D_MODEL),
            pl.BlockSpec((D_MODEL, 2 * N_EXPERTS), lambda i: (0, 0)),
            vec(N_EXPERTS),
        ],
        out_specs=[row(D_MODEL), pl.BlockSpec((tm * ROW_TILES, LANE), lambda i: (i, 0)),
                   row(TOP_K), row(TOP_K), vec(N_EXPERTS)],
        out_shape=[
            jax.ShapeDtypeStruct((n, D_MODEL), F32),
            jax.ShapeDtypeStruct((n * ROW_TILES, LANE), U32),
            jax.ShapeDtypeStruct((n, TOP_K), jnp.int32),
            jax.ShapeDtypeStruct((n, TOP_K), F32),
            jax.ShapeDtypeStruct((1, N_EXPERTS), jnp.int32),
        ],
        scratch_shapes=[pltpu.VMEM((1, N_EXPERTS), F32)],
        compiler_params=_cparams(("arbitrary",)),
        name="out_proj",
    )(y_lru, y_rw, h0, w_out_bf16, g, b, w_router_split, b_router)


def _scatter_kernel(dest_ref, fill_ref, h_ref, xs_ref, zbuf, sem, zsem, *, tb, tm, nb):
    @pl.when(pl.program_id(0) == 0)
    def _():
        zbuf[...] = jnp.zeros_like(zbuf)

        def fill_copy(start):
            start = pl.multiple_of(start * ROW_TILES, SUBLANE)
            return pltpu.make_async_copy(zbuf, xs_ref.at[pl.ds(start, tm * ROW_TILES), :], zsem)

        def fill(e, carry):
            @pl.when(fill_ref[e] >= 0)
            def _():
                fill_copy(fill_ref[e]).start()
            return carry

        def fill_wait(e, carry):
            @pl.when(fill_ref[e] >= 0)
            def _():
                fill_copy(fill_ref[e]).wait()
            return carry

        def tail(blk, carry):
            fill_copy(blk * tm).start()
            return carry

        def tail_wait(blk, carry):
            fill_copy(blk * tm).wait()
            return carry

        n_used = fill_ref[N_EXPERTS]
        lax.fori_loop(0, N_EXPERTS, fill, 0)
        lax.fori_loop(n_used, nb, tail, 0)
        lax.fori_loop(0, N_EXPERTS, fill_wait, 0)
        lax.fori_loop(n_used, nb, tail_wait, 0)

    base = pl.program_id(0) * (tb * TOP_K)

    def copy(t, kq):
        slot = dest_ref[base + t * TOP_K + kq]
        return pltpu.make_async_copy(_row_tile(h_ref, t), _row_tile(xs_ref, slot), sem)

    def issue(t, carry):
        for kq in range(TOP_K):
            copy(t, kq).start(priority=kq % 2)
        return carry

    lax.fori_loop(0, tb, issue, 0, unroll=4)
    for kq in range(TOP_K):
        pltpu.make_async_copy(h_ref, xs_ref.at[pl.ds(0, tb * ROW_TILES), :], sem).wait()


def _scatter(dest_flat, fill_start, h1p, n_slots, tb, tm):
    n = h1p.shape[0] // ROW_TILES
    nb = n_slots // tm
    grid_spec = pltpu.PrefetchScalarGridSpec(
        num_scalar_prefetch=2,
        grid=(n // tb,),
        in_specs=[pl.BlockSpec((tb * ROW_TILES, LANE), lambda i, d, f: (i, 0))],
        out_specs=pl.BlockSpec(memory_space=pl.ANY),
        scratch_shapes=[pltpu.VMEM((tm * ROW_TILES, LANE), U32), pltpu.SemaphoreType.DMA(()),
                        pltpu.SemaphoreType.DMA(())],
    )
    return pl.pallas_call(
        functools.partial(_scatter_kernel, tb=tb, tm=tm, nb=nb),
        grid_spec=grid_spec,
        out_shape=jax.ShapeDtypeStruct((n_slots * ROW_TILES, LANE), U32),
        compiler_params=_cparams(("arbitrary",)),
        name="scatter",
    )(dest_flat, fill_start, h1p)


def _gmm1_kernel(meta_ref, x_ref, wg_ref, wu_ref, bg_ref, bu_ref, o_ref, wgb, wub, *, nb, tm):
    i = pl.program_id(1)
    e = meta_ref[i]
    prev_e = meta_ref[jnp.maximum(i - 1, 0)]

    @pl.when(jnp.logical_or(i == 0, e != prev_e))
    def _():
        wgb[...] = wg_ref[0].astype(BF16)
        wub[...] = wu_ref[0].astype(BF16)

    @pl.when(i < meta_ref[nb])
    def _():
        x_lo, x_hi = _unpack_halves(_load_rows(x_ref, 0, tm))
        x_lo = x_lo.astype(BF16)
        x_hi = x_hi.astype(BF16)
        gate = (jnp.dot(x_lo, wgb[0:HALF, :], preferred_element_type=F32)
                + jnp.dot(x_hi, wgb[HALF:, :], preferred_element_type=F32) + bg_ref[0])
        up = (jnp.dot(x_lo, wub[0:HALF, :], preferred_element_type=F32)
              + jnp.dot(x_hi, wub[HALF:, :], preferred_element_type=F32) + bu_ref[0])
        gate = jnp.minimum(gate, SWIGLU_LIMIT)
        up = jnp.clip(up, -SWIGLU_LIMIT, SWIGLU_LIMIT)
        act = gate * _sigmoid(SWIGLU_ALPHA * gate) * (up + 1.0)
        o_ref[...] = act.astype(o_ref.dtype)

    @pl.when(i >= meta_ref[nb])
    def _():
        o_ref[...] = jnp.zeros_like(o_ref)


def _gmm1(meta, xs, w1, b1, nb, tm, tn):
    n_slots = xs.shape[0] // ROW_TILES
    nj = D_EXPERT // tn

    def row_blk(j, i, m):
        return jnp.minimum(i, m[nb] - 1)

    grid_spec = pltpu.PrefetchScalarGridSpec(
        num_scalar_prefetch=1,
        grid=(nj, nb),
        in_specs=[
            pl.BlockSpec((tm * ROW_TILES, LANE), lambda j, i, m: (row_blk(j, i, m), 0)),
            pl.BlockSpec((1, D_MODEL, tn), lambda j, i, m: (m[i], 0, j)),
            pl.BlockSpec((1, D_MODEL, tn), lambda j, i, m: (m[i], 0, j + nj)),
            pl.BlockSpec((1, 1, tn), lambda j, i, m: (m[i], 0, j)),
            pl.BlockSpec((1, 1, tn), lambda j, i, m: (m[i], 0, j + nj)),
        ],
        out_specs=pl.BlockSpec((tm, tn), lambda j, i, m: (i, j)),
        scratch_shapes=[pltpu.VMEM((D_MODEL, tn), BF16), pltpu.VMEM((D_MODEL, tn), BF16)],
    )
    return pl.pallas_call(
        functools.partial(_gmm1_kernel, nb=nb, tm=tm),
        grid_spec=grid_spec,
        out_shape=jax.ShapeDtypeStruct((n_slots, D_EXPERT), BF16),
        compiler_params=_cparams(("arbitrary", "arbitrary")),
        name="gmm1",
    )(meta, xs, w1, w1, b1, b1)


def _gmm2_kernel(meta_ref, a_ref, w_ref, b_ref, o_ref, wb, *, nb):
    i = pl.program_id(1)
    e = meta_ref[i]
    prev_e = meta_ref[jnp.maximum(i - 1, 0)]

    @pl.when(jnp.logical_or(i == 0, e != prev_e))
    def _():
        wb[...] = w_ref[0].astype(BF16)

    @pl.when(i < meta_ref[nb])
    def _():
        out = jnp.dot(a_ref[...], wb[...], preferred_element_type=F32) + b_ref[0]
        _store_rows(o_ref, 0, _pack_halves(out))

    @pl.when(i >= meta_ref[nb])
    def _():
        o_ref[...] = jnp.zeros_like(o_ref)


def _gmm2(meta, act, w2, b2, nb, tm):
    n_slots = act.shape[0]

    def row_blk(j, i, m):
        return jnp.minimum(i, m[nb] - 1)

    grid_spec = pltpu.PrefetchScalarGridSpec(
        num_scalar_prefetch=1,
        grid=(1, nb),
        in_specs=[
            pl.BlockSpec((tm, D_EXPERT), lambda j, i, m: (row_blk(j, i, m), 0)),
            pl.BlockSpec((1, D_EXPERT, D_MODEL), lambda j, i, m: (m[i], 0, 0)),
            pl.BlockSpec((1, 1, D_MODEL), lambda j, i, m: (m[i], 0, 0)),
        ],
        out_specs=pl.BlockSpec((tm * ROW_TILES, LANE), lambda j, i, m: (i, 0)),
        scratch_shapes=[pltpu.VMEM((D_EXPERT, D_MODEL), BF16)],
    )
    return pl.pallas_call(
        functools.partial(_gmm2_kernel, nb=nb),
        grid_spec=grid_spec,
        out_shape=jax.ShapeDtypeStruct((n_slots * ROW_TILES, LANE), U32),
        compiler_params=_cparams(("arbitrary", "arbitrary")),
        name="gmm2",
    )(meta, act, w2, b2)


def _combine_kernel(dest_ref, h1_ref, tg_ref, g_ref, b_ref, ys_ref, o_ref, buf, sems, *, tb, n_steps):
    i = pl.program_id(0)

    def issue(step, half):
        base = step * (tb * TOP_K)

        def body(t, carry):
            for kq in range(TOP_K):
                slot = dest_ref[base + t * TOP_K + kq]
                pltpu.make_async_copy(_row_tile(ys_ref, slot), _row_tile(buf.at[half, kq], t),
                                      sems.at[half]).start(priority=kq % 2)
            return carry

        lax.fori_loop(0, tb, body, 0, unroll=4)

    @pl.when(i == 0)
    def _():
        issue(0, 0)

    @pl.when(i + 1 < n_steps)
    def _():
        issue(i + 1, (i + 1) % 2)

    half = i % 2
    for kq in range(TOP_K):
        pltpu.make_async_copy(ys_ref.at[pl.ds(0, tb * ROW_TILES), :], buf.at[half, kq],
                              sems.at[half]).wait()

    tg = tg_ref[...]
    y = DEEPNORM_ALPHA * h1_ref[...]
    for kq in range(TOP_K):
        lo, hi = _unpack_halves(_load_rows(buf.at[half, kq], 0, tb))
        y = y + tg[:, kq:kq + 1] * jnp.concatenate([lo, hi], axis=1)
    o_ref[...] = _layer_norm_rows(y, g_ref[...], b_ref[...])


def _combine(dest_flat, h1, tg, g, b, ys, tb):
    n = h1.shape[0]
    grid_spec = pltpu.PrefetchScalarGridSpec(
        num_scalar_prefetch=1,
        grid=(n // tb,),
        in_specs=[
            pl.BlockSpec((tb, D_MODEL), lambda i, d: (i, 0)),
            pl.BlockSpec((tb, TOP_K), lambda i, d: (i, 0)),
            pl.BlockSpec((1, D_MODEL), lambda i, d: (0, 0)),
            pl.BlockSpec((1, D_MODEL), lambda i, d: (0, 0)),
            pl.BlockSpec(memory_space=pl.ANY),
        ],
        out_specs=pl.BlockSpec((tb, D_MODEL), lambda i, d: (i, 0)),
        scratch_shapes=[pltpu.VMEM((2, TOP_K, tb * ROW_TILES, LANE), U32),
                        pltpu.SemaphoreType.DMA((2,))],
    )
    return pl.pallas_call(
        functools.partial(_combine_kernel, tb=tb, n_steps=n // tb),
        grid_spec=grid_spec,
        out_shape=jax.ShapeDtypeStruct((n, D_MODEL), F32),
        compiler_params=_cparams(("arbitrary",)),
        name="combine",
    )(dest_flat, h1, tg, g, b, ys)


def _block_diag(w, group):
    nb = w.shape[0] // group
    w = w.reshape(nb, group, LRU_BLOCK_W, LRU_BLOCK_W)
    eye = jnp.eye(group, dtype=w.dtype)
    out = jnp.einsum('gajk,ab->gajbk', w, eye)
    return out.reshape(nb, group * LRU_BLOCK_W, group * LRU_BLOCK_W)


def _pad_rows(w, rows):
    return jnp.pad(w, ((0, rows - w.shape[0]), (0, 0)))


def _layer(h_in_x, l, prm, bsz, t):
    n = bsz * t
    row = lambda a: a.reshape(1, -1)

    w_in = prm['w_in'][l]
    o_rw = 2 * LRU_WIDTH
    o_l = o_rw + 3 * RWKV_WIDTH
    pad_c = lambda w: jnp.pad(w, ((0, 0), (0, LORA_PAD - w.shape[1])))
    w_in_p = jnp.concatenate([
        w_in[:, :o_l],
        pad_c(w_in[:, o_l:o_l + DECAY_LORA]),
        pad_c(w_in[:, o_l + DECAY_LORA:o_l + DECAY_LORA + AAA_LORA]),
        w_in[:, o_l + DECAY_LORA + AAA_LORA:],
    ], axis=1).astype(BF16)
    mu = prm['shift_mu'][l]
    pad_v = lambda v: jnp.pad(v, (0, LORA_PAD - v.shape[0]))
    mu_l = mu[3 * RWKV_WIDTH:]
    mu_a = jnp.concatenate([pad_v(mu_l[:DECAY_LORA]), pad_v(mu_l[DECAY_LORA:DECAY_LORA + AAA_LORA]),
                            mu_l[DECAY_LORA + AAA_LORA:]])

    tm_in = min(1024, n)
    h0, p = _in_proj(h_in_x, row(prm['ln_in_g']), row(prm['ln_in_b']), w_in_p, tm_in, 512)

    tt = min(1024, t)
    group = MXU_DIM // LRU_BLOCK_W
    y_lru = _lru(p, prm['conv_w'][l], row(prm['conv_b'][l]),
                 _block_diag(prm['w_rgate'][l], group).astype(BF16), row(prm['b_rgate'][l]),
                 _block_diag(prm['w_igate'][l], group).astype(BF16), row(prm['b_igate'][l]),
                 row(prm['lru_lambda'][l]), bsz, t, tt)

    head_id = jnp.arange(MXU_DIM) // HEAD_SIZE
    ones_bd = (head_id[:, None] == head_id[None, :]).astype(BF16)
    tt_rw = min(256, t)
    r, k, v, kk, bvec, cl, g = _rw_prep(
        p, row(mu[:RWKV_WIDTH]), row(mu[RWKV_WIDTH:2 * RWKV_WIDTH]),
        row(mu[2 * RWKV_WIDTH:3 * RWKV_WIDTH]), row(mu_a),
        row(prm['w0'][l]), _pad_rows(prm['rw_decay_up'][l], LORA_PAD).astype(BF16),
        row(prm['a0'][l]), _pad_rows(prm['rw_aaa_up'][l], LORA_PAD).astype(BF16),
        prm['rw_gate_up'][l].astype(BF16), row(prm['k_k'][l]), row(prm['k_a'][l]),
        ones_bd, bsz, t, tt_rw)
    y = _rw_chunk(r, k, v, kk, bvec, cl, bsz, t, 4 if bsz % 4 == 0 else 1)
    y_rw = _rw_post(y, r, k, v, g, row(prm['r_k'][l]), row(prm['gn_g'][l]), row(prm['gn_b'][l]),
                    ones_bd, min(1024, n))

    w_r = prm['w_router'][l]
    w_r_hi = w_r.astype(BF16)
    w_r_lo = (w_r - w_r_hi.astype(F32)).astype(BF16)
    tm_out = min(512, n)
    assert n < RANK_SPAN
    h1, h1p, code, top_g, counts = _out_proj(
        y_lru, y_rw, h0, prm['w_out'][l].astype(BF16), row(prm['ln1_g'][l]), row(prm['ln1_b'][l]),
        jnp.concatenate([w_r_hi, w_r_lo], axis=1), row(prm['b_router'][l]), tm_out)

    tm = 512
    counts = counts.reshape(N_EXPERTS)
    padded = ((counts + tm - 1) // tm) * tm
    pad_end = jnp.cumsum(padded)
    pad_start = pad_end - padded
    code = code.reshape(n * TOP_K // LANE, LANE)
    dest = (pad_start[code // RANK_SPAN] + code % RANK_SPAN).reshape(-1).astype(jnp.int32)
    nb = (n * TOP_K) // tm + N_EXPERTS
    n_slots = nb * tm
    n_used = (pad_end[-1] // tm).astype(jnp.int32)
    blk = jnp.minimum(jnp.arange(nb, dtype=jnp.int32), n_used - 1)
    blk_e = jnp.sum((pad_end[None, :] <= (blk * tm)[:, None]).astype(jnp.int32), axis=1)
    blk_e = jnp.minimum(blk_e, N_EXPERTS - 1)
    meta = jnp.concatenate([blk_e.astype(jnp.int32), n_used.reshape(1)])

    fill_start = jnp.concatenate([jnp.where(padded > 0, pad_end - tm, -1).astype(jnp.int32),
                                  n_used.reshape(1)])
    tb = min(128, n)
    xs = _scatter(dest, fill_start, h1p, n_slots, min(512, n), tm)
    act = _gmm1(meta, xs, prm['w_exp1'][l], prm['b_exp1'][l].reshape(N_EXPERTS, 1, 2 * D_EXPERT),
                nb, tm, 1024)
    ys = _gmm2(meta, act, prm['w_exp2'][l], prm['b_exp2'][l].reshape(N_EXPERTS, 1, D_MODEL),
               nb, tm)
    return _combine(dest, h1, top_g, row(prm['ln2_g'][l]), row(prm['ln2_b'][l]),
                    ys, tb)


def kernel(x, ln_in_g, ln_in_b, w_in, conv_w, conv_b, w_rgate, b_rgate, w_igate, b_igate, lru_lambda, shift_mu, w0, rw_decay_up, a0, rw_aaa_up, rw_gate_up, k_k, k_a, r_k, gn_g, gn_b, w_out, ln1_g, ln1_b, w_router, b_router, w_exp1, b_exp1, w_exp2, b_exp2, ln2_g, ln2_b):
    bsz, t, d = x.shape
    prm = dict(ln_in_g=ln_in_g, ln_in_b=ln_in_b, w_in=w_in, conv_w=conv_w, conv_b=conv_b,
               w_rgate=w_rgate, b_rgate=b_rgate, w_igate=w_igate, b_igate=b_igate,
               lru_lambda=lru_lambda, shift_mu=shift_mu, w0=w0, rw_decay_up=rw_decay_up, a0=a0,
               rw_aaa_up=rw_aaa_up, rw_gate_up=rw_gate_up, k_k=k_k, k_a=k_a, r_k=r_k, gn_g=gn_g,
               gn_b=gn_b, w_out=w_out, ln1_g=ln1_g, ln1_b=ln1_b, w_router=w_router,
               b_router=b_router, w_exp1=w_exp1, b_exp1=b_exp1, w_exp2=w_exp2, b_exp2=b_exp2,
               ln2_g=ln2_g, ln2_b=ln2_b)
    out = _layer(x.reshape(bsz * t, d), 0, prm, bsz, t)
    return out.reshape(bsz, t, d)
```

```python
import functools

import jax
import jax.numpy as jnp
from jax import lax
from jax.experimental import pallas as pl
from jax.experimental.pallas import tpu as pltpu

D_MODEL = 2048
DEPTH = 1
CHUNK = 64
LRU_WIDTH = 1024
LRU_BLOCKS = 16
LRU_BLOCK_W = LRU_WIDTH // LRU_BLOCKS
CONV_WIDTH = 4
RG_C = 8.0
RWKV_WIDTH = D_MODEL - LRU_WIDTH
HEAD_SIZE = 64
RWKV_HEADS = RWKV_WIDTH // HEAD_SIZE
DECAY_LORA = 96
AAA_LORA = 96
GATE_LORA = 256
N_EXPERTS = 32
TOP_K = 4
D_EXPERT = D_MODEL
SWIGLU_LIMIT = 7.0
SWIGLU_ALPHA = 1.702
LN_EPS = 1e-5
GN_EPS = HEAD_SIZE * 1e-5
DEEPNORM_ALPHA = (2.0 * DEPTH) ** 0.25

LANE = 128
SUBLANE = 8
MXU_DIM = 256
LORA_PAD = 128
LORA_W = 2 * LORA_PAD + GATE_LORA
P_WIDTH = 2 * LRU_WIDTH + 3 * RWKV_WIDTH + LORA_W
VMEM_LIMIT = 56 * 1024 * 1024
GMM1_VMEM_LIMIT = 60 * 1024 * 1024

F32 = jnp.float32
BF16 = jnp.bfloat16


def _cparams(sem, vmem_limit=VMEM_LIMIT):
    return pltpu.CompilerParams(dimension_semantics=sem, vmem_limit_bytes=vmem_limit)


def _layer_norm_rows(x, g, b):
    mu = jnp.mean(x, axis=-1, keepdims=True)
    xc = x - mu
    var = jnp.mean(xc * xc, axis=-1, keepdims=True)
    return xc * lax.rsqrt(var + LN_EPS) * g + b


def _softplus(z):
    return jnp.maximum(z, 0.0) + jnp.log1p(jnp.exp(-jnp.abs(z)))


def _sigmoid(z):
    return 1.0 / (1.0 + jnp.exp(-z))


def _split2(x):
    hi = x.astype(BF16)
    lo = (x - hi.astype(F32)).astype(BF16)
    return hi, lo


HALF = D_MODEL // 2
U32 = jnp.uint32
RANK_SPAN = 1 << 20


def _pack_halves(x):
    lo = lax.bitcast_convert_type(x[:, :HALF].astype(BF16).astype(F32), U32) >> 16
    hi = lax.bitcast_convert_type(x[:, HALF:].astype(BF16).astype(F32), U32) & jnp.uint32(0xFFFF0000)
    return hi | lo


def _unpack_halves(w):
    lo = lax.bitcast_convert_type(w << 16, F32)
    hi = lax.bitcast_convert_type(w & jnp.uint32(0xFFFF0000), F32)
    return lo, hi


ROW_TILES = HALF // LANE
assert ROW_TILES == SUBLANE


def _store_rows(ref, row0, packed):
    m = packed.shape[0]
    for c in range(ROW_TILES):
        ref[pl.ds(row0 * ROW_TILES + c, m, stride=ROW_TILES), :] = packed[:, c * LANE:(c + 1) * LANE]


def _load_rows(ref, row0, m):
    return jnp.concatenate(
        [ref[pl.ds(row0 * ROW_TILES + c, m, stride=ROW_TILES), :] for c in range(ROW_TILES)], axis=1)


def _row_tile(ref, row):
    return ref.at[pl.ds(pl.multiple_of(row * ROW_TILES, ROW_TILES), ROW_TILES), :]


def _head_sum(x, ones_ref):
    hi, lo = _split2(x)
    ones = ones_ref[...]
    parts = []
    for g in range(x.shape[1] // MXU_DIM):
        sl = slice(g * MXU_DIM, (g + 1) * MXU_DIM)
        parts.append(jnp.dot(hi[:, sl], ones, preferred_element_type=F32)
                     + jnp.dot(lo[:, sl], ones, preferred_element_type=F32))
    return jnp.concatenate(parts, axis=1)


def _in_proj_kernel(x_ref, g_ref, b_ref, w_ref, h_ref, p_ref, hb_ref):
    @pl.when(pl.program_id(1) == 0)
    def _():
        h = _layer_norm_rows(x_ref[...], g_ref[...], b_ref[...])
        h_ref[...] = h
        hb_ref[...] = h.astype(BF16)

    p_ref[...] = jnp.dot(hb_ref[...], w_ref[...], preferred_element_type=F32)


def _in_proj(x2, g, b, w_bf16, tm, tn):
    n = x2.shape[0]
    return pl.pallas_call(
        _in_proj_kernel,
        grid=(n // tm, P_WIDTH // tn),
        in_specs=[
            pl.BlockSpec((tm, D_MODEL), lambda i, j: (i, 0)),
            pl.BlockSpec((1, D_MODEL), lambda i, j: (0, 0)),
            pl.BlockSpec((1, D_MODEL), lambda i, j: (0, 0)),
            pl.BlockSpec((D_MODEL, tn), lambda i, j: (0, j)),
        ],
        out_specs=[
            pl.BlockSpec((tm, D_MODEL), lambda i, j: (i, 0)),
            pl.BlockSpec((tm, tn), lambda i, j: (i, j)),
        ],
        out_shape=[
            jax.ShapeDtypeStruct((n, D_MODEL), F32),
            jax.ShapeDtypeStruct((n, P_WIDTH), F32),
        ],
        scratch_shapes=[pltpu.VMEM((tm, D_MODEL), BF16)],
        compiler_params=_cparams(("parallel", "arbitrary")),
        name="in_proj",
    )(x2, g, b, w_bf16)


def _lru_kernel(u_ref, gi_ref, cw_ref, cb_ref, wr_ref, br_ref, wi_ref, bi_ref,
                lam_ref, o_ref, ubuf, a_s, b_s, carry, *, tt):
    first = pl.program_id(1) == 0

    @pl.when(first)
    def _():
        ubuf[0:SUBLANE, :] = jnp.zeros((SUBLANE, LRU_WIDTH), F32)
        carry[...] = jnp.zeros_like(carry)

    @pl.when(jnp.logical_not(first))
    def _():
        ubuf[0:SUBLANE, :] = ubuf[tt:tt + SUBLANE, :]

    ubuf[SUBLANE:, :] = u_ref[...]
    uc = cb_ref[...]
    for i in range(CONV_WIDTH):
        off = SUBLANE - (CONV_WIDTH - 1) + i
        uc = uc + cw_ref[i:i + 1, :] * ubuf[off:off + tt, :]

    ucb = uc.astype(BF16)
    n_grp = LRU_WIDTH // MXU_DIM

    def gate(w_ref, bias_ref):
        parts = [jnp.dot(ucb[:, g * MXU_DIM:(g + 1) * MXU_DIM], w_ref[g],
                         preferred_element_type=F32) for g in range(n_grp)]
        return _sigmoid(jnp.concatenate(parts, axis=1) + bias_ref[...])

    r_gate = gate(wr_ref, br_ref)
    i_gate = gate(wi_ref, bi_ref)
    log_a = (-RG_C * r_gate) * _softplus(-lam_ref[...])
    a_s[...] = jnp.exp(log_a)
    th = jnp.tanh(log_a)
    b_s[...] = jnp.sqrt(-2.0 * th / (1.0 - th)) * (i_gate * uc)

    row = lax.broadcasted_iota(jnp.int32, (SUBLANE, LRU_WIDTH), 0)

    def group(gidx, c):
        off = pl.multiple_of(gidx * SUBLANE, SUBLANE)
        a = a_s[pl.ds(off, SUBLANE), :]
        b = b_s[pl.ds(off, SUBLANE), :]
        for d in (1, 2, 4):
            keep = row >= d
            a_sh = pltpu.roll(a, d, axis=0)
            b_sh = pltpu.roll(b, d, axis=0)
            b = jnp.where(keep, a * b_sh + b, b)
            a = jnp.where(keep, a * a_sh, a)
        h = a * c + b
        b_s[pl.ds(off, SUBLANE), :] = h
        return h[SUBLANE - 1:SUBLANE, :]

    carry[...] = lax.fori_loop(0, tt // SUBLANE, group, carry[...])
    o_ref[...] = (b_s[...] * jax.nn.gelu(gi_ref[...])).astype(o_ref.dtype)


def _lru(p, conv_w, conv_b, wr_bd, br, wi_bd, bi, lam, bsz, t, tt):
    n = bsz * t
    nt = t // tt
    vec = lambda: pl.BlockSpec((1, LRU_WIDTH), lambda b, i: (0, 0))
    wspec = lambda: pl.BlockSpec((LRU_WIDTH // MXU_DIM, MXU_DIM, MXU_DIM), lambda b, i: (0, 0, 0))
    return pl.pallas_call(
        functools.partial(_lru_kernel, tt=tt),
        grid=(bsz, nt),
        in_specs=[
            pl.BlockSpec((tt, LRU_WIDTH), lambda b, i: (b * nt + i, 0)),
            pl.BlockSpec((tt, LRU_WIDTH), lambda b, i: (b * nt + i, 1)),
            pl.BlockSpec((CONV_WIDTH, LRU_WIDTH), lambda b, i: (0, 0)),
            vec(), wspec(), vec(), wspec(), vec(), vec(),
        ],
        out_specs=pl.BlockSpec((tt, LRU_WIDTH), lambda b, i: (b * nt + i, 0)),
        out_shape=jax.ShapeDtypeStruct((n, LRU_WIDTH), BF16),
        scratch_shapes=[
            pltpu.VMEM((tt + SUBLANE, LRU_WIDTH), F32),
            pltpu.VMEM((tt, LRU_WIDTH), F32),
            pltpu.VMEM((tt, LRU_WIDTH), F32),
            pltpu.VMEM((1, LRU_WIDTH), F32),
        ],
        compiler_params=_cparams(("parallel", "arbitrary")),
        name="lru",
    )(p, p, conv_w, conv_b, wr_bd, br, wi_bd, bi, lam)


def _rw_prep_kernel(pr_ref, pk_ref, pv_ref, pa_ref, mur_ref, muk_ref, muv_ref, mua_ref,
                    w0_ref, wdec_ref, a0_ref, waaa_ref, wgate_ref, kk_ref, ka_ref, ones_ref,
                    r_out, k_out, v_out, kkn_out, b_out, cl_out, g_out,
                    prev_r, prev_k, prev_v, prev_a, *, tt):
    first = pl.program_id(1) == 0

    @pl.when(first)
    def _():
        for ref in (prev_r, prev_k, prev_v, prev_a):
            ref[...] = jnp.zeros_like(ref)

    def shift(x_ref, prev_ref, mu_ref):
        x = x_ref[...]
        row = lax.broadcasted_iota(jnp.int32, x.shape, 0)
        prev = jnp.where(row == 0, prev_ref[...], pltpu.roll(x, 1, axis=0))
        prev_ref[...] = x[tt - 1:tt, :]
        return x + (prev - x) * mu_ref[...]

    r = shift(pr_ref, prev_r, mur_ref)
    k = shift(pk_ref, prev_k, muk_ref)
    v = shift(pv_ref, prev_v, muv_ref)
    ad = shift(pa_ref, prev_a, mua_ref)

    wd = jnp.tanh(ad[:, 0:LORA_PAD]).astype(BF16)
    aa = ad[:, LORA_PAD:2 * LORA_PAD].astype(BF16)
    gd = _sigmoid(ad[:, 2 * LORA_PAD:]).astype(BF16)
    w_pre = w0_ref[...] + jnp.dot(wd, wdec_ref[...], preferred_element_type=F32)
    w_log = -_softplus(-w_pre) - 0.5
    a = _sigmoid(a0_ref[...] + jnp.dot(aa, waaa_ref[...], preferred_element_type=F32))
    g = jnp.dot(gd, wgate_ref[...], preferred_element_type=F32)

    kk = k * kk_ref[...]
    norm = jnp.sqrt(_head_sum(kk * kk, ones_ref))
    kk = kk / jnp.maximum(norm, 1e-12)

    lw = -jnp.exp(w_log)
    ri = lax.broadcasted_iota(jnp.int32, (tt, tt), 0)
    ci = lax.broadcasted_iota(jnp.int32, (tt, tt), 1)
    tri = jnp.logical_and(ri // CHUNK == ci // CHUNK, ri >= ci).astype(BF16)
    lw_hi = lw.astype(BF16)
    rem = lw - lw_hi.astype(F32)
    lw_mid = rem.astype(BF16)
    lw_lo = (rem - lw_mid.astype(F32)).astype(BF16)
    cl = (jnp.dot(tri, lw_hi, preferred_element_type=F32)
          + jnp.dot(tri, lw_mid, preferred_element_type=F32)
          + jnp.dot(tri, lw_lo, preferred_element_type=F32))

    r_out[...] = r
    k_out[...] = k * (1.0 + (a - 1.0) * ka_ref[...])
    v_out[...] = v
    kkn_out[...] = kk
    b_out[...] = kk * a
    cl_out[...] = cl
    g_out[...] = g


def _rw_prep(p, mu_r, mu_k, mu_v, mu_a, w0, wdec, a0, waaa, wgate, k_k, k_a, ones_bd, bsz, t, tt):
    n = bsz * t
    nt = t // tt
    cb = 2 * LRU_WIDTH // RWKV_WIDTH
    row = lambda c: pl.BlockSpec((tt, RWKV_WIDTH), lambda b, i: (b * nt + i, c))
    vec = lambda w: pl.BlockSpec((1, w), lambda b, i: (0, 0))
    full = lambda s: pl.BlockSpec(s, lambda b, i: (0, 0))
    lora_cb = (2 * LRU_WIDTH + 3 * RWKV_WIDTH) // LORA_W
    out = jax.ShapeDtypeStruct((n, RWKV_WIDTH), F32)
    return pl.pallas_call(
        functools.partial(_rw_prep_kernel, tt=tt),
        grid=(bsz, nt),
        in_specs=[
            row(cb), row(cb + 1), row(cb + 2),
            pl.BlockSpec((tt, LORA_W), lambda b, i: (b * nt + i, lora_cb)),
            vec(RWKV_WIDTH), vec(RWKV_WIDTH), vec(RWKV_WIDTH), vec(LORA_W),
            vec(RWKV_WIDTH), full((LORA_PAD, RWKV_WIDTH)),
            vec(RWKV_WIDTH), full((LORA_PAD, RWKV_WIDTH)),
            full((GATE_LORA, RWKV_WIDTH)),
            vec(RWKV_WIDTH), vec(RWKV_WIDTH),
            full((MXU_DIM, MXU_DIM)),
        ],
        out_specs=[pl.BlockSpec((tt, RWKV_WIDTH), lambda b, i: (b * nt + i, 0))] * 7,
        out_shape=[out] * 7,
        scratch_shapes=[pltpu.VMEM((1, RWKV_WIDTH), F32)] * 3 + [pltpu.VMEM((1, LORA_W), F32)],
        compiler_params=_cparams(("parallel", "arbitrary")),
        name="rw_prep",
    )(p, p, p, p, mu_r, mu_k, mu_v, mu_a, w0, wdec, a0, waaa, wgate, k_k, k_a, ones_bd)


def _rw_chunk_kernel(r_ref, k_ref, v_ref, kk_ref, b_ref, cl_ref, y_ref, *s_refs):
    c = CHUNK
    hs = HEAD_SIZE
    n_seq = r_ref.shape[0]

    @pl.when(pl.program_id(1) == 0)
    def _():
        for s_ref in s_refs:
            s_ref[...] = jnp.zeros_like(s_ref)

    r2 = lax.broadcasted_iota(jnp.int32, (2 * c, 2 * c), 0)
    c2 = lax.broadcasted_iota(jnp.int32, (2 * c, 2 * c), 1)
    tq = jnp.where(r2 >= c, r2 - c, r2)
    tk = jnp.where(c2 >= c, c2 - c, c2)
    mask = tk < tq + jnp.where(r2 >= c, 1, 0)
    zeros_cc = jnp.zeros((c, hs), BF16)
    ri_c = lax.broadcasted_iota(jnp.int32, (c, c), 0)
    ci_c = lax.broadcasted_iota(jnp.int32, (c, c), 1)

    dn_t = (((1,), (1,)), ((), ()))
    dn_l = (((0,), (0,)), ((), ()))

    row_w = lax.broadcasted_iota(jnp.int32, (c, RWKV_WIDTH), 0)

    def scaled_operands(q):
        cl = cl_ref[q]
        cl_ex = jnp.where(row_w == 0, 0.0, pltpu.roll(cl, 1, axis=0))
        cl_last = cl[c - 1:c, :]
        g_inv = jnp.exp(-cl)
        g_dec = jnp.exp(cl_last - cl)
        bv = b_ref[q]
        kx = k_ref[q]
        rg = r_ref[q] * jnp.exp(cl)
        return dict(
            g_last=jnp.exp(cl_last),
            v=v_ref[q].astype(BF16),
            rg=rg,
            lhs_a=(-kk_ref[q] * jnp.exp(cl_ex)).astype(BF16),
            lhs_r=rg.astype(BF16),
            rhs_b=(bv * g_inv).astype(BF16),
            rhs_k=(kx * g_inv).astype(BF16),
            dec_b=(bv * g_dec).astype(BF16),
            dec_k=(kx * g_dec).astype(BF16),
        )

    ops = [scaled_operands(q) for q in range(n_seq)]

    chains = [(q, h) for q in range(n_seq) for h in range(RWKV_HEADS)]
    sl_of = lambda h: slice(h * hs, (h + 1) * hs)
    s0s = [s_refs[q * RWKV_HEADS + h][...] for q, h in chains]
    aas = []
    for q, h in chains:
        o, sl = ops[q], sl_of(h)
        lhs = jnp.concatenate([o['lhs_a'][:, sl], o['lhs_r'][:, sl]], axis=0)
        rhs = jnp.concatenate([o['rhs_b'][:, sl], o['rhs_k'][:, sl]], axis=0)
        aa = lax.dot_general(lhs, rhs, dn_t, preferred_element_type=F32)
        aas.append(jnp.where(mask, aa, 0.0))
    v_hs = [ops[q]['v'][:, sl_of(h)] for q, h in chains]
    a_rs = [aa[c:, :].astype(BF16) for aa in aas]
    ps = [aa[:c, :c] for aa in aas]
    x0s = []
    for i, (q, h) in enumerate(chains):
        akv = jnp.dot(aas[i][:c, c:].astype(BF16), v_hs[i], preferred_element_type=F32)
        x0s.append(jnp.concatenate([ops[q]['lhs_a'][:, sl_of(h)], akv.astype(BF16)], axis=1))

    def mm(a, b):
        return jnp.dot(a.astype(BF16), b.astype(BF16), preferred_element_type=F32)

    blk = 4
    diag = (ri_c // blk) == (ci_c // blk)
    l4s = [jnp.where(diag, p, 0.0) for p in ps]
    p2s = [mm(l4, l4) for l4 in l4s]
    ts = [jnp.where(ri_c == ci_c, 1.0, l4) for l4 in l4s]
    ts = [t + mm(t, p2) for t, p2 in zip(ts, p2s)]
    s = blk
    while s < c:
        off = jnp.logical_and((ri_c // (2 * s)) == (ci_c // (2 * s)),
                              jnp.logical_and((ri_c // s) % 2 == 1, (ci_c // s) % 2 == 0))
        ys = [mm(jnp.where(off, p, 0.0), t) for p, t in zip(ps, ts)]
        ts = [t + mm(t, y) for t, y in zip(ts, ys)]
        s *= 2
    xs = [jnp.dot(t.astype(BF16), x0, preferred_element_type=F32) for t, x0 in zip(ts, x0s)]
    for i, (q, h) in enumerate(chains):
        o, sl = ops[q], sl_of(h)
        rhs2 = jnp.concatenate(
            [xs[i].astype(BF16), jnp.concatenate([zeros_cc, v_hs[i]], axis=1)], axis=0)
        qy = jnp.dot(a_rs[i], rhs2, preferred_element_type=F32)
        dec = jnp.concatenate([o['dec_b'][:, sl], o['dec_k'][:, sl]], axis=0)
        mnt = lax.dot_general(rhs2, dec, dn_l, preferred_element_type=F32)
        s0b = s0s[i].astype(BF16)
        qq = (o['rg'][:, sl] + qy[:, :hs]).astype(BF16)
        y_ref[q, :, sl] = lax.dot_general(qq, s0b, dn_t, preferred_element_type=F32) + qy[:, hs:]
        s_refs[i][...] = (s0s[i] * o['g_last'][:, sl]
                          + jnp.dot(s0b, mnt[:hs, :].astype(BF16), preferred_element_type=F32)
                          + mnt[hs:, :])


def _rw_chunk(r, k, v, kk, b, cl, bsz, t, n_seq):
    nc = t // CHUNK
    shape3 = (bsz, t, RWKV_WIDTH)
    spec = lambda: pl.BlockSpec((n_seq, CHUNK, RWKV_WIDTH), lambda bb, i: (bb, i, 0))
    y = pl.pallas_call(
        _rw_chunk_kernel,
        grid=(bsz // n_seq, nc),
        in_specs=[spec() for _ in range(6)],
        out_specs=spec(),
        out_shape=jax.ShapeDtypeStruct(shape3, F32),
        scratch_shapes=[pltpu.VMEM((HEAD_SIZE, HEAD_SIZE), F32)
                        for _ in range(n_seq * RWKV_HEADS)],
        compiler_params=_cparams(("parallel", "arbitrary")),
        name="rw_chunk",
    )(*[a.reshape(shape3) for a in (r, k, v, kk, b, cl)])
    return y.reshape(bsz * t, RWKV_WIDTH)


def _rw_post_kernel(y_ref, r_ref, k_ref, v_ref, g_ref, rk_ref, gg_ref, gb_ref, ones_ref, o_ref):
    y = y_ref[...]
    inv = 1.0 / HEAD_SIZE
    mu = _head_sum(y, ones_ref) * inv
    yc = y - mu
    var = _head_sum(yc * yc, ones_ref) * inv
    yn = yc * lax.rsqrt(var + GN_EPS) * gg_ref[...] + gb_ref[...]
    bonus = _head_sum(r_ref[...] * k_ref[...] * rk_ref[...], ones_ref) * v_ref[...]
    o_ref[...] = ((yn + bonus) * g_ref[...]).astype(o_ref.dtype)


def _rw_post(y, r, k, v, g, r_k, gn_g, gn_b, ones_bd, tt):
    n = y.shape[0]
    row = lambda: pl.BlockSpec((tt, RWKV_WIDTH), lambda i: (i, 0))
    vec = lambda: pl.BlockSpec((1, RWKV_WIDTH), lambda i: (0, 0))
    return pl.pallas_call(
        _rw_post_kernel,
        grid=(n // tt,),
        in_specs=[row(), row(), row(), row(), row(), vec(), vec(), vec(),
                  pl.BlockSpec((MXU_DIM, MXU_DIM), lambda i: (0, 0))],
        out_specs=row(),
        out_shape=jax.ShapeDtypeStruct((n, RWKV_WIDTH), BF16),
        compiler_params=_cparams(("parallel",)),
        name="rw_post",
    )(y, r, k, v, g, r_k, gn_g, gn_b, ones_bd)


def _out_proj_kernel(yl_ref, yr_ref, h0_ref, wo_ref, g_ref, b_ref, wr_ref, br_ref,
                     h1_ref, h1p_ref, code_ref, tg_ref, cnt_ref, carry, *, tm):
    @pl.when(pl.program_id(0) == 0)
    def _():
        carry[...] = jnp.zeros_like(carry)

    mix = jnp.dot(jnp.concatenate([yl_ref[...], yr_ref[...]], axis=1), wo_ref[...],
                  preferred_element_type=F32)
    h1 = _layer_norm_rows(DEEPNORM_ALPHA * h0_ref[...] + mix, g_ref[...], b_ref[...])
    h1_ref[...] = h1
    _store_rows(h1p_ref, 0, _pack_halves(h1))

    h_hi, h_lo = _split2(h1)
    both = jnp.dot(h_hi, wr_ref[...], preferred_element_type=F32)
    logits = (both[:, :N_EXPERTS] + both[:, N_EXPERTS:]
              + jnp.dot(h_lo, wr_ref[:, 0:N_EXPERTS], preferred_element_type=F32)) + br_ref[...]

    lane = lax.broadcasted_iota(jnp.int32, (tm, N_EXPERTS), 1).astype(F32)
    lane4 = lax.broadcasted_iota(jnp.int32, (tm, TOP_K), 1)
    work = logits
    vals, idxs, sels = [], [], []
    for _ in range(TOP_K):
        m = jnp.max(work, axis=-1, keepdims=True)
        idx = jnp.min(jnp.where(work == m, lane, float(N_EXPERTS)), axis=-1, keepdims=True)
        sel = lane == idx
        vals.append(m)
        idxs.append(idx)
        sels.append(sel)
        work = jnp.where(sel, -jnp.inf, work)
    exps = [jnp.exp(vv - vals[0]) for vv in vals]
    denom = exps[0] + exps[1] + exps[2] + exps[3]

    member = jnp.zeros((tm, N_EXPERTS), F32)
    for sel in sels:
        member = member + sel.astype(F32)
    ri = lax.broadcasted_iota(jnp.int32, (tm, tm), 0)
    ci = lax.broadcasted_iota(jnp.int32, (tm, tm), 1)
    before = (ci < ri).astype(BF16)
    rank_full = jnp.dot(before, member.astype(BF16), preferred_element_type=F32) + carry[...]
    carry[...] = carry[...] + jnp.sum(member, axis=0, keepdims=True)
    cnt_ref[...] = carry[...].astype(jnp.int32)

    code = jnp.zeros((tm, TOP_K), jnp.int32)
    tg = jnp.zeros((tm, TOP_K), F32)
    for kq in range(TOP_K):
        rank_k = jnp.sum(jnp.where(sels[kq], rank_full, 0.0), axis=-1, keepdims=True)
        code_k = idxs[kq].astype(jnp.int32) * RANK_SPAN + rank_k.astype(jnp.int32)
        code = jnp.where(lane4 == kq, code_k, code)
        tg = jnp.where(lane4 == kq, exps[kq] / denom, tg)
    code_ref[...] = code
    tg_ref[...] = tg


def _out_proj(y_lru, y_rw, h0, w_out_bf16, g, b, w_router_split, b_router, tm):
    n = h0.shape[0]
    row = lambda w: pl.BlockSpec((tm, w), lambda i: (i, 0))
    vec = lambda w: pl.BlockSpec((1, w), lambda i: (0, 0))
    return pl.pallas_call(
        functools.partial(_out_proj_kernel, tm=tm),
        grid=(n // tm,),
        in_specs=[
            row(LRU_WIDTH), row(RWKV_WIDTH), row(D_MODEL),
            pl.BlockSpec((D_MODEL, D_MODEL), lambda i: (0, 0)),
            vec(D_MODEL), vec(D_MODEL),
            pl.BlockSpec((D_MODEL, 2 * N_EXPERTS), lambda i: (0, 0)),
            vec(N_EXPERTS),
        ],
        out_specs=[row(D_MODEL), pl.BlockSpec((tm * ROW_TILES, LANE), lambda i: (i, 0)),
                   row(TOP_K), row(TOP_K), vec(N_EXPERTS)],
        out_shape=[
            jax.ShapeDtypeStruct((n, D_MODEL), F32),
            jax.ShapeDtypeStruct((n * ROW_TILES, LANE), U32),
            jax.ShapeDtypeStruct((n, TOP_K), jnp.int32),
            jax.ShapeDtypeStruct((n, TOP_K), F32),
            jax.ShapeDtypeStruct((1, N_EXPERTS), jnp.int32),
        ],
        scratch_shapes=[pltpu.VMEM((1, N_EXPERTS), F32)],
        compiler_params=_cparams(("arbitrary",)),
        name="out_proj",
    )(y_lru, y_rw, h0, w_out_bf16, g, b, w_router_split, b_router)


def _scatter_kernel(dest_ref, fill_ref, h_ref, xs_ref, zbuf, sem, zsem, *, tb, tm, nb):
    @pl.when(pl.program_id(0) == 0)
    def _():
        zbuf[...] = jnp.zeros_like(zbuf)

        def fill_copy(start):
            start = pl.multiple_of(start * ROW_TILES, SUBLANE)
            return pltpu.make_async_copy(zbuf, xs_ref.at[pl.ds(start, tm * ROW_TILES), :], zsem)

        def fill(e, carry):
            @pl.when(fill_ref[e] >= 0)
            def _():
                fill_copy(fill_ref[e]).start()
            return carry

        def fill_wait(e, carry):
            @pl.when(fill_ref[e] >= 0)
            def _():
                fill_copy(fill_ref[e]).wait()
            return carry

        def tail(blk, carry):
            fill_copy(blk * tm).start()
            return carry

        def tail_wait(blk, carry):
            fill_copy(blk * tm).wait()
            return carry

        n_used = fill_ref[N_EXPERTS]
        lax.fori_loop(0, N_EXPERTS, fill, 0)
        lax.fori_loop(n_used, nb, tail, 0)
        lax.fori_loop(0, N_EXPERTS, fill_wait, 0)
        lax.fori_loop(n_used, nb, tail_wait, 0)

    base = pl.program_id(0) * (tb * TOP_K)

    def copy(t, kq):
        slot = dest_ref[base + t * TOP_K + kq]
        return pltpu.make_async_copy(_row_tile(h_ref, t), _row_tile(xs_ref, slot), sem)

    def issue(t, carry):
        for kq in range(TOP_K):
            copy(t, kq).start(priority=kq % 2)
        return carry

    lax.fori_loop(0, tb, issue, 0, unroll=4)
    for kq in range(TOP_K):
        pltpu.make_async_copy(h_ref, xs_ref.at[pl.ds(0, tb * ROW_TILES), :], sem).wait()


def _scatter(dest_flat, fill_start, h1p, n_slots, tb, tm):
    n = h1p.shape[0] // ROW_TILES
    nb = n_slots // tm
    grid_spec = pltpu.PrefetchScalarGridSpec(
        num_scalar_prefetch=2,
        grid=(n // tb,),
        in_specs=[pl.BlockSpec((tb * ROW_TILES, LANE), lambda i, d, f: (i, 0))],
        out_specs=pl.BlockSpec(memory_space=pl.ANY),
        scratch_shapes=[pltpu.VMEM((tm * ROW_TILES, LANE), U32), pltpu.SemaphoreType.DMA(()),
                        pltpu.SemaphoreType.DMA(())],
    )
    return pl.pallas_call(
        functools.partial(_scatter_kernel, tb=tb, tm=tm, nb=nb),
        grid_spec=grid_spec,
        out_shape=jax.ShapeDtypeStruct((n_slots * ROW_TILES, LANE), U32),
        compiler_params=_cparams(("arbitrary",)),
        name="scatter",
    )(dest_flat, fill_start, h1p)


def _run_weights(meta_ref, runs, nb, n_col, copies, convert):
    j = pl.program_id(0)
    i = pl.program_id(1)
    e = meta_ref[i]
    prev_e = meta_ref[jnp.maximum(i - 1, 0)]

    @pl.when(jnp.logical_and(j == 0, i == 0))
    def _():
        runs[0] = 0

    @pl.when(jnp.logical_or(i == 0, e != prev_e))
    def _():
        cnt = runs[0]
        slot = cnt % 2

        @pl.when(cnt == 0)
        def _():
            for cp in copies(e, j, slot):
                cp.start()

        for cp in copies(e, j, slot):
            cp.wait()
        convert(slot)

        nxt = meta_ref[nb + 1 + i]

        @pl.when(nxt >= 0)
        def _():
            for cp in copies(nxt, j, 1 - slot):
                cp.start()

        @pl.when(jnp.logical_and(nxt < 0, j + 1 < n_col))
        def _():
            for cp in copies(meta_ref[0], j + 1, 1 - slot):
                cp.start()

        runs[0] = cnt + 1


def _gmm1_kernel(meta_ref, x_ref, w_hbm, bg_ref, bu_ref, o_ref, wst, wgb, wub, runs, sems,
                 *, nb, tm, tn, nj):
    i = pl.program_id(1)

    def copies(expert, col, slot):
        gate_cols = pl.ds(pl.multiple_of(col * tn, tn), tn)
        up_cols = pl.ds(pl.multiple_of((nj + col) * tn, tn), tn)
        return (pltpu.make_async_copy(w_hbm.at[expert, :, gate_cols], wst.at[slot, 0], sems.at[slot]),
                pltpu.make_async_copy(w_hbm.at[expert, :, up_cols], wst.at[slot, 1], sems.at[slot]))

    def convert(slot):
        wgb[...] = wst[slot, 0].astype(BF16)
        wub[...] = wst[slot, 1].astype(BF16)

    _run_weights(meta_ref, runs, nb, nj, copies, convert)

    @pl.when(i < meta_ref[nb])
    def _():
        x_lo, x_hi = _unpack_halves(_load_rows(x_ref, 0, tm))
        x_lo = x_lo.astype(BF16)
        x_hi = x_hi.astype(BF16)
        gate = (jnp.dot(x_lo, wgb[0:HALF, :], preferred_element_type=F32)
                + jnp.dot(x_hi, wgb[HALF:, :], preferred_element_type=F32) + bg_ref[0])
        up = (jnp.dot(x_lo, wub[0:HALF, :], preferred_element_type=F32)
              + jnp.dot(x_hi, wub[HALF:, :], preferred_element_type=F32) + bu_ref[0])
        gate = jnp.minimum(gate, SWIGLU_LIMIT)
        up = jnp.clip(up, -SWIGLU_LIMIT, SWIGLU_LIMIT)
        act = gate * _sigmoid(SWIGLU_ALPHA * gate) * (up + 1.0)
        o_ref[...] = act.astype(o_ref.dtype)

    @pl.when(i >= meta_ref[nb])
    def _():
        o_ref[...] = jnp.zeros_like(o_ref)


def _gmm1(meta, xs, w1, b1, nb, tm, tn):
    n_slots = xs.shape[0] // ROW_TILES
    nj = D_EXPERT // tn

    def row_blk(j, i, m):
        return jnp.minimum(i, m[nb] - 1)

    grid_spec = pltpu.PrefetchScalarGridSpec(
        num_scalar_prefetch=1,
        grid=(nj, nb),
        in_specs=[
            pl.BlockSpec((tm * ROW_TILES, LANE), lambda j, i, m: (row_blk(j, i, m), 0)),
            pl.BlockSpec(memory_space=pl.ANY),
            pl.BlockSpec((1, 1, tn), lambda j, i, m: (m[i], 0, j)),
            pl.BlockSpec((1, 1, tn), lambda j, i, m: (m[i], 0, j + nj)),
        ],
        out_specs=pl.BlockSpec((tm, tn), lambda j, i, m: (i, j)),
        scratch_shapes=[pltpu.VMEM((2, 2, D_MODEL, tn), F32),
                        pltpu.VMEM((D_MODEL, tn), BF16), pltpu.VMEM((D_MODEL, tn), BF16),
                        pltpu.SMEM((1,), jnp.int32), pltpu.SemaphoreType.DMA((2,))],
    )
    return pl.pallas_call(
        functools.partial(_gmm1_kernel, nb=nb, tm=tm, tn=tn, nj=nj),
        grid_spec=grid_spec,
        out_shape=jax.ShapeDtypeStruct((n_slots, D_EXPERT), BF16),
        compiler_params=_cparams(("arbitrary", "arbitrary"), GMM1_VMEM_LIMIT),
        name="gmm1",
    )(meta, xs, w1, b1, b1)


def _gmm2_kernel(meta_ref, a_ref, w_hbm, b_ref, o_ref, wst, wb, runs, sems, *, nb):
    i = pl.program_id(1)

    def copies(expert, col, slot):
        del col
        return (pltpu.make_async_copy(w_hbm.at[expert], wst.at[slot], sems.at[slot]),)

    def convert(slot):
        wb[...] = wst[slot].astype(BF16)

    _run_weights(meta_ref, runs, nb, 1, copies, convert)

    @pl.when(i < meta_ref[nb])
    def _():
        out = jnp.dot(a_ref[...], wb[...], preferred_element_type=F32) + b_ref[0]
        _store_rows(o_ref, 0, _pack_halves(out))

    @pl.when(i >= meta_ref[nb])
    def _():
        o_ref[...] = jnp.zeros_like(o_ref)


def _gmm2(meta, act, w2, b2, nb, tm):
    n_slots = act.shape[0]

    def row_blk(j, i, m):
        return jnp.minimum(i, m[nb] - 1)

    grid_spec = pltpu.PrefetchScalarGridSpec(
        num_scalar_prefetch=1,
        grid=(1, nb),
        in_specs=[
            pl.BlockSpec((tm, D_EXPERT), lambda j, i, m: (row_blk(j, i, m), 0)),
            pl.BlockSpec(memory_space=pl.ANY),
            pl.BlockSpec((1, 1, D_MODEL), lambda j, i, m: (m[i], 0, 0)),
        ],
        out_specs=pl.BlockSpec((tm * ROW_TILES, LANE), lambda j, i, m: (i, 0)),
        scratch_shapes=[pltpu.VMEM((2, D_EXPERT, D_MODEL), F32), pltpu.VMEM((D_EXPERT, D_MODEL), BF16),
                        pltpu.SMEM((1,), jnp.int32), pltpu.SemaphoreType.DMA((2,))],
    )
    return pl.pallas_call(
        functools.partial(_gmm2_kernel, nb=nb),
        grid_spec=grid_spec,
        out_shape=jax.ShapeDtypeStruct((n_slots * ROW_TILES, LANE), U32),
        compiler_params=_cparams(("arbitrary", "arbitrary")),
        name="gmm2",
    )(meta, act, w2, b2)


def _combine_kernel(dest_ref, h1_ref, tg_ref, g_ref, b_ref, ys_ref, o_ref, buf, sems, *, tb, n_steps):
    i = pl.program_id(0)

    def issue(step, half):
        base = step * (tb * TOP_K)

        def body(t, carry):
            for kq in range(TOP_K):
                slot = dest_ref[base + t * TOP_K + kq]
                pltpu.make_async_copy(_row_tile(ys_ref, slot), _row_tile(buf.at[half, kq], t),
                                      sems.at[half]).start(priority=kq % 2)
            return carry

        lax.fori_loop(0, tb, body, 0, unroll=4)

    @pl.when(i == 0)
    def _():
        issue(0, 0)

    @pl.when(i + 1 < n_steps)
    def _():
        issue(i + 1, (i + 1) % 2)

    half = i % 2
    for kq in range(TOP_K):
        pltpu.make_async_copy(ys_ref.at[pl.ds(0, tb * ROW_TILES), :], buf.at[half, kq],
                              sems.at[half]).wait()

    tg = tg_ref[...]
    y = DEEPNORM_ALPHA * h1_ref[...]
    for kq in range(TOP_K):
        lo, hi = _unpack_halves(_load_rows(buf.at[half, kq], 0, tb))
        y = y + tg[:, kq:kq + 1] * jnp.concatenate([lo, hi], axis=1)
    o_ref[...] = _layer_norm_rows(y, g_ref[...], b_ref[...])


def _combine(dest_flat, h1, tg, g, b, ys, tb):
    n = h1.shape[0]
    grid_spec = pltpu.PrefetchScalarGridSpec(
        num_scalar_prefetch=1,
        grid=(n // tb,),
        in_specs=[
            pl.BlockSpec((tb, D_MODEL), lambda i, d: (i, 0)),
            pl.BlockSpec((tb, TOP_K), lambda i, d: (i, 0)),
            pl.BlockSpec((1, D_MODEL), lambda i, d: (0, 0)),
            pl.BlockSpec((1, D_MODEL), lambda i, d: (0, 0)),
            pl.BlockSpec(memory_space=pl.ANY),
        ],
        out_specs=pl.BlockSpec((tb, D_MODEL), lambda i, d: (i, 0)),
        scratch_shapes=[pltpu.VMEM((2, TOP_K, tb * ROW_TILES, LANE), U32),
                        pltpu.SemaphoreType.DMA((2,))],
    )
    return pl.pallas_call(
        functools.partial(_combine_kernel, tb=tb, n_steps=n // tb),
        grid_spec=grid_spec,
        out_shape=jax.ShapeDtypeStruct((n, D_MODEL), F32),
        compiler_params=_cparams(("arbitrary",)),
        name="combine",
    )(dest_flat, h1, tg, g, b, ys)


def _block_diag(w, group):
    nb = w.shape[0] // group
    w = w.reshape(nb, group, LRU_BLOCK_W, LRU_BLOCK_W)
    eye = jnp.eye(group, dtype=w.dtype)
    out = jnp.einsum('gajk,ab->gajbk', w, eye)
    return out.reshape(nb, group * LRU_BLOCK_W, group * LRU_BLOCK_W)


def _pad_rows(w, rows):
    return jnp.pad(w, ((0, rows - w.shape[0]), (0, 0)))


def _layer(h_in_x, l, prm, bsz, t):
    n = bsz * t
    row = lambda a: a.reshape(1, -1)

    w_in = prm['w_in'][l]
    o_rw = 2 * LRU_WIDTH
    o_l = o_rw + 3 * RWKV_WIDTH
    pad_c = lambda w: jnp.pad(w, ((0, 0), (0, LORA_PAD - w.shape[1])))
    w_in_p = jnp.concatenate([
        w_in[:, :o_l],
        pad_c(w_in[:, o_l:o_l + DECAY_LORA]),
        pad_c(w_in[:, o_l + DECAY_LORA:o_l + DECAY_LORA + AAA_LORA]),
        w_in[:, o_l + DECAY_LORA + AAA_LORA:],
    ], axis=1).astype(BF16)
    mu = prm['shift_mu'][l]
    pad_v = lambda v: jnp.pad(v, (0, LORA_PAD - v.shape[0]))
    mu_l = mu[3 * RWKV_WIDTH:]
    mu_a = jnp.concatenate([pad_v(mu_l[:DECAY_LORA]), pad_v(mu_l[DECAY_LORA:DECAY_LORA + AAA_LORA]),
                            mu_l[DECAY_LORA + AAA_LORA:]])

    tm_in = min(1024, n)
    h0, p = _in_proj(h_in_x, row(prm['ln_in_g']), row(prm['ln_in_b']), w_in_p, tm_in, 512)

    tt = min(1024, t)
    group = MXU_DIM // LRU_BLOCK_W
    y_lru = _lru(p, prm['conv_w'][l], row(prm['conv_b'][l]),
                 _block_diag(prm['w_rgate'][l], group).astype(BF16), row(prm['b_rgate'][l]),
                 _block_diag(prm['w_igate'][l], group).astype(BF16), row(prm['b_igate'][l]),
                 row(prm['lru_lambda'][l]), bsz, t, tt)

    head_id = jnp.arange(MXU_DIM) // HEAD_SIZE
    ones_bd = (head_id[:, None] == head_id[None, :]).astype(BF16)
    tt_rw = min(256, t)
    r, k, v, kk, bvec, cl, g = _rw_prep(
        p, row(mu[:RWKV_WIDTH]), row(mu[RWKV_WIDTH:2 * RWKV_WIDTH]),
        row(mu[2 * RWKV_WIDTH:3 * RWKV_WIDTH]), row(mu_a),
        row(prm['w0'][l]), _pad_rows(prm['rw_decay_up'][l], LORA_PAD).astype(BF16),
        row(prm['a0'][l]), _pad_rows(prm['rw_aaa_up'][l], LORA_PAD).astype(BF16),
        prm['rw_gate_up'][l].astype(BF16), row(prm['k_k'][l]), row(prm['k_a'][l]),
        ones_bd, bsz, t, tt_rw)
    y = _rw_chunk(r, k, v, kk, bvec, cl, bsz, t, 4 if bsz % 4 == 0 else 1)
    y_rw = _rw_post(y, r, k, v, g, row(prm['r_k'][l]), row(prm['gn_g'][l]), row(prm['gn_b'][l]),
                    ones_bd, min(1024, n))

    w_r = prm['w_router'][l]
    w_r_hi = w_r.astype(BF16)
    w_r_lo = (w_r - w_r_hi.astype(F32)).astype(BF16)
    tm_out = min(512, n)
    assert n < RANK_SPAN
    h1, h1p, code, top_g, counts = _out_proj(
        y_lru, y_rw, h0, prm['w_out'][l].astype(BF16), row(prm['ln1_g'][l]), row(prm['ln1_b'][l]),
        jnp.concatenate([w_r_hi, w_r_lo], axis=1), row(prm['b_router'][l]), tm_out)

    tm = 512
    counts = counts.reshape(N_EXPERTS)
    padded = ((counts + tm - 1) // tm) * tm
    pad_end = jnp.cumsum(padded)
    pad_start = pad_end - padded
    code = code.reshape(n * TOP_K // LANE, LANE)
    dest = (pad_start[code // RANK_SPAN] + code % RANK_SPAN).reshape(-1).astype(jnp.int32)
    nb = (n * TOP_K) // tm + N_EXPERTS
    n_slots = nb * tm
    n_used = (pad_end[-1] // tm).astype(jnp.int32)
    blk = jnp.minimum(jnp.arange(nb, dtype=jnp.int32), n_used - 1)
    blk_e = jnp.sum((pad_end[None, :] <= (blk * tm)[:, None]).astype(jnp.int32), axis=1)
    blk_e = jnp.minimum(blk_e, N_EXPERTS - 1)
    ids = jnp.arange(nb, dtype=jnp.int32)
    later = jnp.logical_and(jnp.logical_and(ids[None, :] > ids[:, None], ids[None, :] < n_used),
                            blk_e[None, :] != blk_e[:, None])
    nxt_e = jnp.where(jnp.any(later, axis=1), blk_e[jnp.argmax(later, axis=1)], -1)
    meta = jnp.concatenate([blk_e.astype(jnp.int32), n_used.reshape(1), nxt_e.astype(jnp.int32)])

    fill_start = jnp.concatenate([jnp.where(padded > 0, pad_end - tm, -1).astype(jnp.int32),
                                  n_used.reshape(1)])
    tb = min(128, n)
    xs = _scatter(dest, fill_start, h1p, n_slots, min(512, n), tm)
    act = _gmm1(meta, xs, prm['w_exp1'][l], prm['b_exp1'][l].reshape(N_EXPERTS, 1, 2 * D_EXPERT),
                nb, tm, 1024)
    ys = _gmm2(meta, act, prm['w_exp2'][l], prm['b_exp2'][l].reshape(N_EXPERTS, 1, D_MODEL),
               nb, tm)
    return _combine(dest, h1, top_g, row(prm['ln2_g'][l]), row(prm['ln2_b'][l]),
                    ys, tb)


def kernel(x, ln_in_g, ln_in_b, w_in, conv_w, conv_b, w_rgate, b_rgate, w_igate, b_igate, lru_lambda, shift_mu, w0, rw_decay_up, a0, rw_aaa_up, rw_gate_up, k_k, k_a, r_k, gn_g, gn_b, w_out, ln1_g, ln1_b, w_router, b_router, w_exp1, b_exp1, w_exp2, b_exp2, ln2_g, ln2_b):
    bsz, t, d = x.shape
    prm = dict(ln_in_g=ln_in_g, ln_in_b=ln_in_b, w_in=w_in, conv_w=conv_w, conv_b=conv_b,
               w_rgate=w_rgate, b_rgate=b_rgate, w_igate=w_igate, b_igate=b_igate,
               lru_lambda=lru_lambda, shift_mu=shift_mu, w0=w0, rw_decay_up=rw_decay_up, a0=a0,
               rw_aaa_up=rw_aaa_up, rw_gate_up=rw_gate_up, k_k=k_k, k_a=k_a, r_k=r_k, gn_g=gn_g,
               gn_b=gn_b, w_out=w_out, ln1_g=ln1_g, ln1_b=ln1_b, w_router=w_router,
               b_router=b_router, w_exp1=w_exp1, b_exp1=b_exp1, w_exp2=w_exp2, b_exp2=b_exp2,
               ln2_g=ln2_g, ln2_b=ln2_b)
    out = _layer(x.reshape(bsz * t, d), 0, prm, bsz, t)
    return out.reshape(bsz, t, d)
```
